```python
import jax, jax.numpy as jnp
from jax import lax
import numpy as np

D_MODEL = 1024
BATCH = 8
SEQ = 4096
DEPTH = 2

D_A = 512
CONV_WIDTH = 3
H_B = 4
DK_B = 128
DV_B = 128
D_BK = H_B * DK_B
D_BV = H_B * DV_B
HGRN_CHUNK = 16
G_C = 4
GC_CH = 128
D_C = G_C * GC_CH
SGU_CHUNK = 128
D_FF = 2816
N_EXPERTS = 8
TOP_K = 2
D_FF_EXPERT = 2816
N_DENSE = (DEPTH + 1) // 2
N_MOE = DEPTH // 2
N_IN = 3 * D_A + 2 * D_BK + 2 * D_BV + 2 * D_C + 3 * D_MODEL
EPS = 1e-6
LB_FLOOR = 1e-30

kernel_name = "hybrid_conv_hgrn2_gmlp_moe_adaln"


def rmsnorm(x, w):
    xf = x.astype(jnp.float32)
    y = xf * lax.rsqrt(jnp.mean(xf * xf, axis=-1, keepdims=True) + EPS)
    return (y * w.astype(jnp.float32)).astype(x.dtype)


def short_conv_mixer(gate_b, gate_c, h, conv_w):
    z = gate_c * h
    seq = z.shape[1]
    zp = jnp.pad(z, ((0, 0), (CONV_WIDTH - 1, 0), (0, 0)))
    y = conv_w[0] * zp[:, CONV_WIDTH - 1:CONV_WIDTH - 1 + seq]
    for k in range(1, CONV_WIDTH):
        y = y + conv_w[k] * zp[:, CONV_WIDTH - 1 - k:CONV_WIDTH - 1 - k + seq]
    return gate_b * y


def hgrn2_mixer(q, f_raw, i, g, lb, norm_w):
    bsz, seq, _ = q.shape
    nc = seq // HGRN_CHUNK
    f32 = jnp.float32
    fr = f_raw.astype(f32)
    lb = lb.astype(f32)
    log_f = jnp.logaddexp(jnp.log(jnp.maximum(lb, LB_FLOOR)), jnp.log1p(-lb) + jax.nn.log_sigmoid(fr))
    k = (1.0 - lb) * jax.nn.sigmoid(-fr)
    qf = jax.nn.silu(q.astype(f32))
    v = i.astype(f32)

    def to_chunks(t, d):
        return t.reshape(bsz, nc, HGRN_CHUNK, H_B, d).transpose(1, 0, 3, 2, 4)

    xs = (to_chunks(qf, DK_B), to_chunks(k, DK_B), to_chunks(v, DV_B), to_chunks(log_f, DK_B))
    causal = jnp.tril(jnp.ones((HGRN_CHUNK, HGRN_CHUNK), dtype=bool))[:, :, None]

    def chunk_step(state, inp):
        qc, kc, vc, lc = inp
        b = jnp.cumsum(lc, axis=2)
        diff = b[:, :, :, None, :] - b[:, :, None, :, :]
        decay = jnp.where(causal, jnp.exp(jnp.where(causal, diff, 0.0)), 0.0)
        scores = jnp.einsum('bhtd,bhsd,bhtsd->bhts', qc, kc, decay)
        o = (jnp.einsum('bhts,bhsv->bhtv', scores, vc)
             + jnp.einsum('bhtd,bhdv->bhtv', qc * jnp.exp(b), state))
        b_last = b[:, :, -1:, :]
        new_state = (jnp.exp(b_last[:, :, 0, :])[..., None] * state
                     + jnp.einsum('bhsd,bhsv->bhdv', kc * jnp.exp(b_last - b), vc))
        return new_state, o

    state0 = jnp.zeros((bsz, H_B, DK_B, DV_B), f32)
    _, o = lax.scan(chunk_step, state0, xs)
    o = o.transpose(1, 0, 3, 2, 4).reshape(bsz, seq, H_B, DV_B)
    o = o * lax.rsqrt(jnp.mean(o * o, axis=-1, keepdims=True) + EPS) * norm_w.astype(f32).reshape(H_B, DV_B)
    o = o.reshape(bsz, seq, D_BV) * jax.nn.silu(g.astype(f32))
    return o.astype(q.dtype)


def chunked_sgu(u, v, norm_w, w_s, b_s):
    bsz, seq, _ = u.shape
    v = rmsnorm(v, norm_w)
    vg = v.reshape(bsz, seq // SGU_CHUNK, SGU_CHUNK, G_C, GC_CH)
    tril = jnp.tril(jnp.ones((SGU_CHUNK, SGU_CHUNK), dtype=w_s.dtype))
    mixed = jnp.einsum('gts,bnsgc->bntgc', w_s * tril, vg) + b_s.T[None, None, :, :, None]
    return u * mixed.reshape(bsz, seq, D_C)


def hybrid_mixer(h, lb, w_in, conv_w, hgrn_norm_w, sgu_norm_w, sgu_w, sgu_b, w_br_a, w_br_b, w_br_c, w_o):
    sizes = (D_A, D_A, D_A, D_BK, D_BK, D_BV, D_BV, D_C, D_C, D_MODEL, D_MODEL, D_MODEL)
    split_idx = np.cumsum(sizes)[:-1].tolist()
    proj = h @ w_in
    (a_b, a_c, a_h, b_q, b_f, b_i, b_g, c_u, c_v, gate_a, gate_b, gate_c) = jnp.split(proj, split_idx, axis=-1)
    y_a = short_conv_mixer(a_b, a_c, a_h, conv_w) @ w_br_a
    y_b = hgrn2_mixer(b_q, b_f, b_i, b_g, lb, hgrn_norm_w) @ w_br_b
    y_c = chunked_sgu(c_u, c_v, sgu_norm_w, sgu_w, sgu_b) @ w_br_c
    merged = jax.nn.sigmoid(gate_a) * y_a + jax.nn.sigmoid(gate_b) * y_b + jax.nn.sigmoid(gate_c) * y_c
    return merged @ w_o


def swiglu(h, w1, w3, w2):
    return (jax.nn.silu(h @ w1) * (h @ w3)) @ w2


def moe_swiglu(h, router_w, router_b, w1, w3, w2):
    logits = (h @ router_w).astype(jnp.float32) + router_b.astype(jnp.float32)
    top_v, top_i = lax.top_k(logits, TOP_K)
    top_p = jax.nn.softmax(top_v, axis=-1)
    gates = jnp.sum(jax.nn.one_hot(top_i, N_EXPERTS, dtype=jnp.float32) * top_p[..., None], axis=-2)
    gates = gates.astype(h.dtype)
    out = jnp.zeros_like(h)
    for e in range(N_EXPERTS):
        out = out + gates[..., e:e + 1] * swiglu(h, w1[e], w3[e], w2[e])
    return out


def setup_inputs(seed: int = 0) -> dict:
    key = jax.random.key(seed)
    ks = jax.random.split(key, 32)
    f32 = jnp.float32

    def nrm(k, shape, scale):
        return jax.random.normal(k, shape, f32) * scale

    return {
        "x": nrm(ks[0], (BATCH, SEQ, D_MODEL), 1.0),
        "c": nrm(ks[1], (BATCH, D_MODEL), 1.0),
        "ada_w": nrm(ks[2], (DEPTH, D_MODEL, 6 * D_MODEL), D_MODEL ** -0.5),
        "ada_b": nrm(ks[3], (DEPTH, 6 * D_MODEL), 0.02),
        "norm_mix_w": 1.0 + nrm(ks[4], (DEPTH, D_MODEL), 0.02),
        "norm_ffn_w": 1.0 + nrm(ks[5], (DEPTH, D_MODEL), 0.02),
        "w_in": nrm(ks[6], (DEPTH, D_MODEL, N_IN), D_MODEL ** -0.5),
        "conv_w": nrm(ks[7], (DEPTH, CONV_WIDTH, D_A), CONV_WIDTH ** -0.5),
        "hgrn_lb_logits": nrm(ks[8], (DEPTH, D_BK), 0.5),
        "hgrn_norm_w": 1.0 + nrm(ks[9], (DEPTH, D_BV), 0.02),
        "sgu_norm_w": 1.0 + nrm(ks[10], (DEPTH, D_C), 0.02),
        "sgu_w": nrm(ks[11], (DEPTH, G_C, SGU_CHUNK, SGU_CHUNK), SGU_CHUNK ** -0.5),
        "sgu_b": 1.0 + nrm(ks[12], (DEPTH, G_C, SGU_CHUNK), 0.1),
        "w_br_a": nrm(ks[13], (DEPTH, D_A, D_MODEL), D_A ** -0.5),
        "w_br_b": nrm(ks[14], (DEPTH, D_BV, D_MODEL), D_BV ** -0.5),
        "w_br_c": nrm(ks[15], (DEPTH, D_C, D_MODEL), D_C ** -0.5),
        "w_o": nrm(ks[16], (DEPTH, D_MODEL, D_MODEL), D_MODEL ** -0.5),
        "ffn_w1": nrm(ks[17], (N_DENSE, D_MODEL, D_FF), D_MODEL ** -0.5),
        "ffn_w3": nrm(ks[18], (N_DENSE, D_MODEL, D_FF), D_MODEL ** -0.5),
        "ffn_w2": nrm(ks[19], (N_DENSE, D_FF, D_MODEL), D_FF ** -0.5),
        "moe_router_w": nrm(ks[20], (N_MOE, D_MODEL, N_EXPERTS), D_MODEL ** -0.5),
        "moe_router_b": nrm(ks[21], (N_MOE, N_EXPERTS), 0.01),
        "moe_w1": nrm(ks[22], (N_MOE, N_EXPERTS, D_MODEL, D_FF_EXPERT), D_MODEL ** -0.5),
        "moe_w3": nrm(ks[23], (N_MOE, N_EXPERTS, D_MODEL, D_FF_EXPERT), D_MODEL ** -0.5),
        "moe_w2": nrm(ks[24], (N_MOE, N_EXPERTS, D_FF_EXPERT, D_MODEL), D_FF_EXPERT ** -0.5),
        "final_norm_w": 1.0 + nrm(ks[25], (D_MODEL,), 0.02),
    }


def reference(x, c, ada_w, ada_b, norm_mix_w, norm_ffn_w, w_in, conv_w, hgrn_lb_logits, hgrn_norm_w,
              sgu_norm_w, sgu_w, sgu_b, w_br_a, w_br_b, w_br_c, w_o, ffn_w1, ffn_w3, ffn_w2,
              moe_router_w, moe_router_b, moe_w1, moe_w3, moe_w2, final_norm_w):
    p = jax.nn.softmax(hgrn_lb_logits.astype(jnp.float32), axis=0)
    lb_all = jnp.clip(jnp.cumsum(p, axis=0) - p[0:1], 0.0, 1.0)
    c_act = jax.nn.silu(c)
    for l in range(DEPTH):
        mod = (c_act @ ada_w[l] + ada_b[l])[:, None, :]
        sh1, sc1, g1, sh2, sc2, g2 = jnp.split(mod, 6, axis=-1)
        h = rmsnorm(x, norm_mix_w[l]) * (1.0 + sc1) + sh1
        x = x + g1 * hybrid_mixer(h, lb_all[l], w_in[l], conv_w[l], hgrn_norm_w[l], sgu_norm_w[l], sgu_w[l],
                                  sgu_b[l], w_br_a[l], w_br_b[l], w_br_c[l], w_o[l])
        h = rmsnorm(x, norm_ffn_w[l]) * (1.0 + sc2) + sh2
        j = l // 2
        if l % 2 == 0:
            ffn = swiglu(h, ffn_w1[j], ffn_w3[j], ffn_w2[j])
        else:
            ffn = moe_swiglu(h, moe_router_w[j], moe_router_b[j], moe_w1[j], moe_w3[j], moe_w2[j])
        x = x + g2 * ffn
    return rmsnorm(x, final_norm_w)
```

```python
import functools

import numpy as np
import jax
import jax.numpy as jnp
from jax import lax
from jax.experimental import pallas as pl
from jax.experimental.pallas import tpu as pltpu

F32 = jnp.float32
BF16 = jnp.bfloat16

EPS = 1e-6
LB_FLOOR = 1e-30

D_A = 512
CONV_WIDTH = 3
H_B = 4
DK_B = 128
DV_B = 128
D_BK = H_B * DK_B
D_BV = H_B * DV_B
G_C = 4
GC_CH = 128
D_C = G_C * GC_CH
SGU_CHUNK = 128
N_EXPERTS = 8

OFF_AB, OFF_AC, OFF_AH = 0, D_A, 2 * D_A
OFF_BQ = 3 * D_A
OFF_BF = OFF_BQ + D_BK
OFF_BI = OFF_BF + D_BK
OFF_BG = OFF_BI + D_BV
OFF_CU = OFF_BG + D_BV
OFF_CV = OFF_CU + D_C
OFF_GATE = OFF_CV + D_C

CHUNK = 128
N_LEVELS = 7
assert 1 << N_LEVELS == CHUNK and CHUNK == SGU_CHUNK

LANES = 128
MXU_N = 256
VMEM_LIMIT = 60 * 1024 * 1024
NEG_BIG = -1e30


def _sigmoid(v):
    return 1.0 / (1.0 + jnp.exp(-v))


def _silu(v):
    return v * _sigmoid(v)


def _dot(a, b):
    return jnp.dot(a, b, preferred_element_type=F32)


def _dot_nt(a, b):
    return lax.dot_general(a, b, (((1,), (1,)), ((), ())), preferred_element_type=F32)


def _dot_tn(a, b):
    return lax.dot_general(a, b, (((0,), (0,)), ((), ())), preferred_element_type=F32)


def _norm_mod(x, nw, shift, scale):
    ms = jnp.mean(x * x, axis=-1, keepdims=True)
    return (x * lax.rsqrt(ms + EPS) * nw) * (1.0 + scale) + shift


def _params(*sem):
    return pltpu.CompilerParams(dimension_semantics=sem, vmem_limit_bytes=VMEM_LIMIT)


def _adaln_kernel(c_ref, w_ref, b_ref, o_ref):
    ca = _silu(c_ref[...]).astype(BF16)
    o_ref[0] = _dot(ca, w_ref[0].astype(BF16)) + b_ref[0]


def _adaln(c, ada_w, ada_b):
    depth, d, n = ada_w.shape
    bsz = c.shape[0]
    tn = n // 4
    return pl.pallas_call(
        _adaln_kernel,
        grid=(depth, n // tn),
        in_specs=[
            pl.BlockSpec((bsz, d), lambda l, j: (0, 0)),
            pl.BlockSpec((1, d, tn), lambda l, j: (l, 0, j)),
            pl.BlockSpec((1, 1, tn), lambda l, j: (l, 0, j)),
        ],
        out_specs=pl.BlockSpec((1, bsz, tn), lambda l, j: (l, 0, j)),
        out_shape=jax.ShapeDtypeStruct((depth, bsz, n), F32),
        compiler_params=_params("arbitrary", "arbitrary"),
        name="adaln",
    )(c, ada_w, ada_b.reshape(depth, 1, n))


def _inproj_kernel(x_ref, mod_ref, nw_ref, w_ref, o_ref, h_ref):
    @pl.when(pl.program_id(2) == 0)
    def _():
        h = _norm_mod(x_ref[0], nw_ref[...], mod_ref[0, 0:1, :], mod_ref[0, 1:2, :])
        h_ref[...] = h.astype(BF16)

    o_ref[0] = _dot(h_ref[...], w_ref[...]).astype(o_ref.dtype)


def _inproj(x, mod, nw, w_bf16, tm, tn):
    bsz, seq, d = x.shape
    n = w_bf16.shape[1]
    return pl.pallas_call(
        _inproj_kernel,
        grid=(bsz, seq // tm, n // tn),
        in_specs=[
            pl.BlockSpec((1, tm, d), lambda b, i, j: (b, i, 0)),
            pl.BlockSpec((1, 6, d), lambda b, i, j: (b, 0, 0)),
            pl.BlockSpec((1, d), lambda b, i, j: (0, 0)),
            pl.BlockSpec((d, tn), lambda b, i, j: (0, j)),
        ],
        out_specs=pl.BlockSpec((1, tm, tn), lambda b, i, j: (b, i, j)),
        out_shape=jax.ShapeDtypeStruct((bsz, seq, n), BF16),
        scratch_shapes=[pltpu.VMEM((tm, d), BF16)],
        compiler_params=_params("arbitrary", "arbitrary", "arbitrary"),
        name="inproj",
    )(x, mod, nw.reshape(1, d), w_bf16)


def _level_tables():
    t = np.arange(CHUNK)
    sum_mat = np.zeros((N_LEVELS + 1, CHUNK, CHUNK), np.float32)
    sum_mat[0] = (t[None, :] <= t[:, None])
    up = np.zeros((N_LEVELS + 1, CHUNK, LANES), np.float32)
    low = np.zeros((N_LEVELS + 1, CHUNK, LANES), np.float32)
    pair = np.zeros((N_LEVELS + 1, CHUNK, CHUNK), np.float32)
    pair[0] = np.eye(CHUNK)
    for j in range(1, N_LEVELS + 1):
        blk, half = 1 << j, 1 << (j - 1)
        base = (t // blk) * blk
        m = base + half - 1
        is_up = (t - base) >= half
        u = t[None, :]
        upper_rows = (u > m[:, None]) & (u <= t[:, None])
        lower_rows = (u > t[:, None]) & (u <= m[:, None])
        sum_mat[j] = np.where(is_up[:, None], upper_rows, lower_rows)
        up[j] = is_up[:, None]
        low[j] = ~is_up[:, None]
        pair[j] = (t[:, None] // blk) == (t[None, :] // blk)
    return (jnp.asarray(sum_mat.reshape((N_LEVELS + 1) * CHUNK, CHUNK), BF16),
            jnp.asarray(up), jnp.asarray(low), jnp.asarray(pair))


def _mixer_kernel(layer, ts,
                  proj_ref, x_ref, mod_ref, convw_ref, lbl_ref, hnw_ref, snw_ref, sguw_ref, sgub_ref,
                  summat_ref, up_ref, low_ref, pair_ref, wa_ref, wb_ref, wc_ref, wo_ref,
                  o_ref,
                  zbuf, st_ref, lf_ref, qs_ref, kk_ref, vn_ref, yb_ref, yc_ref, wsm_ref):
    n_chunks = ts // CHUNK

    @pl.when(pl.program_id(1) == 0)
    def _():
        zbuf[0:8, :] = jnp.zeros((8, D_A), F32)
        st_ref[...] = jnp.zeros_like(st_ref)

    def cols(off, width):
        return proj_ref[0, :, off:off + width].astype(F32)

    z = cols(OFF_AC, D_A) * cols(OFF_AH, D_A)
    zbuf[8:8 + ts, :] = z
    y = (convw_ref[0:1, :] * z + convw_ref[1:2, :] * zbuf[7:7 + ts, :]
         + convw_ref[2:3, :] * zbuf[6:6 + ts, :])
    ya = (cols(OFF_AB, D_A) * y).astype(BF16)
    zbuf[0:8, :] = zbuf[ts:ts + 8, :]

    lg = lbl_ref[...]
    pe = jnp.exp(lg - jnp.max(lg, axis=0, keepdims=True))
    p = pe / jnp.sum(pe, axis=0, keepdims=True)
    lb = jnp.clip(jnp.sum(p[0:layer + 1], axis=0, keepdims=True) - p[0:1], 0.0, 1.0)
    log_lb = jnp.log(jnp.maximum(lb, LB_FLOOR))
    log_1m = jnp.log(1.0 - lb)
    fr = cols(OFF_BF, D_BK)
    log_sig = jnp.minimum(fr, 0.0) - jnp.log(1.0 + jnp.exp(-jnp.abs(fr)))
    c2 = log_1m + log_sig
    lf_ref[...] = jnp.maximum(log_lb, c2) + jnp.log(1.0 + jnp.exp(-jnp.abs(log_lb - c2)))
    kk_ref[...] = (1.0 - lb) * _sigmoid(-fr)
    qs_ref[...] = _silu(cols(OFF_BQ, D_BK))

    cv = cols(OFF_CV, D_C)
    vn = cv * lax.rsqrt(jnp.mean(cv * cv, axis=-1, keepdims=True) + EPS) * snw_ref[...]
    vn_ref[...] = vn.astype(BF16)
    tril = (lax.broadcasted_iota(jnp.int32, (SGU_CHUNK, SGU_CHUNK), 0)
            >= lax.broadcasted_iota(jnp.int32, (SGU_CHUNK, SGU_CHUNK), 1))
    for g in range(G_C):
        wsm_ref[g] = jnp.where(tril, sguw_ref[g], 0.0).astype(BF16)

    def chunk_body(c, carry):
        r0 = pl.multiple_of(c * CHUNK, CHUNK)
        rows = pl.ds(r0, CHUNK)

        for g in range(G_C):
            cs = slice(g * GC_CH, (g + 1) * GC_CH)
            mixed = _dot(wsm_ref[g], vn_ref[rows, cs]) + sgub_ref[g]
            u = proj_ref[0, rows, OFF_CU + g * GC_CH:OFF_CU + (g + 1) * GC_CH].astype(F32)
            yc_ref[rows, cs] = (u * mixed).astype(BF16)

        lf_c = lf_ref[rows, :]
        lf_hi = lf_c.astype(BF16)
        lf_lo = (lf_c - lf_hi.astype(F32)).astype(BF16)
        lv = _dot(summat_ref[...], lf_hi) + _dot(summat_ref[...], lf_lo)

        for h in range(H_B):
            cs = slice(h * DK_B, (h + 1) * DK_B)
            b_h = lv[0:CHUNK, cs]
            q_h = qs_ref[rows, cs]
            k_h = kk_ref[rows, cs]
            v_h = proj_ref[0, rows, OFF_BI + h * DV_B:OFF_BI + (h + 1) * DV_B]
            st = st_ref[h]
            o = _dot_nt((q_h * jnp.exp(b_h)).astype(BF16), st.astype(BF16))
            scores = pair_ref[0] * _dot_nt(q_h.astype(BF16), k_h.astype(BF16))
            for j in range(1, N_LEVELS + 1):
                e = jnp.exp(lv[j * CHUNK:(j + 1) * CHUNK, cs])
                q_j = (q_h * e * up_ref[j]).astype(BF16)
                k_j = (k_h * e * low_ref[j]).astype(BF16)
                scores = scores + pair_ref[j] * _dot_nt(q_j, k_j)
            o = o + _dot(scores.astype(BF16), v_h)
            b_last = b_h[CHUNK - 1:CHUNK, :]
            k_dec = (k_h * jnp.exp(b_last - b_h)).astype(BF16)
            st_ref[h] = st * jnp.exp(b_last) + _dot_tn(v_h, k_dec)
            on = o * lax.rsqrt(jnp.mean(o * o, axis=-1, keepdims=True) + EPS) * hnw_ref[:, cs]
            g_h = proj_ref[0, rows, OFF_BG + h * DV_B:OFF_BG + (h + 1) * DV_B].astype(F32)
            yb_ref[rows, cs] = (on * _silu(g_h)).astype(BF16)
        return carry

    lax.fori_loop(0, n_chunks, chunk_body, 0)

    d = x_ref.shape[-1]
    merged = (_sigmoid(cols(OFF_GATE, d)) * _dot(ya, wa_ref[...])
              + _sigmoid(cols(OFF_GATE + d, d)) * _dot(yb_ref[...], wb_ref[...])
              + _sigmoid(cols(OFF_GATE + 2 * d, d)) * _dot(yc_ref[...], wc_ref[...]))
    o_ref[0] = x_ref[0] + mod_ref[0, 2:3, :] * _dot(merged.astype(BF16), wo_ref[...])


def _mixer(layer, proj, x, mod, conv_w, lb_logits, hgrn_nw, sgu_nw, sgu_w, sgu_b, tables,
           wa, wb, wc, wo, ts):
    bsz, seq, d = x.shape
    n_in = proj.shape[-1]
    sum_mat, up, low, pair = tables
    sgub = jnp.broadcast_to(sgu_b[:, :, None], (G_C, SGU_CHUNK, GC_CH))
    const2 = lambda b, i: (0, 0)
    const3 = lambda b, i: (0, 0, 0)
    return pl.pallas_call(
        functools.partial(_mixer_kernel, layer, ts),
        grid=(bsz, seq // ts),
        in_specs=[
            pl.BlockSpec((1, ts, n_in), lambda b, i: (b, i, 0)),
            pl.BlockSpec((1, ts, d), lambda b, i: (b, i, 0)),
            pl.BlockSpec((1, 6, d), lambda b, i: (b, 0, 0)),
            pl.BlockSpec(conv_w.shape, const2),
            pl.BlockSpec(lb_logits.shape, const2),
            pl.BlockSpec((1, D_BV), const2),
            pl.BlockSpec((1, D_C), const2),
            pl.BlockSpec(sgu_w.shape, const3),
            pl.BlockSpec(sgub.shape, const3),
            pl.BlockSpec(sum_mat.shape, const2),
            pl.BlockSpec(up.shape, const3),
            pl.BlockSpec(low.shape, const3),
            pl.BlockSpec(pair.shape, const3),
            pl.BlockSpec(wa.shape, const2),
            pl.BlockSpec(wb.shape, const2),
            pl.BlockSpec(wc.shape, const2),
            pl.BlockSpec(wo.shape, const2),
        ],
        out_specs=pl.BlockSpec((1, ts, d), lambda b, i: (b, i, 0)),
        out_shape=jax.ShapeDtypeStruct((bsz, seq, d), F32),
        scratch_shapes=[
            pltpu.VMEM((ts + 8, D_A), F32),
            pltpu.VMEM((H_B, DV_B, DK_B), F32),
            pltpu.VMEM((ts, D_BK), F32),
            pltpu.VMEM((ts, D_BK), F32),
            pltpu.VMEM((ts, D_BK), F32),
            pltpu.VMEM((ts, D_C), BF16),
            pltpu.VMEM((ts, D_BV), BF16),
            pltpu.VMEM((ts, D_C), BF16),
            pltpu.VMEM((G_C, SGU_CHUNK, SGU_CHUNK), BF16),
        ],
        compiler_params=_params("arbitrary", "arbitrary"),
        name="mixer",
    )(proj, x, mod, conv_w, lb_logits, hgrn_nw.reshape(1, D_BV), sgu_nw.reshape(1, D_C), sgu_w, sgub,
      sum_mat, up, low, pair, wa, wb, wc, wo)


def _swiglu_tile(h, w1_at, w3_at, w2, g_ref):
    d_ff = g_ref.shape[1]
    for c0 in range(0, d_ff, MXU_N):
        a = _dot(h, w1_at(c0))
        g_ref[:, c0:c0 + MXU_N] = (_silu(a) * _dot(h, w3_at(c0))).astype(BF16)
    return _dot(g_ref[...], w2)


def _final_norm(v, fw):
    return v * lax.rsqrt(jnp.mean(v * v, axis=-1, keepdims=True) + EPS) * fw


def _ffn_kernel(final, x_ref, mod_ref, nw_ref, fw_ref, w1_ref, w3_ref, w2_ref, o_ref, g_ref):
    x = x_ref[0]
    h = _norm_mod(x, nw_ref[...], mod_ref[0, 3:4, :], mod_ref[0, 4:5, :]).astype(BF16)
    y = _swiglu_tile(h, lambda c0: w1_ref[:, c0:c0 + MXU_N], lambda c0: w3_ref[:, c0:c0 + MXU_N],
                     w2_ref[...], g_ref)
    out = x + mod_ref[0, 5:6, :] * y
    o_ref[0] = _final_norm(out, fw_ref[...]) if final else out


def _ffn(x, mod, nw, fw, w1, w3, w2, tm, final):
    bsz, seq, d = x.shape
    d_ff = w1.shape[1]
    const2 = lambda b, i: (0, 0)
    resident = dict(pipeline_mode=pl.Buffered(1))
    return pl.pallas_call(
        functools.partial(_ffn_kernel, final),
        grid=(bsz, seq // tm),
        in_specs=[
            pl.BlockSpec((1, tm, d), lambda b, i: (b, i, 0)),
            pl.BlockSpec((1, 6, d), lambda b, i: (b, 0, 0)),
            pl.BlockSpec((1, d), const2),
            pl.BlockSpec((1, d), const2),
            pl.BlockSpec((d, d_ff), const2, **resident),
            pl.BlockSpec((d, d_ff), const2, **resident),
            pl.BlockSpec((d_ff, d), const2, **resident),
        ],
        out_specs=pl.BlockSpec((1, tm, d), lambda b, i: (b, i, 0)),
        out_shape=jax.ShapeDtypeStruct((bsz, seq, d), F32),
        scratch_shapes=[pltpu.VMEM((tm, d_ff), BF16)],
        compiler_params=_params("arbitrary", "arbitrary"),
        name="ffn",
    )(x, mod, nw.reshape(1, d), fw.reshape(1, d), w1, w3, w2)


META_E, META_P, META_R = 0, 2, 4


def _router_kernel(x_ref, mod_ref, nw_ref, rw_ref, rb_ref, tri_ref, h_ref, meta_ref, cnt_ref, carry):
    first = jnp.logical_and(pl.program_id(0) == 0, pl.program_id(1) == 0)

    @pl.when(first)
    def _():
        carry[...] = jnp.zeros_like(carry)

    h = _norm_mod(x_ref[0], nw_ref[...], mod_ref[0, 3:4, :], mod_ref[0, 4:5, :])
    h_ref[0] = h
    logits = jnp.dot(h, rw_ref[...], preferred_element_type=F32,
                     precision=lax.Precision.HIGHEST) + rb_ref[...]
    lane = lax.broadcasted_iota(jnp.int32, logits.shape, 1)
    m1 = jnp.max(logits, axis=-1, keepdims=True)
    i1 = jnp.min(jnp.where(logits == m1, lane, LANES), axis=-1, keepdims=True)
    rest = jnp.where(lane == i1, -jnp.inf, logits)
    m2 = jnp.max(rest, axis=-1, keepdims=True)
    i2 = jnp.min(jnp.where(rest == m2, lane, LANES), axis=-1, keepdims=True)
    e2 = jnp.exp(m2 - m1)
    p1 = 1.0 / (1.0 + e2)
    p2 = e2 / (1.0 + e2)
    sel1 = lane == i1
    sel2 = lane == i2
    onehot = jnp.where(jnp.logical_or(sel1, sel2), 1.0, 0.0)
    before = _dot(tri_ref[...], onehot.astype(BF16)) + carry[...]
    r1 = jnp.sum(jnp.where(sel1, before, 0.0), axis=-1, keepdims=True)
    r2 = jnp.sum(jnp.where(sel2, before, 0.0), axis=-1, keepdims=True)
    carry[...] = carry[...] + jnp.sum(onehot, axis=0, keepdims=True)
    cnt_ref[...] = carry[...]
    rec = jnp.where(lane == META_E, i1.astype(F32), 0.0)
    rec = jnp.where(lane == META_E + 1, i2.astype(F32), rec)
    rec = jnp.where(lane == META_P, p1, rec)
    rec = jnp.where(lane == META_P + 1, p2, rec)
    rec = jnp.where(lane == META_R, r1, rec)
    rec = jnp.where(lane == META_R + 1, r2, rec)
    meta_ref[0] = rec


def _router(x, mod, nw, router_w, router_b, tm):
    bsz, seq, d = x.shape
    ne = router_w.shape[1]
    rw = jnp.zeros((d, LANES), F32).at[:, :ne].set(router_w)
    rb = jnp.full((1, LANES), NEG_BIG, F32).at[0, :ne].set(router_b)
    tri = jnp.asarray(np.tril(np.ones((tm, tm), np.float32), -1), BF16)
    const2 = lambda b, i: (0, 0)
    return pl.pallas_call(
        _router_kernel,
        grid=(bsz, seq // tm),
        in_specs=[
            pl.BlockSpec((1, tm, d), lambda b, i: (b, i, 0)),
            pl.BlockSpec((1, 6, d), lambda b, i: (b, 0, 0)),
            pl.BlockSpec((1, d), const2),
            pl.BlockSpec((d, LANES), const2),
            pl.BlockSpec((1, LANES), const2),
            pl.BlockSpec((tm, tm), const2),
        ],
        out_specs=[
            pl.BlockSpec((1, tm, d), lambda b, i: (b, i, 0)),
            pl.BlockSpec((1, tm, LANES), lambda b, i: (b, i, 0)),
            pl.BlockSpec((1, LANES), const2),
        ],
        out_shape=[
            jax.ShapeDtypeStruct((bsz, seq, d), F32),
            jax.ShapeDtypeStruct((bsz, seq, LANES), F32),
            jax.ShapeDtypeStruct((1, LANES), F32),
        ],
        scratch_shapes=[pltpu.VMEM((1, LANES), F32)],
        compiler_params=_params("arbitrary", "arbitrary"),
        name="router",
    )(x, mod, nw.reshape(1, d), rw, rb, tri)


def _row_copy(src, src_row, dst, dst_row, sem):
    return pltpu.make_async_copy(src.at[pl.ds(src_row, 1), :], dst.at[pl.ds(dst_row, 1), :], sem)


def _dispatch_kernel(tt, pos_ref, h_hbm, xs_hbm, sem):
    base = pl.program_id(0) * tt

    def issue(t, carry):
        _row_copy(h_hbm, base + t, xs_hbm, pos_ref[0, 0, 2 * t], sem).start()
        _row_copy(h_hbm, base + t, xs_hbm, pos_ref[0, 0, 2 * t + 1], sem).start()
        return carry

    def drain(t, carry):
        _row_copy(h_hbm, base + t, xs_hbm, pos_ref[0, 0, 2 * t], sem).wait()
        _row_copy(h_hbm, base + t, xs_hbm, pos_ref[0, 0, 2 * t + 1], sem).wait()
        return carry

    lax.fori_loop(0, tt, issue, 0)
    lax.fori_loop(0, tt, drain, 0)


def _dispatch(h2d, pos, tt):
    n_tok, d = h2d.shape
    return pl.pallas_call(
        functools.partial(_dispatch_kernel, tt),
        grid=(n_tok // tt,),
        in_specs=[
            pl.BlockSpec((1, 1, 2 * tt), lambda i: (i, 0, 0), memory_space=pltpu.SMEM),
            pl.BlockSpec(memory_space=pl.ANY),
        ],
        out_specs=pl.BlockSpec(memory_space=pl.ANY),
        out_shape=jax.ShapeDtypeStruct((2 * n_tok, d), F32),
        scratch_shapes=[pltpu.SemaphoreType.DMA(())],
        compiler_params=_params("arbitrary"),
        name="dispatch",
    )(pos.reshape(n_tok // tt, 1, 2 * tt), h2d)


def _expert_kernel(tm, tile_ref, exp_ref, lo_ref, hi_ref, first_ref, valid_ref,
                   xs_ref, w1_ref, w3_ref, w2_ref, y_ref, g_ref):
    s = pl.program_id(0)

    @pl.when(valid_ref[s] == 1)
    def _():
        h = xs_ref[...].astype(BF16)
        res = _swiglu_tile(h, lambda c0: w1_ref[0, :, c0:c0 + MXU_N],
                           lambda c0: w3_ref[0, :, c0:c0 + MXU_N], w2_ref[0], g_ref)
        row = tile_ref[s] * tm + lax.broadcasted_iota(jnp.int32, (tm, 1), 0)
        mine = jnp.logical_and(row >= lo_ref[s], row < hi_ref[s])

        @pl.when(first_ref[s] == 1)
        def _():
            y_ref[...] = jnp.where(mine, res, 0.0)

        @pl.when(first_ref[s] == 0)
        def _():
            y_ref[...] = jnp.where(mine, res, y_ref[...])


def _experts(xs, steps, w1, w3, w2, tm):
    n_rows, d = xs.shape
    d_ff = w1.shape[2]
    n_steps = steps[0].shape[0]
    grid_spec = pltpu.PrefetchScalarGridSpec(
        num_scalar_prefetch=6,
        grid=(n_steps,),
        in_specs=[
            pl.BlockSpec((tm, d), lambda s, tile, exp, lo, hi, first, valid: (tile[s], 0)),
            pl.BlockSpec((1, d, d_ff), lambda s, tile, exp, lo, hi, first, valid: (exp[s], 0, 0)),
            pl.BlockSpec((1, d, d_ff), lambda s, tile, exp, lo, hi, first, valid: (exp[s], 0, 0)),
            pl.BlockSpec((1, d_ff, d), lambda s, tile, exp, lo, hi, first, valid: (exp[s], 0, 0)),
        ],
        out_specs=pl.BlockSpec((tm, d), lambda s, tile, exp, lo, hi, first, valid: (tile[s], 0)),
        scratch_shapes=[pltpu.VMEM((tm, d_ff), BF16)],
    )
    return pl.pallas_call(
        functools.partial(_expert_kernel, tm),
        grid_spec=grid_spec,
        out_shape=jax.ShapeDtypeStruct((n_rows, d), F32),
        compiler_params=_params("arbitrary"),
        name="experts",
    )(*steps, xs, w1, w3, w2)


def _combine_kernel(tt, final, pos_ref, y_hbm, x_ref, mod_ref, meta_ref, fw_ref, o_ref, buf, sem):
    def issue(t, carry):
        _row_copy(y_hbm, pos_ref[0, 0, 2 * t], buf.at[0], t, sem).start()
        _row_copy(y_hbm, pos_ref[0, 0, 2 * t + 1], buf.at[1], t, sem).start()
        return carry

    def drain(t, carry):
        _row_copy(y_hbm, pos_ref[0, 0, 2 * t], buf.at[0], t, sem).wait()
        _row_copy(y_hbm, pos_ref[0, 0, 2 * t + 1], buf.at[1], t, sem).wait()
        return carry

    lax.fori_loop(0, tt, issue, 0)
    lax.fori_loop(0, tt, drain, 0)
    p1 = meta_ref[0, :, META_P:META_P + 1]
    p2 = meta_ref[0, :, META_P + 1:META_P + 2]
    out = x_ref[0] + mod_ref[0, 5:6, :] * (p1 * buf[0] + p2 * buf[1])
    o_ref[0] = _final_norm(out, fw_ref[...]) if final else out


def _combine(y, pos, x, mod, meta, fw, tt, final):
    bsz, seq, d = x.shape
    per_b = seq // tt
    return pl.pallas_call(
        functools.partial(_combine_kernel, tt, final),
        grid=(bsz, per_b),
        in_specs=[
            pl.BlockSpec((1, 1, 2 * tt), lambda b, i: (b * per_b + i, 0, 0), memory_space=pltpu.SMEM),
            pl.BlockSpec(memory_space=pl.ANY),
            pl.BlockSpec((1, tt, d), lambda b, i: (b, i, 0)),
            pl.BlockSpec((1, 6, d), lambda b, i: (b, 0, 0)),
            pl.BlockSpec((1, tt, LANES), lambda b, i: (b, i, 0)),
            pl.BlockSpec((1, d), lambda b, i: (0, 0)),
        ],
        out_specs=pl.BlockSpec((1, tt, d), lambda b, i: (b, i, 0)),
        out_shape=jax.ShapeDtypeStruct((bsz, seq, d), F32),
        scratch_shapes=[pltpu.VMEM((2, tt, d), F32), pltpu.SemaphoreType.DMA(())],
        compiler_params=_params("arbitrary", "arbitrary"),
        name="combine",
    )(pos.reshape(bsz * per_b, 1, 2 * tt), y, x, mod, meta, fw.reshape(1, d))


def _expert_steps(counts, n_rows, tm):
    n_tiles = n_rows // tm
    n_steps = n_tiles + N_EXPERTS - 1
    ends = jnp.cumsum(counts)
    starts = ends - counts
    first_tile = starts // tm
    last_tile = (ends - 1) // tm
    tiles_e = jnp.where(counts > 0, last_tile - first_tile + 1, 0)
    step_end = jnp.cumsum(tiles_e)
    step_start = step_end - tiles_e
    total = step_end[-1]
    s = jnp.arange(n_steps, dtype=jnp.int32)
    sc = jnp.minimum(s, total - 1)
    e_s = jnp.sum((sc[:, None] >= step_end[None, :]).astype(jnp.int32), axis=1)
    tile_s = first_tile[e_s] + (sc - step_start[e_s])
    valid = (s < total).astype(jnp.int32)
    prev_tile = jnp.concatenate([jnp.full((1,), -1, jnp.int32), tile_s[:-1]])
    first = jnp.logical_and(tile_s != prev_tile, valid == 1).astype(jnp.int32)
    as_i32 = lambda v: v.astype(jnp.int32)
    return (as_i32(tile_s), as_i32(e_s), as_i32(starts[e_s]), as_i32(ends[e_s]), first, valid)


def _moe(x, mod, nw, fw, router_w, router_b, w1, w3, w2, tm_route, tt, tm_exp, final):
    bsz, seq, d = x.shape
    n_tok = bsz * seq
    h, meta, cnt = _router(x, mod, nw, router_w, router_b, tm_route)
    counts = cnt[0, :N_EXPERTS].astype(jnp.int32)
    starts = jnp.cumsum(counts) - counts
    meta2 = meta.reshape(n_tok, LANES)
    experts = meta2[:, META_E:META_E + 2].astype(jnp.int32)
    ranks = meta2[:, META_R:META_R + 2].astype(jnp.int32)
    onehot = experts[:, :, None] == jnp.arange(N_EXPERTS, dtype=jnp.int32)[None, None, :]
    pos = jnp.sum(jnp.where(onehot, starts[None, None, :], 0), axis=-1) + ranks
    xs = _dispatch(h.reshape(n_tok, d), pos, tt)
    y = _experts(xs, _expert_steps(counts, 2 * n_tok, tm_exp), w1, w3, w2, tm_exp)
    return _combine(y, pos, x, mod, meta, fw, tt, final)


def _pick(n, pref):
    t = min(n, pref)
    assert n % t == 0, (n, pref)
    return t


def kernel(x, c, ada_w, ada_b, norm_mix_w, norm_ffn_w, w_in, conv_w, hgrn_lb_logits, hgrn_norm_w,
           sgu_norm_w, sgu_w, sgu_b, w_br_a, w_br_b, w_br_c, w_o, ffn_w1, ffn_w3, ffn_w2,
           moe_router_w, moe_router_b, moe_w1, moe_w3, moe_w2, final_norm_w):
    depth = ada_w.shape[0]
    bsz, seq, d = x.shape
    n_in = w_in.shape[-1]
    assert n_in == OFF_GATE + 3 * d and seq % CHUNK == 0
    assert ffn_w1.shape[-1] % MXU_N == 0 and moe_w1.shape[-1] % MXU_N == 0

    tm_in = _pick(seq, 1024)
    tn_in = n_in // 5 if (n_in // 5) % LANES == 0 and n_in % 5 == 0 else n_in
    ts = _pick(seq, 512)
    tm_ffn = _pick(seq, 512)
    tm_route = _pick(seq, 1024)
    tt = _pick(seq, 256)
    tm_exp = _pick(2 * bsz * seq, 512)

    mod_all = _adaln(c, ada_w, ada_b).reshape(depth, bsz, 6, d)
    tables = _level_tables()
    bf = lambda w: w.astype(BF16)

    for l in range(depth):
        mod = mod_all[l]
        proj = _inproj(x, mod, norm_mix_w[l], bf(w_in[l]), tm_in, tn_in)
        x = _mixer(l, proj, x, mod, conv_w[l], hgrn_lb_logits, hgrn_norm_w[l], sgu_norm_w[l], sgu_w[l],
                   sgu_b[l], tables, bf(w_br_a[l]), bf(w_br_b[l]), bf(w_br_c[l]), bf(w_o[l]), ts)
        final = l == depth - 1
        j = l // 2
        if l % 2 == 0:
            x = _ffn(x, mod, norm_ffn_w[l], final_norm_w, bf(ffn_w1[j]), bf(ffn_w3[j]), bf(ffn_w2[j]),
                     tm_ffn, final)
        else:
            x = _moe(x, mod, norm_ffn_w[l], final_norm_w, moe_router_w[j], moe_router_b[j],
                     bf(moe_w1[j]), bf(moe_w3[j]), bf(moe_w2[j]), tm_route, tt, tm_exp, final)
    return x
```

```python
import functools

import numpy as np
import jax
import jax.numpy as jnp
from jax import lax
from jax.experimental import pallas as pl
from jax.experimental.pallas import tpu as pltpu

F32 = jnp.float32
BF16 = jnp.bfloat16

EPS = 1e-6
LB_FLOOR = 1e-30

D_A = 512
CONV_WIDTH = 3
H_B = 4
DK_B = 128
DV_B = 128
D_BK = H_B * DK_B
D_BV = H_B * DV_B
G_C = 4
GC_CH = 128
D_C = G_C * GC_CH
SGU_CHUNK = 128
N_EXPERTS = 8

OFF_AB, OFF_AC, OFF_AH = 0, D_A, 2 * D_A
OFF_BQ = 3 * D_A
OFF_BF = OFF_BQ + D_BK
OFF_BI = OFF_BF + D_BK
OFF_BG = OFF_BI + D_BV
OFF_CU = OFF_BG + D_BV
OFF_CV = OFF_CU + D_C
OFF_GATE = OFF_CV + D_C

CHUNK = 128
N_LEVELS = 7
assert 1 << N_LEVELS == CHUNK and CHUNK == SGU_CHUNK

LANES = 128
MXU_N = 256
VMEM_LIMIT = 60 * 1024 * 1024
NEG_BIG = -1e30


def _sigmoid(v):
    return 1.0 / (1.0 + jnp.exp(-v))


def _silu(v):
    return v * _sigmoid(v)


def _dot(a, b):
    return jnp.dot(a, b, preferred_element_type=F32)


def _dot_nt(a, b):
    return lax.dot_general(a, b, (((1,), (1,)), ((), ())), preferred_element_type=F32)


def _dot_tn(a, b):
    return lax.dot_general(a, b, (((0,), (0,)), ((), ())), preferred_element_type=F32)


def _norm_mod(x, nw, shift, scale):
    ms = jnp.mean(x * x, axis=-1, keepdims=True)
    return (x * lax.rsqrt(ms + EPS) * nw) * (1.0 + scale) + shift


def _params(*sem):
    return pltpu.CompilerParams(dimension_semantics=sem, vmem_limit_bytes=VMEM_LIMIT)


def _adaln_kernel(c_ref, w_ref, b_ref, o_ref):
    ca = _silu(c_ref[...]).astype(BF16)
    o_ref[0] = _dot(ca, w_ref[0].astype(BF16)) + b_ref[0]


def _adaln(c, ada_w, ada_b):
    depth, d, n = ada_w.shape
    bsz = c.shape[0]
    tn = n // 4
    return pl.pallas_call(
        _adaln_kernel,
        grid=(depth, n // tn),
        in_specs=[
            pl.BlockSpec((bsz, d), lambda l, j: (0, 0)),
            pl.BlockSpec((1, d, tn), lambda l, j: (l, 0, j)),
            pl.BlockSpec((1, 1, tn), lambda l, j: (l, 0, j)),
        ],
        out_specs=pl.BlockSpec((1, bsz, tn), lambda l, j: (l, 0, j)),
        out_shape=jax.ShapeDtypeStruct((depth, bsz, n), F32),
        compiler_params=_params("arbitrary", "arbitrary"),
        name="adaln",
    )(c, ada_w, ada_b.reshape(depth, 1, n))


def _inproj_kernel(x_ref, mod_ref, nw_ref, w_ref, o_ref, h_ref):
    @pl.when(pl.program_id(2) == 0)
    def _():
        h = _norm_mod(x_ref[0], nw_ref[...], mod_ref[0, 0:1, :], mod_ref[0, 1:2, :])
        h_ref[...] = h.astype(BF16)

    o_ref[0] = _dot(h_ref[...], w_ref[...]).astype(o_ref.dtype)


def _inproj(x, mod, nw, w_bf16, tm, tn):
    bsz, seq, d = x.shape
    n = w_bf16.shape[1]
    return pl.pallas_call(
        _inproj_kernel,
        grid=(bsz, seq // tm, n // tn),
        in_specs=[
            pl.BlockSpec((1, tm, d), lambda b, i, j: (b, i, 0)),
            pl.BlockSpec((1, 6, d), lambda b, i, j: (b, 0, 0)),
            pl.BlockSpec((1, d), lambda b, i, j: (0, 0)),
            pl.BlockSpec((d, tn), lambda b, i, j: (0, j)),
        ],
        out_specs=pl.BlockSpec((1, tm, tn), lambda b, i, j: (b, i, j)),
        out_shape=jax.ShapeDtypeStruct((bsz, seq, n), BF16),
        scratch_shapes=[pltpu.VMEM((tm, d), BF16)],
        compiler_params=_params("arbitrary", "arbitrary", "arbitrary"),
        name="inproj",
    )(x, mod, nw.reshape(1, d), w_bf16)


def _level_tables():
    t = np.arange(CHUNK)
    sum_mat = np.zeros((N_LEVELS + 1, CHUNK, CHUNK), np.float32)
    sum_mat[0] = (t[None, :] <= t[:, None])
    up = np.zeros((N_LEVELS + 1, CHUNK, LANES), np.float32)
    low = np.zeros((N_LEVELS + 1, CHUNK, LANES), np.float32)
    pair = np.zeros((N_LEVELS + 1, CHUNK, CHUNK), np.float32)
    pair[0] = np.eye(CHUNK)
    for j in range(1, N_LEVELS + 1):
        blk, half = 1 << j, 1 << (j - 1)
        base = (t // blk) * blk
        m = base + half - 1
        is_up = (t - base) >= half
        u = t[None, :]
        upper_rows = (u > m[:, None]) & (u <= t[:, None])
        lower_rows = (u > t[:, None]) & (u <= m[:, None])
        sum_mat[j] = np.where(is_up[:, None], upper_rows, lower_rows)
        up[j] = is_up[:, None]
        low[j] = ~is_up[:, None]
        pair[j] = (t[:, None] // blk) == (t[None, :] // blk)
    return (jnp.asarray(sum_mat.reshape((N_LEVELS + 1) * CHUNK, CHUNK), BF16),
            jnp.asarray(up), jnp.asarray(low), jnp.asarray(pair))


def _mixer_kernel(layer, ts,
                  proj_ref, x_ref, mod_ref, convw_ref, lbl_ref, hnw_ref, snw_ref, sguw_ref, sgub_ref,
                  summat_ref, up_ref, low_ref, pair_ref, wa_ref, wb_ref, wc_ref, wo_ref,
                  o_ref,
                  zbuf, st_ref, lf_ref, qs_ref, kk_ref, vn_ref, yb_ref, yc_ref, wsm_ref):
    n_chunks = ts // CHUNK

    @pl.when(pl.program_id(1) == 0)
    def _():
        zbuf[0:8, :] = jnp.zeros((8, D_A), F32)
        st_ref[...] = jnp.zeros_like(st_ref)

    def cols(off, width):
        return proj_ref[0, :, off:off + width].astype(F32)

    z = cols(OFF_AC, D_A) * cols(OFF_AH, D_A)
    zbuf[8:8 + ts, :] = z
    y = (convw_ref[0:1, :] * z + convw_ref[1:2, :] * zbuf[7:7 + ts, :]
         + convw_ref[2:3, :] * zbuf[6:6 + ts, :])
    ya = (cols(OFF_AB, D_A) * y).astype(BF16)
    zbuf[0:8, :] = zbuf[ts:ts + 8, :]

    lg = lbl_ref[...]
    pe = jnp.exp(lg - jnp.max(lg, axis=0, keepdims=True))
    p = pe / jnp.sum(pe, axis=0, keepdims=True)
    lb = jnp.clip(jnp.sum(p[0:layer + 1], axis=0, keepdims=True) - p[0:1], 0.0, 1.0)
    log_lb = jnp.log(jnp.maximum(lb, LB_FLOOR))
    log_1m = jnp.log(1.0 - lb)
    fr = cols(OFF_BF, D_BK)
    log_sig = jnp.minimum(fr, 0.0) - jnp.log(1.0 + jnp.exp(-jnp.abs(fr)))
    c2 = log_1m + log_sig
    lf_ref[...] = jnp.maximum(log_lb, c2) + jnp.log(1.0 + jnp.exp(-jnp.abs(log_lb - c2)))
    kk_ref[...] = (1.0 - lb) * _sigmoid(-fr)
    qs_ref[...] = _silu(cols(OFF_BQ, D_BK))

    cv = cols(OFF_CV, D_C)
    vn = cv * lax.rsqrt(jnp.mean(cv * cv, axis=-1, keepdims=True) + EPS) * snw_ref[...]
    vn_ref[...] = vn.astype(BF16)
    tril = (lax.broadcasted_iota(jnp.int32, (SGU_CHUNK, SGU_CHUNK), 0)
            >= lax.broadcasted_iota(jnp.int32, (SGU_CHUNK, SGU_CHUNK), 1))
    for g in range(G_C):
        wsm_ref[g] = jnp.where(tril, sguw_ref[g], 0.0).astype(BF16)

    def chunk_body(c, carry):
        r0 = pl.multiple_of(c * CHUNK, CHUNK)
        rows = pl.ds(r0, CHUNK)

        for g in range(G_C):
            cs = slice(g * GC_CH, (g + 1) * GC_CH)
            mixed = _dot(wsm_ref[g], vn_ref[rows, cs]) + sgub_ref[g]
            u = proj_ref[0, rows, OFF_CU + g * GC_CH:OFF_CU + (g + 1) * GC_CH].astype(F32)
            yc_ref[rows, cs] = (u * mixed).astype(BF16)

        lf_c = lf_ref[rows, :]
        lf_hi = lf_c.astype(BF16)
        lf_lo = (lf_c - lf_hi.astype(F32)).astype(BF16)
        lv = _dot(summat_ref[...], lf_hi) + _dot(summat_ref[...], lf_lo)

        for h in range(H_B):
            cs = slice(h * DK_B, (h + 1) * DK_B)
            b_h = lv[0:CHUNK, cs]
            q_h = qs_ref[rows, cs]
            k_h = kk_ref[rows, cs]
            v_h = proj_ref[0, rows, OFF_BI + h * DV_B:OFF_BI + (h + 1) * DV_B]
            st = st_ref[h]
            o = _dot_nt((q_h * jnp.exp(b_h)).astype(BF16), st.astype(BF16))
            scores = pair_ref[0] * _dot_nt(q_h.astype(BF16), k_h.astype(BF16))
            for j in range(1, N_LEVELS + 1):
                e = jnp.exp(lv[j * CHUNK:(j + 1) * CHUNK, cs])
                q_j = (q_h * e * up_ref[j]).astype(BF16)
                k_j = (k_h * e * low_ref[j]).astype(BF16)
                scores = scores + pair_ref[j] * _dot_nt(q_j, k_j)
            o = o + _dot(scores.astype(BF16), v_h)
            b_last = b_h[CHUNK - 1:CHUNK, :]
            k_dec = (k_h * jnp.exp(b_last - b_h)).astype(BF16)
            st_ref[h] = st * jnp.exp(b_last) + _dot_tn(v_h, k_dec)
            on = o * lax.rsqrt(jnp.mean(o * o, axis=-1, keepdims=True) + EPS) * hnw_ref[:, cs]
            g_h = proj_ref[0, rows, OFF_BG + h * DV_B:OFF_BG + (h + 1) * DV_B].astype(F32)
            yb_ref[rows, cs] = (on * _silu(g_h)).astype(BF16)
        return carry

    lax.fori_loop(0, n_chunks, chunk_body, 0)

    d = x_ref.shape[-1]
    merged = (_sigmoid(cols(OFF_GATE, d)) * _dot(ya, wa_ref[...])
              + _sigmoid(cols(OFF_GATE + d, d)) * _dot(yb_ref[...], wb_ref[...])
              + _sigmoid(cols(OFF_GATE + 2 * d, d)) * _dot(yc_ref[...], wc_ref[...]))
    o_ref[0] = x_ref[0] + mod_ref[0, 2:3, :] * _dot(merged.astype(BF16), wo_ref[...])


def _mixer(layer, proj, x, mod, conv_w, lb_logits, hgrn_nw, sgu_nw, sgu_w, sgu_b, tables,
           wa, wb, wc, wo, ts):
    bsz, seq, d = x.shape
    n_in = proj.shape[-1]
    sum_mat, up, low, pair = tables
    sgub = jnp.broadcast_to(sgu_b[:, :, None], (G_C, SGU_CHUNK, GC_CH))
    const2 = lambda b, i: (0, 0)
    const3 = lambda b, i: (0, 0, 0)
    return pl.pallas_call(
        functools.partial(_mixer_kernel, layer, ts),
        grid=(bsz, seq // ts),
        in_specs=[
            pl.BlockSpec((1, ts, n_in), lambda b, i: (b, i, 0)),
            pl.BlockSpec((1, ts, d), lambda b, i: (b, i, 0)),
            pl.BlockSpec((1, 6, d), lambda b, i: (b, 0, 0)),
            pl.BlockSpec(conv_w.shape, const2),
            pl.BlockSpec(lb_logits.shape, const2),
            pl.BlockSpec((1, D_BV), const2),
            pl.BlockSpec((1, D_C), const2),
            pl.BlockSpec(sgu_w.shape, const3),
            pl.BlockSpec(sgub.shape, const3),
            pl.BlockSpec(sum_mat.shape, const2),
            pl.BlockSpec(up.shape, const3),
            pl.BlockSpec(low.shape, const3),
            pl.BlockSpec(pair.shape, const3),
            pl.BlockSpec(wa.shape, const2),
            pl.BlockSpec(wb.shape, const2),
            pl.BlockSpec(wc.shape, const2),
            pl.BlockSpec(wo.shape, const2),
        ],
        out_specs=pl.BlockSpec((1, ts, d), lambda b, i: (b, i, 0)),
        out_shape=jax.ShapeDtypeStruct((bsz, seq, d), F32),
        scratch_shapes=[
            pltpu.VMEM((ts + 8, D_A), F32),
            pltpu.VMEM((H_B, DV_B, DK_B), F32),
            pltpu.VMEM((ts, D_BK), F32),
            pltpu.VMEM((ts, D_BK), F32),
            pltpu.VMEM((ts, D_BK), F32),
            pltpu.VMEM((ts, D_C), BF16),
            pltpu.VMEM((ts, D_BV), BF16),
            pltpu.VMEM((ts, D_C), BF16),
            pltpu.VMEM((G_C, SGU_CHUNK, SGU_CHUNK), BF16),
        ],
        compiler_params=_params("arbitrary", "arbitrary"),
        name="mixer",
    )(proj, x, mod, conv_w, lb_logits, hgrn_nw.reshape(1, D_BV), sgu_nw.reshape(1, D_C), sgu_w, sgub,
      sum_mat, up, low, pair, wa, wb, wc, wo)


def _swiglu_tile(h, w1_at, w3_at, w2, g_ref):
    d_ff = g_ref.shape[1]
    for c0 in range(0, d_ff, MXU_N):
        a = _dot(h, w1_at(c0))
        g_ref[:, c0:c0 + MXU_N] = (_silu(a) * _dot(h, w3_at(c0))).astype(BF16)
    return _dot(g_ref[...], w2)


def _final_norm(v, fw):
    return v * lax.rsqrt(jnp.mean(v * v, axis=-1, keepdims=True) + EPS) * fw


def _ffn_kernel(final, x_ref, mod_ref, nw_ref, fw_ref, w1_ref, w3_ref, w2_ref, o_ref, g_ref):
    x = x_ref[0]
    h = _norm_mod(x, nw_ref[...], mod_ref[0, 3:4, :], mod_ref[0, 4:5, :]).astype(BF16)
    y = _swiglu_tile(h, lambda c0: w1_ref[:, c0:c0 + MXU_N], lambda c0: w3_ref[:, c0:c0 + MXU_N],
                     w2_ref[...], g_ref)
    out = x + mod_ref[0, 5:6, :] * y
    o_ref[0] = _final_norm(out, fw_ref[...]) if final else out


def _ffn(x, mod, nw, fw, w1, w3, w2, tm, final):
    bsz, seq, d = x.shape
    d_ff = w1.shape[1]
    const2 = lambda b, i: (0, 0)
    resident = dict(pipeline_mode=pl.Buffered(1))
    return pl.pallas_call(
        functools.partial(_ffn_kernel, final),
        grid=(bsz, seq // tm),
        in_specs=[
            pl.BlockSpec((1, tm, d), lambda b, i: (b, i, 0)),
            pl.BlockSpec((1, 6, d), lambda b, i: (b, 0, 0)),
            pl.BlockSpec((1, d), const2),
            pl.BlockSpec((1, d), const2),
            pl.BlockSpec((d, d_ff), const2, **resident),
            pl.BlockSpec((d, d_ff), const2, **resident),
            pl.BlockSpec((d_ff, d), const2, **resident),
        ],
        out_specs=pl.BlockSpec((1, tm, d), lambda b, i: (b, i, 0)),
        out_shape=jax.ShapeDtypeStruct((bsz, seq, d), F32),
        scratch_shapes=[pltpu.VMEM((tm, d_ff), BF16)],
        compiler_params=_params("arbitrary", "arbitrary"),
        name="ffn",
    )(x, mod, nw.reshape(1, d), fw.reshape(1, d), w1, w3, w2)


META_E, META_P, META_R = 0, 2, 4


def _router_kernel(x_ref, mod_ref, nw_ref, rw_ref, rb_ref, tri_ref, h_ref, meta_ref, cnt_ref, carry):
    first = jnp.logical_and(pl.program_id(0) == 0, pl.program_id(1) == 0)

    @pl.when(first)
    def _():
        carry[...] = jnp.zeros_like(carry)

    h = _norm_mod(x_ref[0], nw_ref[...], mod_ref[0, 3:4, :], mod_ref[0, 4:5, :])
    h_ref[0] = h
    logits = jnp.dot(h, rw_ref[...], preferred_element_type=F32,
                     precision=lax.Precision.HIGHEST) + rb_ref[...]
    lane = lax.broadcasted_iota(jnp.int32, logits.shape, 1)
    m1 = jnp.max(logits, axis=-1, keepdims=True)
    i1 = jnp.min(jnp.where(logits == m1, lane, LANES), axis=-1, keepdims=True)
    rest = jnp.where(lane == i1, -jnp.inf, logits)
    m2 = jnp.max(rest, axis=-1, keepdims=True)
    i2 = jnp.min(jnp.where(rest == m2, lane, LANES), axis=-1, keepdims=True)
    e2 = jnp.exp(m2 - m1)
    p1 = 1.0 / (1.0 + e2)
    p2 = e2 / (1.0 + e2)
    sel1 = lane == i1
    sel2 = lane == i2
    onehot = jnp.where(jnp.logical_or(sel1, sel2), 1.0, 0.0)
    before = _dot(tri_ref[...], onehot.astype(BF16)) + carry[...]
    r1 = jnp.sum(jnp.where(sel1, before, 0.0), axis=-1, keepdims=True)
    r2 = jnp.sum(jnp.where(sel2, before, 0.0), axis=-1, keepdims=True)
    carry[...] = carry[...] + jnp.sum(onehot, axis=0, keepdims=True)
    cnt_ref[...] = carry[...]
    rec = jnp.where(lane == META_E, i1.astype(F32), 0.0)
    rec = jnp.where(lane == META_E + 1, i2.astype(F32), rec)
    rec = jnp.where(lane == META_P, p1, rec)
    rec = jnp.where(lane == META_P + 1, p2, rec)
    rec = jnp.where(lane == META_R, r1, rec)
    rec = jnp.where(lane == META_R + 1, r2, rec)
    meta_ref[0] = rec


def _router(x, mod, nw, router_w, router_b, tm):
    bsz, seq, d = x.shape
    ne = router_w.shape[1]
    rw = jnp.zeros((d, LANES), F32).at[:, :ne].set(router_w)
    rb = jnp.full((1, LANES), NEG_BIG, F32).at[0, :ne].set(router_b)
    tri = jnp.asarray(np.tril(np.ones((tm, tm), np.float32), -1), BF16)
    const2 = lambda b, i: (0, 0)
    return pl.pallas_call(
        _router_kernel,
        grid=(bsz, seq // tm),
        in_specs=[
            pl.BlockSpec((1, tm, d), lambda b, i: (b, i, 0)),
            pl.BlockSpec((1, 6, d), lambda b, i: (b, 0, 0)),
            pl.BlockSpec((1, d), const2),
            pl.BlockSpec((d, LANES), const2),
            pl.BlockSpec((1, LANES), const2),
            pl.BlockSpec((tm, tm), const2),
        ],
        out_specs=[
            pl.BlockSpec((1, tm, d), lambda b, i: (b, i, 0)),
            pl.BlockSpec((1, tm, LANES), lambda b, i: (b, i, 0)),
            pl.BlockSpec((1, LANES), const2),
        ],
        out_shape=[
            jax.ShapeDtypeStruct((bsz, seq, d), F32),
            jax.ShapeDtypeStruct((bsz, seq, LANES), F32),
            jax.ShapeDtypeStruct((1, LANES), F32),
        ],
        scratch_shapes=[pltpu.VMEM((1, LANES), F32)],
        compiler_params=_params("arbitrary", "arbitrary"),
        name="router",
    )(x, mod, nw.reshape(1, d), rw, rb, tri)


def _row_copy(src, src_row, dst, dst_row, sem):
    return pltpu.make_async_copy(src.at[pl.ds(src_row, 1), :], dst.at[pl.ds(dst_row, 1), :], sem)


def _dispatch_kernel(tt, pos_ref, h_ref, xs_hbm, sem):
    def issue(t, carry):
        _row_copy(h_ref, t, xs_hbm, pos_ref[0, 0, 2 * t], sem).start()
        _row_copy(h_ref, t, xs_hbm, pos_ref[0, 0, 2 * t + 1], sem).start()
        return carry

    lax.fori_loop(0, tt, issue, 0, unroll=8)
    for _ in range(2):
        pltpu.make_async_copy(h_ref, xs_hbm.at[pl.ds(0, tt), :], sem).wait()


def _dispatch(h2d, pos, tt):
    n_tok, d = h2d.shape
    return pl.pallas_call(
        functools.partial(_dispatch_kernel, tt),
        grid=(n_tok // tt,),
        in_specs=[
            pl.BlockSpec((1, 1, 2 * tt), lambda i: (i, 0, 0), memory_space=pltpu.SMEM),
            pl.BlockSpec((tt, d), lambda i: (i, 0)),
        ],
        out_specs=pl.BlockSpec(memory_space=pl.ANY),
        out_shape=jax.ShapeDtypeStruct((2 * n_tok, d), F32),
        scratch_shapes=[pltpu.SemaphoreType.DMA(())],
        compiler_params=_params("arbitrary"),
        name="dispatch",
    )(pos.reshape(n_tok // tt, 1, 2 * tt), h2d)


def _expert_kernel(tm, tile_ref, exp_ref, lo_ref, hi_ref, first_ref, valid_ref,
                   xs_ref, w1_ref, w3_ref, w2_ref, y_ref, g_ref):
    s = pl.program_id(0)

    @pl.when(valid_ref[s] == 1)
    def _():
        h = xs_ref[...].astype(BF16)
        res = _swiglu_tile(h, lambda c0: w1_ref[0, :, c0:c0 + MXU_N],
                           lambda c0: w3_ref[0, :, c0:c0 + MXU_N], w2_ref[0], g_ref)
        row = tile_ref[s] * tm + lax.broadcasted_iota(jnp.int32, (tm, 1), 0)
        mine = jnp.logical_and(row >= lo_ref[s], row < hi_ref[s])

        @pl.when(first_ref[s] == 1)
        def _():
            y_ref[...] = jnp.where(mine, res, 0.0)

        @pl.when(first_ref[s] == 0)
        def _():
            y_ref[...] = jnp.where(mine, res, y_ref[...])


def _experts(xs, steps, w1, w3, w2, tm):
    n_rows, d = xs.shape
    d_ff = w1.shape[2]
    n_steps = steps[0].shape[0]
    grid_spec = pltpu.PrefetchScalarGridSpec(
        num_scalar_prefetch=6,
        grid=(n_steps,),
        in_specs=[
            pl.BlockSpec((tm, d), lambda s, tile, exp, lo, hi, first, valid: (tile[s], 0)),
            pl.BlockSpec((1, d, d_ff), lambda s, tile, exp, lo, hi, first, valid: (exp[s], 0, 0)),
            pl.BlockSpec((1, d, d_ff), lambda s, tile, exp, lo, hi, first, valid: (exp[s], 0, 0)),
            pl.BlockSpec((1, d_ff, d), lambda s, tile, exp, lo, hi, first, valid: (exp[s], 0, 0)),
        ],
        out_specs=pl.BlockSpec((tm, d), lambda s, tile, exp, lo, hi, first, valid: (tile[s], 0)),
        scratch_shapes=[pltpu.VMEM((tm, d_ff), BF16)],
    )
    return pl.pallas_call(
        functools.partial(_expert_kernel, tm),
        grid_spec=grid_spec,
        out_shape=jax.ShapeDtypeStruct((n_rows, d), F32),
        compiler_params=_params("arbitrary"),
        name="experts",
    )(*steps, xs, w1, w3, w2)


def _combine_kernel(tt, final, pos_ref, y_hbm, x_ref, mod_ref, meta_ref, fw_ref, o_ref, buf, sem):
    def issue(t, carry):
        _row_copy(y_hbm, pos_ref[0, 0, 2 * t], buf.at[0], t, sem).start()
        _row_copy(y_hbm, pos_ref[0, 0, 2 * t + 1], buf.at[1], t, sem).start()
        return carry

    lax.fori_loop(0, tt, issue, 0, unroll=8)
    for k in range(2):
        pltpu.make_async_copy(y_hbm.at[pl.ds(0, tt), :], buf.at[k], sem).wait()
    p1 = meta_ref[0, :, META_P:META_P + 1]
    p2 = meta_ref[0, :, META_P + 1:META_P + 2]
    out = x_ref[0] + mod_ref[0, 5:6, :] * (p1 * buf[0] + p2 * buf[1])
    o_ref[0] = _final_norm(out, fw_ref[...]) if final else out


def _combine(y, pos, x, mod, meta, fw, tt, final):
    bsz, seq, d = x.shape
    per_b = seq // tt
    return pl.pallas_call(
        functools.partial(_combine_kernel, tt, final),
        grid=(bsz, per_b),
        in_specs=[
            pl.BlockSpec((1, 1, 2 * tt), lambda b, i: (b * per_b + i, 0, 0), memory_space=pltpu.SMEM),
            pl.BlockSpec(memory_space=pl.ANY),
            pl.BlockSpec((1, tt, d), lambda b, i: (b, i, 0)),
            pl.BlockSpec((1, 6, d), lambda b, i: (b, 0, 0)),
            pl.BlockSpec((1, tt, LANES), lambda b, i: (b, i, 0)),
            pl.BlockSpec((1, d), lambda b, i: (0, 0)),
        ],
        out_specs=pl.BlockSpec((1, tt, d), lambda b, i: (b, i, 0)),
        out_shape=jax.ShapeDtypeStruct((bsz, seq, d), F32),
        scratch_shapes=[pltpu.VMEM((2, tt, d), F32), pltpu.SemaphoreType.DMA(())],
        compiler_params=_params("arbitrary", "arbitrary"),
        name="combine",
    )(pos.reshape(bsz * per_b, 1, 2 * tt), y, x, mod, meta, fw.reshape(1, d))


def _expert_steps(counts, n_rows, tm):
    n_tiles = n_rows // tm
    n_steps = n_tiles + N_EXPERTS - 1
    ends = jnp.cumsum(counts)
    starts = ends - counts
    first_tile = starts // tm
    last_tile = (ends - 1) // tm
    tiles_e = jnp.where(counts > 0, last_tile - first_tile + 1, 0)
    step_end = jnp.cumsum(tiles_e)
    step_start = step_end - tiles_e
    total = step_end[-1]
    s = jnp.arange(n_steps, dtype=jnp.int32)
    sc = jnp.minimum(s, total - 1)
    e_s = jnp.sum((sc[:, None] >= step_end[None, :]).astype(jnp.int32), axis=1)
    tile_s = first_tile[e_s] + (sc - step_start[e_s])
    valid = (s < total).astype(jnp.int32)
    prev_tile = jnp.concatenate([jnp.full((1,), -1, jnp.int32), tile_s[:-1]])
    first = jnp.logical_and(tile_s != prev_tile, valid == 1).astype(jnp.int32)
    as_i32 = lambda v: v.astype(jnp.int32)
    return (as_i32(tile_s), as_i32(e_s), as_i32(starts[e_s]), as_i32(ends[e_s]), first, valid)


def _moe(x, mod, nw, fw, router_w, router_b, w1, w3, w2, tm_route, tt, tm_exp, final):
    bsz, seq, d = x.shape
    n_tok = bsz * seq
    h, meta, cnt = _router(x, mod, nw, router_w, router_b, tm_route)
    counts = cnt[0, :N_EXPERTS].astype(jnp.int32)
    starts = jnp.cumsum(counts) - counts
    meta2 = meta.reshape(n_tok, LANES)
    experts = meta2[:, META_E:META_E + 2].astype(jnp.int32)
    ranks = meta2[:, META_R:META_R + 2].astype(jnp.int32)
    onehot = experts[:, :, None] == jnp.arange(N_EXPERTS, dtype=jnp.int32)[None, None, :]
    pos = jnp.sum(jnp.where(onehot, starts[None, None, :], 0), axis=-1) + ranks
    xs = _dispatch(h.reshape(n_tok, d), pos, tt)
    y = _experts(xs, _expert_steps(counts, 2 * n_tok, tm_exp), w1, w3, w2, tm_exp)
    return _combine(y, pos, x, mod, meta, fw, tt, final)


def _pick(n, pref):
    t = min(n, pref)
    assert n % t == 0, (n, pref)
    return t


def kernel(x, c, ada_w, ada_b, norm_mix_w, norm_ffn_w, w_in, conv_w, hgrn_lb_logits, hgrn_norm_w,
           sgu_norm_w, sgu_w, sgu_b, w_br_a, w_br_b, w_br_c, w_o, ffn_w1, ffn_w3, ffn_w2,
           moe_router_w, moe_router_b, moe_w1, moe_w3, moe_w2, final_norm_w):
    depth = ada_w.shape[0]
    bsz, seq, d = x.shape
    n_in = w_in.shape[-1]
    assert n_in == OFF_GATE + 3 * d and seq % CHUNK == 0
    assert ffn_w1.shape[-1] % MXU_N == 0 and moe_w1.shape[-1] % MXU_N == 0

    tm_in = _pick(seq, 1024)
    tn_in = n_in // 5 if (n_in // 5) % LANES == 0 and n_in % 5 == 0 else n_in
    ts = _pick(seq, 512)
    tm_ffn = _pick(seq, 512)
    tm_route = _pick(seq, 1024)
    tt = _pick(seq, 256)
    tm_exp = _pick(2 * bsz * seq, 512)

    mod_all = _adaln(c, ada_w, ada_b).reshape(depth, bsz, 6, d)
    tables = _level_tables()
    bf = lambda w: w.astype(BF16)

    for l in range(depth):
        mod = mod_all[l]
        proj = _inproj(x, mod, norm_mix_w[l], bf(w_in[l]), tm_in, tn_in)
        x = _mixer(l, proj, x, mod, conv_w[l], hgrn_lb_logits, hgrn_norm_w[l], sgu_norm_w[l], sgu_w[l],
                   sgu_b[l], tables, bf(w_br_a[l]), bf(w_br_b[l]), bf(w_br_c[l]), bf(w_o[l]), ts)
        final = l == depth - 1
        j = l // 2
        if l % 2 == 0:
            x = _ffn(x, mod, norm_ffn_w[l], final_norm_w, bf(ffn_w1[j]), bf(ffn_w3[j]), bf(ffn_w2[j]),
                     tm_ffn, final)
        else:
            x = _moe(x, mod, norm_ffn_w[l], final_norm_w, moe_router_w[j], moe_router_b[j],
                     bf(moe_w1[j]), bf(moe_w3[j]), bf(moe_w2[j]), tm_route, tt, tm_exp, final)
    return x
```

```python
import functools

import numpy as np
import jax
import jax.numpy as jnp
from jax import lax
from jax.experimental import pallas as pl
from jax.experimental.pallas import tpu as pltpu

F32 = jnp.float32
BF16 = jnp.bfloat16

EPS = 1e-6
LB_FLOOR = 1e-30

D_A = 512
CONV_WIDTH = 3
H_B = 4
DK_B = 128
DV_B = 128
D_BK = H_B * DK_B
D_BV = H_B * DV_B
G_C = 4
GC_CH = 128
D_C = G_C * GC_CH
SGU_CHUNK = 128
N_EXPERTS = 8

OFF_BQ = 3 * D_A
OFF_BG = OFF_BQ + 2 * D_BK + D_BV
OFF_GATE = OFF_BG + D_BV + 2 * D_C

CHUNK = 128
N_LEVELS = 7
assert 1 << N_LEVELS == CHUNK and CHUNK == SGU_CHUNK

LANES = 128
MXU_N = 256
VMEM_LIMIT = 60 * 1024 * 1024
NEG_BIG = -1e30


def _sigmoid(v):
    return 0.5 + 0.5 * jnp.tanh(0.5 * v)


def _silu(v):
    return v * _sigmoid(v)


def _dot(a, b):
    return jnp.dot(a, b, preferred_element_type=F32)


def _dot_nt(a, b):
    return lax.dot_general(a, b, (((1,), (1,)), ((), ())), preferred_element_type=F32)


def _dot_tn(a, b):
    return lax.dot_general(a, b, (((0,), (0,)), ((), ())), preferred_element_type=F32)


def _norm_mod(x, nw, shift, scale):
    ms = jnp.mean(x * x, axis=-1, keepdims=True)
    return (x * lax.rsqrt(ms + EPS) * nw) * (1.0 + scale) + shift


def _params(*sem):
    return pltpu.CompilerParams(dimension_semantics=sem, vmem_limit_bytes=VMEM_LIMIT)


def _adaln_kernel(c_ref, w_ref, b_ref, o_ref):
    ca = _silu(c_ref[...]).astype(BF16)
    o_ref[0] = _dot(ca, w_ref[0].astype(BF16)) + b_ref[0]


def _adaln(c, ada_w, ada_b):
    depth, d, n = ada_w.shape
    bsz = c.shape[0]
    tn = n // 4
    return pl.pallas_call(
        _adaln_kernel,
        grid=(depth, n // tn),
        in_specs=[
            pl.BlockSpec((bsz, d), lambda l, j: (0, 0)),
            pl.BlockSpec((1, d, tn), lambda l, j: (l, 0, j)),
            pl.BlockSpec((1, 1, tn), lambda l, j: (l, 0, j)),
        ],
        out_specs=pl.BlockSpec((1, bsz, tn), lambda l, j: (l, 0, j)),
        out_shape=jax.ShapeDtypeStruct((depth, bsz, n), F32),
        compiler_params=_params("arbitrary", "arbitrary"),
        name="adaln",
    )(c, ada_w, ada_b.reshape(depth, 1, n))


TN_IN = 3 * D_A
HQ_Q, HQ_FHI, HQ_FLO, HQ_K, HQ_V = 0, D_BK, 2 * D_BK, 3 * D_BK, 4 * D_BK
GC_G, GC_U, GC_V = 0, D_BV, D_BV + D_C
assert OFF_BQ == TN_IN and OFF_BG == 2 * TN_IN and OFF_GATE == 3 * TN_IN and D_BK == D_BV == D_C == D_A


def _inproj_kernel(layer, x_ref, mod_ref, nw_ref, w_ref, convw_ref, lbl_ref, snw_ref,
                   ya_ref, hq_ref, gc_ref, gates_ref, h_ref, zbuf):
    i = pl.program_id(1)
    j = pl.program_id(2)
    tm = x_ref.shape[1]

    def group(g):
        return _dot(h_ref[...], w_ref[:, g * D_A:(g + 1) * D_A])

    @pl.when(jnp.logical_and(j == 0, i == 0))
    def _():
        zbuf[0:8, :] = jnp.zeros((8, D_A), F32)

    @pl.when(j == 0)
    def _():
        h = _norm_mod(x_ref[0], nw_ref[...], mod_ref[0, 0:1, :], mod_ref[0, 1:2, :])
        h_ref[...] = h.astype(BF16)
        c0 = group(0)
        z = group(1) * group(2)
        zbuf[8:8 + tm, :] = z
        y = (convw_ref[0:1, :] * z + convw_ref[1:2, :] * zbuf[7:7 + tm, :]
             + convw_ref[2:3, :] * zbuf[6:6 + tm, :])
        ya_ref[0] = (c0 * y).astype(BF16)
        zbuf[0:8, :] = zbuf[tm:tm + 8, :]

    @pl.when(j == 1)
    def _():
        lg = lbl_ref[...]
        pe = jnp.exp(lg - jnp.max(lg, axis=0, keepdims=True))
        p = pe / jnp.sum(pe, axis=0, keepdims=True)
        lb = jnp.clip(jnp.sum(p[0:layer + 1], axis=0, keepdims=True) - p[0:1], 0.0, 1.0)
        log_lb = jnp.log(jnp.maximum(lb, LB_FLOOR))
        log_1m = jnp.log(1.0 - lb)
        hq_ref[0, :, HQ_Q:HQ_Q + D_BK] = _silu(group(0)).astype(BF16)
        hq_ref[0, :, HQ_V:HQ_V + D_BV] = group(2).astype(BF16)
        fr = group(1)
        log_sig = jnp.minimum(fr, 0.0) - jnp.log(1.0 + jnp.exp(-jnp.abs(fr)))
        t2 = log_1m + log_sig
        lf = jnp.maximum(log_lb, t2) + jnp.log(1.0 + jnp.exp(-jnp.abs(log_lb - t2)))
        lf_hi = lf.astype(BF16)
        hq_ref[0, :, HQ_FHI:HQ_FHI + D_BK] = lf_hi
        hq_ref[0, :, HQ_FLO:HQ_FLO + D_BK] = (lf - lf_hi.astype(F32)).astype(BF16)
        hq_ref[0, :, HQ_K:HQ_K + D_BK] = ((1.0 - lb) * _sigmoid(-fr)).astype(BF16)

    @pl.when(j == 2)
    def _():
        gc_ref[0, :, GC_G:GC_G + D_BV] = _silu(group(0)).astype(BF16)
        gc_ref[0, :, GC_U:GC_U + D_C] = group(1).astype(BF16)
        cv = group(2)
        vn = cv * lax.rsqrt(jnp.mean(cv * cv, axis=-1, keepdims=True) + EPS) * snw_ref[...]
        gc_ref[0, :, GC_V:GC_V + D_C] = vn.astype(BF16)

    @pl.when(j >= 3)
    def _():
        for g in range(TN_IN // D_A):
            gates_ref[0, :, g * D_A:(g + 1) * D_A] = _sigmoid(group(g)).astype(BF16)


def _inproj(layer, x, mod, nw, w_bf16, conv_w, lb_logits, sgu_nw, tm):
    bsz, seq, d = x.shape
    n = w_bf16.shape[1]
    assert TN_IN == d + d // 2 and n == 5 * TN_IN
    row = lambda b, i, j: (b, i, 0)
    const2 = lambda b, i, j: (0, 0)
    out = lambda width: jax.ShapeDtypeStruct((bsz, seq, width), BF16)
    return pl.pallas_call(
        functools.partial(_inproj_kernel, layer),
        grid=(bsz, seq // tm, n // TN_IN),
        in_specs=[
            pl.BlockSpec((1, tm, d), row),
            pl.BlockSpec((1, 6, d), lambda b, i, j: (b, 0, 0)),
            pl.BlockSpec((1, d), const2),
            pl.BlockSpec((d, TN_IN), lambda b, i, j: (0, j)),
            pl.BlockSpec(conv_w.shape, const2),
            pl.BlockSpec(lb_logits.shape, const2),
            pl.BlockSpec((1, D_C), const2),
        ],
        out_specs=[
            pl.BlockSpec((1, tm, D_A), row),
            pl.BlockSpec((1, tm, 5 * D_BK), row),
            pl.BlockSpec((1, tm, 3 * D_C), row),
            pl.BlockSpec((1, tm, TN_IN), lambda b, i, j: (b, i, jnp.maximum(j - 3, 0))),
        ],
        out_shape=[out(D_A), out(5 * D_BK), out(3 * D_C), out(3 * d)],
        scratch_shapes=[pltpu.VMEM((tm, d), BF16), pltpu.VMEM((tm + 8, D_A), F32)],
        compiler_params=_params("arbitrary", "arbitrary", "arbitrary"),
        name="inproj",
    )(x, mod, nw.reshape(1, d), w_bf16, conv_w, lb_logits, sgu_nw.reshape(1, D_C))


def _level_tables():
    t = np.arange(CHUNK)
    sum_mat = np.zeros((N_LEVELS + 1, CHUNK, CHUNK), np.float32)
    sum_mat[0] = (t[None, :] <= t[:, None])
    up = np.zeros((N_LEVELS + 1, CHUNK, LANES), np.float32)
    low = np.zeros((N_LEVELS + 1, CHUNK, LANES), np.float32)
    pair = np.zeros((N_LEVELS + 1, CHUNK, CHUNK), np.float32)
    pair[0] = np.eye(CHUNK)
    for j in range(1, N_LEVELS + 1):
        blk, half = 1 << j, 1 << (j - 1)
        base = (t // blk) * blk
        m = base + half - 1
        is_up = (t - base) >= half
        u = t[None, :]
        upper_rows = (u > m[:, None]) & (u <= t[:, None])
        lower_rows = (u > t[:, None]) & (u <= m[:, None])
        sum_mat[j] = np.where(is_up[:, None], upper_rows, lower_rows)
        up[j] = is_up[:, None]
        low[j] = ~is_up[:, None]
        pair[j] = (t[:, None] // blk) == (t[None, :] // blk)
    return (jnp.asarray(sum_mat.reshape((N_LEVELS + 1) * CHUNK, CHUNK), BF16),
            jnp.asarray(up), jnp.asarray(low), jnp.asarray(pair))


MERGE_ROWS = 128


def _mixer_kernel(ts,
                  ya_ref, hq_ref, gc_ref, gates_ref, x_ref, mod_ref, hnw_ref, sguw_ref, sgub_ref,
                  summat_ref, up_ref, low_ref, pair_ref, wa_ref, wb_ref, wc_ref, wo_ref,
                  o_ref,
                  st_ref, yb_ref, yc_ref, wsm_ref, br_ref, mg_ref):
    n_chunks = ts // CHUNK
    d = x_ref.shape[-1]

    @pl.when(pl.program_id(1) == 0)
    def _():
        st_ref[...] = jnp.zeros_like(st_ref)

    tril = (lax.broadcasted_iota(jnp.int32, (SGU_CHUNK, SGU_CHUNK), 0)
            >= lax.broadcasted_iota(jnp.int32, (SGU_CHUNK, SGU_CHUNK), 1))
    for g in range(G_C):
        wsm_ref[g] = jnp.where(tril, sguw_ref[g], 0.0).astype(BF16)

    def chunk_body(c, carry):
        r0 = pl.multiple_of(c * CHUNK, CHUNK)
        rows = pl.ds(r0, CHUNK)

        for g in range(G_C):
            cs = slice(g * GC_CH, (g + 1) * GC_CH)
            vn = gc_ref[0, rows, GC_V + g * GC_CH:GC_V + (g + 1) * GC_CH]
            u = gc_ref[0, rows, GC_U + g * GC_CH:GC_U + (g + 1) * GC_CH].astype(F32)
            yc_ref[rows, cs] = (u * (_dot(wsm_ref[g], vn) + sgub_ref[g])).astype(BF16)

        lv = (_dot(summat_ref[...], hq_ref[0, rows, HQ_FHI:HQ_FHI + D_BK])
              + _dot(summat_ref[...], hq_ref[0, rows, HQ_FLO:HQ_FLO + D_BK]))

        for h in range(H_B):
            cs = slice(h * DK_B, (h + 1) * DK_B)
            b_h = lv[0:CHUNK, cs]
            q_bf = hq_ref[0, rows, HQ_Q + h * DK_B:HQ_Q + (h + 1) * DK_B]
            k_bf = hq_ref[0, rows, HQ_K + h * DK_B:HQ_K + (h + 1) * DK_B]
            v_h = hq_ref[0, rows, HQ_V + h * DV_B:HQ_V + (h + 1) * DV_B]
            q_h = q_bf.astype(F32)
            k_h = k_bf.astype(F32)
            st = st_ref[h]
            o = _dot_nt((q_h * jnp.exp(b_h)).astype(BF16), st.astype(BF16))
            scores = pair_ref[0] * _dot_nt(q_bf, k_bf)
            for j in range(1, N_LEVELS + 1):
                e = jnp.exp(lv[j * CHUNK:(j + 1) * CHUNK, cs])
                q_j = (q_h * e * up_ref[j]).astype(BF16)
                k_j = (k_h * e * low_ref[j]).astype(BF16)
                scores = scores + pair_ref[j] * _dot_nt(q_j, k_j)
            o = o + _dot(scores.astype(BF16), v_h)
            b_last = b_h[CHUNK - 1:CHUNK, :]
            k_dec = (k_h * jnp.exp(b_last - b_h)).astype(BF16)
            st_ref[h] = st * jnp.exp(b_last) + _dot_tn(v_h, k_dec)
            on = o * lax.rsqrt(jnp.mean(o * o, axis=-1, keepdims=True) + EPS) * hnw_ref[:, cs]
            sg = gc_ref[0, rows, GC_G + h * DV_B:GC_G + (h + 1) * DV_B].astype(F32)
            yb_ref[rows, cs] = (on * sg).astype(BF16)
        return carry

    lax.fori_loop(0, n_chunks, chunk_body, 0)

    br_ref[0] = _dot(ya_ref[0], wa_ref[...])
    br_ref[1] = _dot(yb_ref[...], wb_ref[...])
    br_ref[2] = _dot(yc_ref[...], wc_ref[...])

    def merge_body(r, carry):
        rows = pl.ds(pl.multiple_of(r * MERGE_ROWS, MERGE_ROWS), MERGE_ROWS)
        merged = gates_ref[0, rows, 0:d].astype(F32) * br_ref[0, rows, :]
        merged = merged + gates_ref[0, rows, d:2 * d].astype(F32) * br_ref[1, rows, :]
        merged = merged + gates_ref[0, rows, 2 * d:3 * d].astype(F32) * br_ref[2, rows, :]
        mg_ref[rows, :] = merged.astype(BF16)
        return carry

    lax.fori_loop(0, ts // MERGE_ROWS, merge_body, 0)
    o_ref[0] = x_ref[0] + mod_ref[0, 2:3, :] * _dot(mg_ref[...], wo_ref[...])


def _mixer(ya, hq, gc, gates, x, mod, hgrn_nw, sgu_w, sgu_b, tables, wa, wb, wc, wo, ts):
    bsz, seq, d = x.shape
    sum_mat, up, low, pair = tables
    sgub = jnp.broadcast_to(sgu_b[:, :, None], (G_C, SGU_CHUNK, GC_CH))
    row = lambda b, i: (b, i, 0)
    const2 = lambda b, i: (0, 0)
    const3 = lambda b, i: (0, 0, 0)
    return pl.pallas_call(
        functools.partial(_mixer_kernel, ts),
        grid=(bsz, seq // ts),
        in_specs=[
            pl.BlockSpec((1, ts, ya.shape[-1]), row),
            pl.BlockSpec((1, ts, hq.shape[-1]), row),
            pl.BlockSpec((1, ts, gc.shape[-1]), row),
            pl.BlockSpec((1, ts, gates.shape[-1]), row),
            pl.BlockSpec((1, ts, d), row),
            pl.BlockSpec((1, 6, d), lambda b, i: (b, 0, 0)),
            pl.BlockSpec((1, D_BV), const2),
            pl.BlockSpec(sgu_w.shape, const3),
            pl.BlockSpec(sgub.shape, const3),
            pl.BlockSpec(sum_mat.shape, const2),
            pl.BlockSpec(up.shape, const3),
            pl.BlockSpec(low.shape, const3),
            pl.BlockSpec(pair.shape, const3),
            pl.BlockSpec(wa.shape, const2),
            pl.BlockSpec(wb.shape, const2),
            pl.BlockSpec(wc.shape, const2),
            pl.BlockSpec(wo.shape, const2),
        ],
        out_specs=pl.BlockSpec((1, ts, d), row),
        out_shape=jax.ShapeDtypeStruct((bsz, seq, d), F32),
        scratch_shapes=[
            pltpu.VMEM((H_B, DV_B, DK_B), F32),
            pltpu.VMEM((ts, D_BV), BF16),
            pltpu.VMEM((ts, D_C), BF16),
            pltpu.VMEM((G_C, SGU_CHUNK, SGU_CHUNK), BF16),
            pltpu.VMEM((3, ts, d), F32),
            pltpu.VMEM((ts, d), BF16),
        ],
        compiler_params=_params("arbitrary", "arbitrary"),
        name="mixer",
    )(ya, hq, gc, gates, x, mod, hgrn_nw.reshape(1, D_BV), sgu_w, sgub,
      sum_mat, up, low, pair, wa, wb, wc, wo)


def _swiglu_tile(h, w1_at, w3_at, w2, g_ref):
    d_ff = g_ref.shape[1]
    for c0 in range(0, d_ff, MXU_N):
        a = _dot(h, w1_at(c0))
        g_ref[:, c0:c0 + MXU_N] = (_silu(a) * _dot(h, w3_at(c0))).astype(BF16)
    return _dot(g_ref[...], w2)


def _final_norm(v, fw):
    return v * lax.rsqrt(jnp.mean(v * v, axis=-1, keepdims=True) + EPS) * fw


def _ffn_kernel(final, x_ref, mod_ref, nw_ref, fw_ref, w1_ref, w3_ref, w2_ref, o_ref, g_ref):
    x = x_ref[0]
    h = _norm_mod(x, nw_ref[...], mod_ref[0, 3:4, :], mod_ref[0, 4:5, :]).astype(BF16)
    y = _swiglu_tile(h, lambda c0: w1_ref[:, c0:c0 + MXU_N], lambda c0: w3_ref[:, c0:c0 + MXU_N],
                     w2_ref[...], g_ref)
    out = x + mod_ref[0, 5:6, :] * y
    o_ref[0] = _final_norm(out, fw_ref[...]) if final else out


def _ffn(x, mod, nw, fw, w1, w3, w2, tm, final):
    bsz, seq, d = x.shape
    d_ff = w1.shape[1]
    const2 = lambda b, i: (0, 0)
    resident = dict(pipeline_mode=pl.Buffered(1))
    return pl.pallas_call(
        functools.partial(_ffn_kernel, final),
        grid=(bsz, seq // tm),
        in_specs=[
            pl.BlockSpec((1, tm, d), lambda b, i: (b, i, 0)),
            pl.BlockSpec((1, 6, d), lambda b, i: (b, 0, 0)),
            pl.BlockSpec((1, d), const2),
            pl.BlockSpec((1, d), const2),
            pl.BlockSpec((d, d_ff), const2, **resident),
            pl.BlockSpec((d, d_ff), const2, **resident),
            pl.BlockSpec((d_ff, d), const2, **resident),
        ],
        out_specs=pl.BlockSpec((1, tm, d), lambda b, i: (b, i, 0)),
        out_shape=jax.ShapeDtypeStruct((bsz, seq, d), F32),
        scratch_shapes=[pltpu.VMEM((tm, d_ff), BF16)],
        compiler_params=_params("arbitrary", "arbitrary"),
        name="ffn",
    )(x, mod, nw.reshape(1, d), fw.reshape(1, d), w1, w3, w2)


META_E, META_P, META_R = 0, 2, 4


def _router_kernel(x_ref, mod_ref, nw_ref, rw_ref, rb_ref, tri_ref, h_ref, meta_ref, cnt_ref, carry):
    first = jnp.logical_and(pl.program_id(0) == 0, pl.program_id(1) == 0)

    @pl.when(first)
    def _():
        carry[...] = jnp.zeros_like(carry)

    h = _norm_mod(x_ref[0], nw_ref[...], mod_ref[0, 3:4, :], mod_ref[0, 4:5, :])
    h_ref[0] = h
    logits = jnp.dot(h, rw_ref[...], preferred_element_type=F32,
                     precision=lax.Precision.HIGHEST) + rb_ref[...]
    lane = lax.broadcasted_iota(jnp.int32, logits.shape, 1)
    m1 = jnp.max(logits, axis=-1, keepdims=True)
    i1 = jnp.min(jnp.where(logits == m1, lane, LANES), axis=-1, keepdims=True)
    rest = jnp.where(lane == i1, -jnp.inf, logits)
    m2 = jnp.max(rest, axis=-1, keepdims=True)
    i2 = jnp.min(jnp.where(rest == m2, lane, LANES), axis=-1, keepdims=True)
    e2 = jnp.exp(m2 - m1)
    p1 = 1.0 / (1.0 + e2)
    p2 = e2 / (1.0 + e2)
    sel1 = lane == i1
    sel2 = lane == i2
    onehot = jnp.where(jnp.logical_or(sel1, sel2), 1.0, 0.0)
    before = _dot(tri_ref[...], onehot.astype(BF16)) + carry[...]
    r1 = jnp.sum(jnp.where(sel1, before, 0.0), axis=-1, keepdims=True)
    r2 = jnp.sum(jnp.where(sel2, before, 0.0), axis=-1, keepdims=True)
    carry[...] = carry[...] + jnp.sum(onehot, axis=0, keepdims=True)
    cnt_ref[...] = carry[...]
    rec = jnp.where(lane == META_E, i1.astype(F32), 0.0)
    rec = jnp.where(lane == META_E + 1, i2.astype(F32), rec)
    rec = jnp.where(lane == META_P, p1, rec)
    rec = jnp.where(lane == META_P + 1, p2, rec)
    rec = jnp.where(lane == META_R, r1, rec)
    rec = jnp.where(lane == META_R + 1, r2, rec)
    meta_ref[0] = rec


def _router(x, mod, nw, router_w, router_b, tm):
    bsz, seq, d = x.shape
    ne = router_w.shape[1]
    rw = jnp.zeros((d, LANES), F32).at[:, :ne].set(router_w)
    rb = jnp.full((1, LANES), NEG_BIG, F32).at[0, :ne].set(router_b)
    tri = jnp.asarray(np.tril(np.ones((tm, tm), np.float32), -1), BF16)
    const2 = lambda b, i: (0, 0)
    return pl.pallas_call(
        _router_kernel,
        grid=(bsz, seq // tm),
        in_specs=[
            pl.BlockSpec((1, tm, d), lambda b, i: (b, i, 0)),
            pl.BlockSpec((1, 6, d), lambda b, i: (b, 0, 0)),
            pl.BlockSpec((1, d), const2),
            pl.BlockSpec((d, LANES), const2),
            pl.BlockSpec((1, LANES), const2),
            pl.BlockSpec((tm, tm), const2),
        ],
        out_specs=[
            pl.BlockSpec((1, tm, d), lambda b, i: (b, i, 0)),
            pl.BlockSpec((1, tm, LANES), lambda b, i: (b, i, 0)),
            pl.BlockSpec((1, LANES), const2),
        ],
        out_shape=[
            jax.ShapeDtypeStruct((bsz, seq, d), F32),
            jax.ShapeDtypeStruct((bsz, seq, LANES), F32),
            jax.ShapeDtypeStruct((1, LANES), F32),
        ],
        scratch_shapes=[pltpu.VMEM((1, LANES), F32)],
        compiler_params=_params("arbitrary", "arbitrary"),
        name="router",
    )(x, mod, nw.reshape(1, d), rw, rb, tri)


def _row_copy(src, src_row, dst, dst_row, sem):
    return pltpu.make_async_copy(src.at[pl.ds(src_row, 1), :], dst.at[pl.ds(dst_row, 1), :], sem)


def _dispatch_kernel(tt, pos_ref, h_ref, xs_hbm, sem):
    def issue(t, carry):
        _row_copy(h_ref, t, xs_hbm, pos_ref[0, 0, 2 * t], sem).start()
        _row_copy(h_ref, t, xs_hbm, pos_ref[0, 0, 2 * t + 1], sem).start()
        return carry

    lax.fori_loop(0, tt, issue, 0, unroll=8)
    for _ in range(2):
        pltpu.make_async_copy(h_ref, xs_hbm.at[pl.ds(0, tt), :], sem).wait()


def _dispatch(h2d, pos, tt):
    n_tok, d = h2d.shape
    return pl.pallas_call(
        functools.partial(_dispatch_kernel, tt),
        grid=(n_tok // tt,),
        in_specs=[
            pl.BlockSpec((1, 1, 2 * tt), lambda i: (i, 0, 0), memory_space=pltpu.SMEM),
            pl.BlockSpec((tt, d), lambda i: (i, 0)),
        ],
        out_specs=pl.BlockSpec(memory_space=pl.ANY),
        out_shape=jax.ShapeDtypeStruct((2 * n_tok, d), F32),
        scratch_shapes=[pltpu.SemaphoreType.DMA(())],
        compiler_params=_params("arbitrary"),
        name="dispatch",
    )(pos.reshape(n_tok // tt, 1, 2 * tt), h2d)


def _expert_kernel(tm, tile_ref, exp_ref, lo_ref, hi_ref, first_ref, valid_ref,
                   xs_ref, w1_ref, w3_ref, w2_ref, y_ref, g_ref):
    s = pl.program_id(0)

    @pl.when(valid_ref[s] == 1)
    def _():
        h = xs_ref[...].astype(BF16)
        res = _swiglu_tile(h, lambda c0: w1_ref[0, :, c0:c0 + MXU_N],
                           lambda c0: w3_ref[0, :, c0:c0 + MXU_N], w2_ref[0], g_ref)
        row = tile_ref[s] * tm + lax.broadcasted_iota(jnp.int32, (tm, 1), 0)
        mine = jnp.logical_and(row >= lo_ref[s], row < hi_ref[s])

        @pl.when(first_ref[s] == 1)
        def _():
            y_ref[...] = jnp.where(mine, res, 0.0)

        @pl.when(first_ref[s] == 0)
        def _():
            y_ref[...] = jnp.where(mine, res, y_ref[...])


def _experts(xs, steps, w1, w3, w2, tm):
    n_rows, d = xs.shape
    d_ff = w1.shape[2]
    n_steps = steps[0].shape[0]
    grid_spec = pltpu.PrefetchScalarGridSpec(
        num_scalar_prefetch=6,
        grid=(n_steps,),
        in_specs=[
            pl.BlockSpec((tm, d), lambda s, tile, exp, lo, hi, first, valid: (tile[s], 0)),
            pl.BlockSpec((1, d, d_ff), lambda s, tile, exp, lo, hi, first, valid: (exp[s], 0, 0)),
            pl.BlockSpec((1, d, d_ff), lambda s, tile, exp, lo, hi, first, valid: (exp[s], 0, 0)),
            pl.BlockSpec((1, d_ff, d), lambda s, tile, exp, lo, hi, first, valid: (exp[s], 0, 0)),
        ],
        out_specs=pl.BlockSpec((tm, d), lambda s, tile, exp, lo, hi, first, valid: (tile[s], 0)),
        scratch_shapes=[pltpu.VMEM((tm, d_ff), BF16)],
    )
    return pl.pallas_call(
        functools.partial(_expert_kernel, tm),
        grid_spec=grid_spec,
        out_shape=jax.ShapeDtypeStruct((n_rows, d), F32),
        compiler_params=_params("arbitrary"),
        name="experts",
    )(*steps, xs, w1, w3, w2)


def _combine_kernel(tt, final, pos_ref, y_hbm, x_ref, mod_ref, meta_ref, fw_ref, o_ref, buf, sem):
    def issue(t, carry):
        _row_copy(y_hbm, pos_ref[0, 0, 2 * t], buf.at[0], t, sem).start()
        _row_copy(y_hbm, pos_ref[0, 0, 2 * t + 1], buf.at[1], t, sem).start()
        return carry

    lax.fori_loop(0, tt, issue, 0, unroll=8)
    for k in range(2):
        pltpu.make_async_copy(y_hbm.at[pl.ds(0, tt), :], buf.at[k], sem).wait()
    p1 = meta_ref[0, :, META_P:META_P + 1]
    p2 = meta_ref[0, :, META_P + 1:META_P + 2]
    out = x_ref[0] + mod_ref[0, 5:6, :] * (p1 * buf[0] + p2 * buf[1])
    o_ref[0] = _final_norm(out, fw_ref[...]) if final else out


def _combine(y, pos, x, mod, meta, fw, tt, final):
    bsz, seq, d = x.shape
    per_b = seq // tt
    return pl.pallas_call(
        functools.partial(_combine_kernel, tt, final),
        grid=(bsz, per_b),
        in_specs=[
            pl.BlockSpec((1, 1, 2 * tt), lambda b, i: (b * per_b + i, 0, 0), memory_space=pltpu.SMEM),
            pl.BlockSpec(memory_space=pl.ANY),
            pl.BlockSpec((1, tt, d), lambda b, i: (b, i, 0)),
            pl.BlockSpec((1, 6, d), lambda b, i: (b, 0, 0)),
            pl.BlockSpec((1, tt, LANES), lambda b, i: (b, i, 0)),
            pl.BlockSpec((1, d), lambda b, i: (0, 0)),
        ],
        out_specs=pl.BlockSpec((1, tt, d), lambda b, i: (b, i, 0)),
        out_shape=jax.ShapeDtypeStruct((bsz, seq, d), F32),
        scratch_shapes=[pltpu.VMEM((2, tt, d), F32), pltpu.SemaphoreType.DMA(())],
        compiler_params=_params("arbitrary", "arbitrary"),
        name="combine",
    )(pos.reshape(bsz * per_b, 1, 2 * tt), y, x, mod, meta, fw.reshape(1, d))


def _expert_steps(counts, n_rows, tm):
    n_tiles = n_rows // tm
    n_steps = n_tiles + N_EXPERTS - 1
    ends = jnp.cumsum(counts)
    starts = ends - counts
    first_tile = starts // tm
    last_tile = jnp.maximum(ends - 1, 0) // tm
    tiles_e = jnp.where(counts > 0, last_tile - first_tile + 1, 0)
    step_end = jnp.cumsum(tiles_e)
    step_start = step_end - tiles_e
    total = step_end[-1]
    s = jnp.arange(n_steps, dtype=jnp.int32)
    sc = jnp.minimum(s, jnp.maximum(total - 1, 0))
    e_s = jnp.minimum(jnp.sum((sc[:, None] >= step_end[None, :]).astype(jnp.int32), axis=1), N_EXPERTS - 1)
    tile_s = jnp.clip(first_tile[e_s] + (sc - step_start[e_s]), 0, n_tiles - 1)
    valid = (s < total).astype(jnp.int32)
    prev_tile = jnp.concatenate([jnp.full((1,), -1, jnp.int32), tile_s[:-1]])
    first = jnp.logical_and(tile_s != prev_tile, valid == 1).astype(jnp.int32)
    as_i32 = lambda v: v.astype(jnp.int32)
    return (as_i32(tile_s), as_i32(e_s), as_i32(starts[e_s]), as_i32(ends[e_s]), first, valid)


def _moe(x, mod, nw, fw, router_w, router_b, w1, w3, w2, tm_route, tt, tm_exp, final):
    bsz, seq, d = x.shape
    n_tok = bsz * seq
    h, meta, cnt = _router(x, mod, nw, router_w, router_b, tm_route)
    counts = cnt[0, :N_EXPERTS].astype(jnp.int32)
    starts = jnp.cumsum(counts) - counts
    meta2 = meta.reshape(n_tok, LANES)
    experts = meta2[:, META_E:META_E + 2].astype(jnp.int32)
    ranks = meta2[:, META_R:META_R + 2].astype(jnp.int32)
    onehot = experts[:, :, None] == jnp.arange(N_EXPERTS, dtype=jnp.int32)[None, None, :]
    pos = jnp.sum(jnp.where(onehot, starts[None, None, :], 0), axis=-1) + ranks
    xs = _dispatch(h.reshape(n_tok, d), pos, tt)
    y = _experts(xs, _expert_steps(counts, 2 * n_tok, tm_exp), w1, w3, w2, tm_exp)
    return _combine(y, pos, x, mod, meta, fw, tt, final)


def _pick(n, pref):
    t = min(n, pref)
    assert n % t == 0, (n, pref)
    return t


def kernel(x, c, ada_w, ada_b, norm_mix_w, norm_ffn_w, w_in, conv_w, hgrn_lb_logits, hgrn_norm_w,
           sgu_norm_w, sgu_w, sgu_b, w_br_a, w_br_b, w_br_c, w_o, ffn_w1, ffn_w3, ffn_w2,
           moe_router_w, moe_router_b, moe_w1, moe_w3, moe_w2, final_norm_w):
    depth = ada_w.shape[0]
    bsz, seq, d = x.shape
    assert w_in.shape[-1] == OFF_GATE + 3 * d and seq % CHUNK == 0
    assert ffn_w1.shape[-1] % MXU_N == 0 and moe_w1.shape[-1] % MXU_N == 0

    tm_in = _pick(seq, 1024)
    ts = _pick(seq, 512)
    tm_ffn = _pick(seq, 512)
    tm_route = _pick(seq, 1024)
    tt = _pick(seq, 256)
    tm_exp = _pick(2 * bsz * seq, 512)

    mod_all = _adaln(c, ada_w, ada_b).reshape(depth, bsz, 6, d)
    tables = _level_tables()
    bf = lambda w: w.astype(BF16)

    for l in range(depth):
        mod = mod_all[l]
        ya, hq, gc, gates = _inproj(l, x, mod, norm_mix_w[l], bf(w_in[l]), conv_w[l], hgrn_lb_logits,
                                    sgu_norm_w[l], tm_in)
        x = _mixer(ya, hq, gc, gates, x, mod, hgrn_norm_w[l], sgu_w[l], sgu_b[l], tables,
                   bf(w_br_a[l]), bf(w_br_b[l]), bf(w_br_c[l]), bf(w_o[l]), ts)
        final = l == depth - 1
        j = l // 2
        if l % 2 == 0:
            x = _ffn(x, mod, norm_ffn_w[l], final_norm_w, bf(ffn_w1[j]), bf(ffn_w3[j]), bf(ffn_w2[j]),
                     tm_ffn, final)
        else:
            x = _moe(x, mod, norm_ffn_w[l], final_norm_w, moe_router_w[j], moe_router_b[j],
                     bf(moe_w1[j]), bf(moe_w3[j]), bf(moe_w2[j]), tm_route, tt, tm_exp, final)
    return x
```

```python
import functools

import numpy as np
import jax
import jax.numpy as jnp
from jax import lax
from jax.experimental import pallas as pl
from jax.experimental.pallas import tpu as pltpu

F32 = jnp.float32
BF16 = jnp.bfloat16

EPS = 1e-6
LB_FLOOR = 1e-30

D_A = 512
CONV_WIDTH = 3
H_B = 4
DK_B = 128
DV_B = 128
D_BK = H_B * DK_B
D_BV = H_B * DV_B
G_C = 4
GC_CH = 128
D_C = G_C * GC_CH
SGU_CHUNK = 128
N_EXPERTS = 8

OFF_BQ = 3 * D_A
OFF_BG = OFF_BQ + 2 * D_BK + D_BV
OFF_GATE = OFF_BG + D_BV + 2 * D_C

CHUNK = 128
N_LEVELS = 7
assert 1 << N_LEVELS == CHUNK and CHUNK == SGU_CHUNK

LANES = 128
MXU_N = 256
VMEM_LIMIT = 60 * 1024 * 1024
NEG_BIG = -1e30


def _sigmoid(v):
    return 0.5 + 0.5 * jnp.tanh(0.5 * v)


def _silu(v):
    return v * _sigmoid(v)


def _dot(a, b):
    return jnp.dot(a, b, preferred_element_type=F32)


def _dot_nt(a, b):
    return lax.dot_general(a, b, (((1,), (1,)), ((), ())), preferred_element_type=F32)


def _dot_tn(a, b):
    return lax.dot_general(a, b, (((0,), (0,)), ((), ())), preferred_element_type=F32)


def _norm_mod(x, nw, shift, scale):
    ms = jnp.mean(x * x, axis=-1, keepdims=True)
    return (x * lax.rsqrt(ms + EPS) * nw) * (1.0 + scale) + shift


def _params(*sem):
    return pltpu.CompilerParams(dimension_semantics=sem, vmem_limit_bytes=VMEM_LIMIT)


def _adaln_kernel(c_ref, w_ref, b_ref, o_ref):
    ca = _silu(c_ref[...]).astype(BF16)
    o_ref[0] = _dot(ca, w_ref[0].astype(BF16)) + b_ref[0]


def _adaln(c, ada_w, ada_b):
    depth, d, n = ada_w.shape
    bsz = c.shape[0]
    tn = n // 4
    return pl.pallas_call(
        _adaln_kernel,
        grid=(depth, n // tn),
        in_specs=[
            pl.BlockSpec((bsz, d), lambda l, j: (0, 0)),
            pl.BlockSpec((1, d, tn), lambda l, j: (l, 0, j)),
            pl.BlockSpec((1, 1, tn), lambda l, j: (l, 0, j)),
        ],
        out_specs=pl.BlockSpec((1, bsz, tn), lambda l, j: (l, 0, j)),
        out_shape=jax.ShapeDtypeStruct((depth, bsz, n), F32),
        compiler_params=_params("arbitrary", "arbitrary"),
        name="adaln",
    )(c, ada_w, ada_b.reshape(depth, 1, n))


TN_IN = 3 * D_A
HQ_Q, HQ_F, HQ_K, HQ_V = 0, D_BK, 2 * D_BK, 3 * D_BK
GC_G, GC_U, GC_V = 0, D_BV, D_BV + D_C
assert OFF_BQ == TN_IN and OFF_BG == 2 * TN_IN and OFF_GATE == 3 * TN_IN and D_BK == D_BV == D_C == D_A


def _inproj_kernel(layer, x_ref, mod_ref, nw_ref, w_ref, convw_ref, lbl_ref, snw_ref,
                   ya_ref, hq_ref, gc_ref, gates_ref, h_ref, zbuf):
    i = pl.program_id(1)
    j = pl.program_id(2)
    tm = x_ref.shape[1]

    def group(g):
        return _dot(h_ref[...], w_ref[:, g * D_A:(g + 1) * D_A])

    @pl.when(jnp.logical_and(j == 0, i == 0))
    def _():
        zbuf[0:8, :] = jnp.zeros((8, D_A), F32)

    @pl.when(j == 0)
    def _():
        h = _norm_mod(x_ref[0], nw_ref[...], mod_ref[0, 0:1, :], mod_ref[0, 1:2, :])
        h_ref[...] = h.astype(BF16)
        c0 = group(0)
        z = group(1) * group(2)
        zbuf[8:8 + tm, :] = z
        y = (convw_ref[0:1, :] * z + convw_ref[1:2, :] * zbuf[7:7 + tm, :]
             + convw_ref[2:3, :] * zbuf[6:6 + tm, :])
        ya_ref[0] = (c0 * y).astype(BF16)
        zbuf[0:8, :] = zbuf[tm:tm + 8, :]

    @pl.when(j == 1)
    def _():
        lg = lbl_ref[...]
        pe = jnp.exp(lg - jnp.max(lg, axis=0, keepdims=True))
        p = pe / jnp.sum(pe, axis=0, keepdims=True)
        lb = jnp.clip(jnp.sum(p[0:layer + 1], axis=0, keepdims=True) - p[0:1], 0.0, 1.0)
        hq_ref[0, :, HQ_Q:HQ_Q + D_BK] = _silu(group(0)).astype(BF16)
        hq_ref[0, :, HQ_V:HQ_V + D_BV] = group(2).astype(BF16)
        sig = _sigmoid(group(1))
        hq_ref[0, :, HQ_F:HQ_F + D_BK] = jnp.log(jnp.maximum(lb, LB_FLOOR) + (1.0 - lb) * sig).astype(BF16)
        hq_ref[0, :, HQ_K:HQ_K + D_BK] = ((1.0 - lb) * (1.0 - sig)).astype(BF16)

    @pl.when(j == 2)
    def _():
        gc_ref[0, :, GC_G:GC_G + D_BV] = _silu(group(0)).astype(BF16)
        gc_ref[0, :, GC_U:GC_U + D_C] = group(1).astype(BF16)
        cv = group(2)
        vn = cv * lax.rsqrt(jnp.mean(cv * cv, axis=-1, keepdims=True) + EPS) * snw_ref[...]
        gc_ref[0, :, GC_V:GC_V + D_C] = vn.astype(BF16)

    @pl.when(j >= 3)
    def _():
        for g in range(TN_IN // D_A):
            gates_ref[0, :, g * D_A:(g + 1) * D_A] = _sigmoid(group(g)).astype(BF16)


def _inproj(layer, x, mod, nw, w_bf16, conv_w, lb_logits, sgu_nw, tm):
    bsz, seq, d = x.shape
    n = w_bf16.shape[1]
    assert TN_IN == d + d // 2 and n == 5 * TN_IN
    row = lambda b, i, j: (b, i, 0)
    const2 = lambda b, i, j: (0, 0)
    out = lambda width: jax.ShapeDtypeStruct((bsz, seq, width), BF16)
    return pl.pallas_call(
        functools.partial(_inproj_kernel, layer),
        grid=(bsz, seq // tm, n // TN_IN),
        in_specs=[
            pl.BlockSpec((1, tm, d), row),
            pl.BlockSpec((1, 6, d), lambda b, i, j: (b, 0, 0)),
            pl.BlockSpec((1, d), const2),
            pl.BlockSpec((d, TN_IN), lambda b, i, j: (0, j)),
            pl.BlockSpec(conv_w.shape, const2),
            pl.BlockSpec(lb_logits.shape, const2),
            pl.BlockSpec((1, D_C), const2),
        ],
        out_specs=[
            pl.BlockSpec((1, tm, D_A), row),
            pl.BlockSpec((1, tm, 4 * D_BK), row),
            pl.BlockSpec((1, tm, 3 * D_C), row),
            pl.BlockSpec((1, tm, TN_IN), lambda b, i, j: (b, i, jnp.maximum(j - 3, 0))),
        ],
        out_shape=[out(D_A), out(4 * D_BK), out(3 * D_C), out(3 * d)],
        scratch_shapes=[pltpu.VMEM((tm, d), BF16), pltpu.VMEM((tm + 8, D_A), F32)],
        compiler_params=_params("arbitrary", "arbitrary", "arbitrary"),
        name="inproj",
    )(x, mod, nw.reshape(1, d), w_bf16, conv_w, lb_logits, sgu_nw.reshape(1, D_C))


def _level_tables():
    t = np.arange(CHUNK)
    sum_mat = np.zeros((N_LEVELS + 1, CHUNK, CHUNK), np.float32)
    sum_mat[0] = (t[None, :] <= t[:, None])
    up = np.zeros((N_LEVELS + 1, CHUNK, LANES), np.float32)
    low = np.zeros((N_LEVELS + 1, CHUNK, LANES), np.float32)
    pair = np.zeros((N_LEVELS + 1, CHUNK, CHUNK), np.float32)
    pair[0] = np.eye(CHUNK)
    for j in range(1, N_LEVELS + 1):
        blk, half = 1 << j, 1 << (j - 1)
        base = (t // blk) * blk
        m = base + half - 1
        is_up = (t - base) >= half
        u = t[None, :]
        upper_rows = (u > m[:, None]) & (u <= t[:, None])
        lower_rows = (u > t[:, None]) & (u <= m[:, None])
        sum_mat[j] = np.where(is_up[:, None], upper_rows, lower_rows)
        up[j] = is_up[:, None]
        low[j] = ~is_up[:, None]
        pair[j] = (t[:, None] // blk) == (t[None, :] // blk)
    pairc = np.zeros((N_LEVELS + 1, CHUNK // 2, CHUNK), np.float32)
    for j in range(1, N_LEVELS + 1):
        pairc[j] = pair[j][np.nonzero(((t >> (j - 1)) & 1) == 1)[0]]
    return (jnp.asarray(sum_mat.reshape((N_LEVELS + 1) * CHUNK, CHUNK), BF16),
            jnp.asarray(up), jnp.asarray(low), jnp.asarray(pair), jnp.asarray(pairc))


HEAD_GROUPS = ((0, 1), (2, 3))
SUBLANES = 8
MERGE_ROWS = 128


def _row_blocks(m):
    return [m[SUBLANES * r:SUBLANES * (r + 1), :] for r in range(m.shape[0] // SUBLANES)]


def _mixer_kernel(ts,
                  ya_ref, hq_ref, gc_ref, gates_ref, x_ref, mod_ref, hnw_ref, sguw_ref, sgub_ref,
                  summat_ref, up_ref, low_ref, pair_ref, pairc_ref, wa_ref, wb_ref, wc_ref, wo_ref,
                  o_ref,
                  st_ref, yb_ref, yc_ref, wsm_ref, br_ref, mg_ref, kt_ref):
    n_chunks = ts // CHUNK
    d = x_ref.shape[-1]

    @pl.when(pl.program_id(1) == 0)
    def _():
        st_ref[...] = jnp.zeros_like(st_ref)

    tril = (lax.broadcasted_iota(jnp.int32, (SGU_CHUNK, SGU_CHUNK), 0)
            >= lax.broadcasted_iota(jnp.int32, (SGU_CHUNK, SGU_CHUNK), 1))
    for g in range(G_C):
        wsm_ref[g] = jnp.where(tril, sguw_ref[g], 0.0).astype(BF16)

    def chunk_body(c, carry):
        r0 = pl.multiple_of(c * CHUNK, CHUNK)
        rows = pl.ds(r0, CHUNK)

        for g in range(G_C):
            cs = slice(g * GC_CH, (g + 1) * GC_CH)
            vn = gc_ref[0, rows, GC_V + g * GC_CH:GC_V + (g + 1) * GC_CH]
            u = gc_ref[0, rows, GC_U + g * GC_CH:GC_U + (g + 1) * GC_CH].astype(F32)
            yc_ref[rows, cs] = (u * (_dot(wsm_ref[g], vn) + sgub_ref[g])).astype(BF16)

        lv = _dot(summat_ref[...], hq_ref[0, rows, HQ_F:HQ_F + D_BK])

        for heads in HEAD_GROUPS:
            cs = {h: slice(h * DK_B, (h + 1) * DK_B) for h in heads}
            q_bf = {h: hq_ref[0, rows, HQ_Q + h * DK_B:HQ_Q + (h + 1) * DK_B] for h in heads}
            k_bf = {h: hq_ref[0, rows, HQ_K + h * DK_B:HQ_K + (h + 1) * DK_B] for h in heads}
            v_h = {h: hq_ref[0, rows, HQ_V + h * DV_B:HQ_V + (h + 1) * DV_B] for h in heads}
            q_h = {h: q_bf[h].astype(F32) for h in heads}
            k_h = {h: k_bf[h].astype(F32) for h in heads}
            scores = {}
            for h in heads:
                slot = h * (N_LEVELS + 1)
                kt_ref[slot] = k_bf[h].T
                scores[h] = _row_blocks(pair_ref[0] * _dot(q_bf[h], kt_ref[slot]))
            for j in range(1, N_LEVELS + 1):
                half = 1 << (j - 1)
                for h in heads:
                    e = jnp.exp(lv[j * CHUNK:(j + 1) * CHUNK, cs[h]])
                    slot = h * (N_LEVELS + 1) + j
                    if half < SUBLANES:
                        q_j = (q_h[h] * e * up_ref[j]).astype(BF16)
                        k_j = (k_h[h] * e * low_ref[j]).astype(BF16)
                        kt_ref[slot] = k_j.T
                        s_j = _row_blocks(pair_ref[j] * _dot(q_j, kt_ref[slot]))
                        scores[h] = [a + b for a, b in zip(scores[h], s_j)]
                    else:
                        zero = jnp.zeros((half, DK_B), F32)
                        k_parts, q_parts, up_blocks = [], [], []
                        for a in range(0, CHUNK, 2 * half):
                            k_parts += [k_h[h][a:a + half] * e[a:a + half], zero]
                            q_parts.append(q_h[h][a + half:a + 2 * half] * e[a + half:a + 2 * half])
                            up_blocks += list(range((a + half) // SUBLANES, (a + 2 * half) // SUBLANES))
                        kt_ref[slot] = jnp.concatenate(k_parts, axis=0).astype(BF16).T
                        q_c = jnp.concatenate(q_parts, axis=0).astype(BF16)
                        s_c = _row_blocks(pairc_ref[j] * _dot(q_c, kt_ref[slot]))
                        for r, blk in zip(up_blocks, s_c):
                            scores[h][r] = scores[h][r] + blk
            for h in heads:
                b_h = lv[0:CHUNK, cs[h]]
                st = st_ref[h]
                o = _dot_nt((q_h[h] * jnp.exp(b_h)).astype(BF16), st.astype(BF16))
                o = o + _dot(jnp.concatenate(scores[h], axis=0).astype(BF16), v_h[h])
                b_last = b_h[CHUNK - 1:CHUNK, :]
                k_dec = (k_h[h] * jnp.exp(b_last - b_h)).astype(BF16)
                st_ref[h] = st * jnp.exp(b_last) + _dot_tn(v_h[h], k_dec)
                on = o * lax.rsqrt(jnp.mean(o * o, axis=-1, keepdims=True) + EPS) * hnw_ref[:, cs[h]]
                sg = gc_ref[0, rows, GC_G + h * DV_B:GC_G + (h + 1) * DV_B].astype(F32)
                yb_ref[rows, cs[h]] = (on * sg).astype(BF16)
        return carry

    lax.fori_loop(0, n_chunks, chunk_body, 0)

    br_ref[0] = _dot(ya_ref[0], wa_ref[...])
    br_ref[1] = _dot(yb_ref[...], wb_ref[...])
    br_ref[2] = _dot(yc_ref[...], wc_ref[...])

    def merge_body(r, carry):
        rows = pl.ds(pl.multiple_of(r * MERGE_ROWS, MERGE_ROWS), MERGE_ROWS)
        merged = gates_ref[0, rows, 0:d].astype(F32) * br_ref[0, rows, :]
        merged = merged + gates_ref[0, rows, d:2 * d].astype(F32) * br_ref[1, rows, :]
        merged = merged + gates_ref[0, rows, 2 * d:3 * d].astype(F32) * br_ref[2, rows, :]
        mg_ref[rows, :] = merged.astype(BF16)
        return carry

    lax.fori_loop(0, ts // MERGE_ROWS, merge_body, 0)
    o_ref[0] = x_ref[0] + mod_ref[0, 2:3, :] * _dot(mg_ref[...], wo_ref[...])


def _mixer(ya, hq, gc, gates, x, mod, hgrn_nw, sgu_w, sgu_b, tables, wa, wb, wc, wo, ts):
    bsz, seq, d = x.shape
    sum_mat, up, low, pair, pairc = tables
    sgub = jnp.broadcast_to(sgu_b[:, :, None], (G_C, SGU_CHUNK, GC_CH))
    row = lambda b, i: (b, i, 0)
    const2 = lambda b, i: (0, 0)
    const3 = lambda b, i: (0, 0, 0)
    return pl.pallas_call(
        functools.partial(_mixer_kernel, ts),
        grid=(bsz, seq // ts),
        in_specs=[
            pl.BlockSpec((1, ts, ya.shape[-1]), row),
            pl.BlockSpec((1, ts, hq.shape[-1]), row),
            pl.BlockSpec((1, ts, gc.shape[-1]), row),
            pl.BlockSpec((1, ts, gates.shape[-1]), row),
            pl.BlockSpec((1, ts, d), row),
            pl.BlockSpec((1, 6, d), lambda b, i: (b, 0, 0)),
            pl.BlockSpec((1, D_BV), const2),
            pl.BlockSpec(sgu_w.shape, const3),
            pl.BlockSpec(sgub.shape, const3),
            pl.BlockSpec(sum_mat.shape, const2),
            pl.BlockSpec(up.shape, const3),
            pl.BlockSpec(low.shape, const3),
            pl.BlockSpec(pair.shape, const3),
            pl.BlockSpec(pairc.shape, const3),
            pl.BlockSpec(wa.shape, const2),
            pl.BlockSpec(wb.shape, const2),
            pl.BlockSpec(wc.shape, const2),
            pl.BlockSpec(wo.shape, const2),
        ],
        out_specs=pl.BlockSpec((1, ts, d), row),
        out_shape=jax.ShapeDtypeStruct((bsz, seq, d), F32),
        scratch_shapes=[
            pltpu.VMEM((H_B, DV_B, DK_B), F32),
            pltpu.VMEM((ts, D_BV), BF16),
            pltpu.VMEM((ts, D_C), BF16),
            pltpu.VMEM((G_C, SGU_CHUNK, SGU_CHUNK), BF16),
            pltpu.VMEM((3, ts, d), F32),
            pltpu.VMEM((ts, d), BF16),
            pltpu.VMEM((H_B * (N_LEVELS + 1), DK_B, CHUNK), BF16),
        ],
        compiler_params=_params("arbitrary", "arbitrary"),
        name="mixer",
    )(ya, hq, gc, gates, x, mod, hgrn_nw.reshape(1, D_BV), sgu_w, sgub,
      sum_mat, up, low, pair, pairc, wa, wb, wc, wo)


def _swiglu_tile(h, w1_at, w3_at, w2, g_ref):
    d_ff = g_ref.shape[1]
    for c0 in range(0, d_ff, MXU_N):
        a = _dot(h, w1_at(c0))
        g_ref[:, c0:c0 + MXU_N] = (_silu(a) * _dot(h, w3_at(c0))).astype(BF16)
    return _dot(g_ref[...], w2)


def _final_norm(v, fw):
    return v * lax.rsqrt(jnp.mean(v * v, axis=-1, keepdims=True) + EPS) * fw


def _ffn_kernel(final, x_ref, mod_ref, nw_ref, fw_ref, w1_ref, w3_ref, w2_ref, o_ref, g_ref):
    x = x_ref[0]
    h = _norm_mod(x, nw_ref[...], mod_ref[0, 3:4, :], mod_ref[0, 4:5, :]).astype(BF16)
    y = _swiglu_tile(h, lambda c0: w1_ref[:, c0:c0 + MXU_N], lambda c0: w3_ref[:, c0:c0 + MXU_N],
                     w2_ref[...], g_ref)
    out = x + mod_ref[0, 5:6, :] * y
    o_ref[0] = _final_norm(out, fw_ref[...]) if final else out


def _ffn(x, mod, nw, fw, w1, w3, w2, tm, final):
    bsz, seq, d = x.shape
    d_ff = w1.shape[1]
    const2 = lambda b, i: (0, 0)
    resident = dict(pipeline_mode=pl.Buffered(1))
    return pl.pallas_call(
        functools.partial(_ffn_kernel, final),
        grid=(bsz, seq // tm),
        in_specs=[
            pl.BlockSpec((1, tm, d), lambda b, i: (b, i, 0)),
            pl.BlockSpec((1, 6, d), lambda b, i: (b, 0, 0)),
            pl.BlockSpec((1, d), const2),
            pl.BlockSpec((1, d), const2),
            pl.BlockSpec((d, d_ff), const2, **resident),
            pl.BlockSpec((d, d_ff), const2, **resident),
            pl.BlockSpec((d_ff, d), const2, **resident),
        ],
        out_specs=pl.BlockSpec((1, tm, d), lambda b, i: (b, i, 0)),
        out_shape=jax.ShapeDtypeStruct((bsz, seq, d), F32),
        scratch_shapes=[pltpu.VMEM((tm, d_ff), BF16)],
        compiler_params=_params("arbitrary", "arbitrary"),
        name="ffn",
    )(x, mod, nw.reshape(1, d), fw.reshape(1, d), w1, w3, w2)


META_E, META_P, META_R = 0, 2, 4


def _router_kernel(x_ref, mod_ref, nw_ref, rw_ref, rb_ref, tri_ref, h_ref, meta_ref, cnt_ref, carry):
    first = jnp.logical_and(pl.program_id(0) == 0, pl.program_id(1) == 0)

    @pl.when(first)
    def _():
        carry[...] = jnp.zeros_like(carry)

    h = _norm_mod(x_ref[0], nw_ref[...], mod_ref[0, 3:4, :], mod_ref[0, 4:5, :])
    h_ref[0] = h
    h_hi = h.astype(BF16)
    h_lo = (h - h_hi.astype(F32)).astype(BF16)
    part = _dot(h_hi, rw_ref[...])
    logits = (part[:, 0:LANES] + part[:, LANES:2 * LANES]) + _dot(h_lo, rw_ref[:, 0:LANES]) + rb_ref[...]
    lane = lax.broadcasted_iota(jnp.int32, logits.shape, 1)
    m1 = jnp.max(logits, axis=-1, keepdims=True)
    i1 = jnp.min(jnp.where(logits == m1, lane, LANES), axis=-1, keepdims=True)
    rest = jnp.where(lane == i1, -jnp.inf, logits)
    m2 = jnp.max(rest, axis=-1, keepdims=True)
    i2 = jnp.min(jnp.where(rest == m2, lane, LANES), axis=-1, keepdims=True)
    e2 = jnp.exp(m2 - m1)
    p1 = 1.0 / (1.0 + e2)
    p2 = e2 / (1.0 + e2)
    sel1 = lane == i1
    sel2 = lane == i2
    onehot = jnp.where(jnp.logical_or(sel1, sel2), 1.0, 0.0)
    before = _dot(tri_ref[...], onehot.astype(BF16)) + carry[...]
    r1 = jnp.sum(jnp.where(sel1, before, 0.0), axis=-1, keepdims=True)
    r2 = jnp.sum(jnp.where(sel2, before, 0.0), axis=-1, keepdims=True)
    carry[...] = carry[...] + jnp.sum(onehot, axis=0, keepdims=True)
    cnt_ref[...] = carry[...]
    rec = jnp.where(lane == META_E, i1.astype(F32), 0.0)
    rec = jnp.where(lane == META_E + 1, i2.astype(F32), rec)
    rec = jnp.where(lane == META_P, p1, rec)
    rec = jnp.where(lane == META_P + 1, p2, rec)
    rec = jnp.where(lane == META_R, r1, rec)
    rec = jnp.where(lane == META_R + 1, r2, rec)
    meta_ref[0] = rec


def _router(x, mod, nw, router_w, router_b, tm):
    bsz, seq, d = x.shape
    ne = router_w.shape[1]
    rw = jnp.zeros((d, LANES), F32).at[:, :ne].set(router_w)
    rw_hi = rw.astype(BF16)
    rw = jnp.concatenate([rw_hi, (rw - rw_hi.astype(F32)).astype(BF16)], axis=1)
    rb = jnp.full((1, LANES), NEG_BIG, F32).at[0, :ne].set(router_b)
    tri = jnp.asarray(np.tril(np.ones((tm, tm), np.float32), -1), BF16)
    const2 = lambda b, i: (0, 0)
    return pl.pallas_call(
        _router_kernel,
        grid=(bsz, seq // tm),
        in_specs=[
            pl.BlockSpec((1, tm, d), lambda b, i: (b, i, 0)),
            pl.BlockSpec((1, 6, d), lambda b, i: (b, 0, 0)),
            pl.BlockSpec((1, d), const2),
            pl.BlockSpec((d, 2 * LANES), const2),
            pl.BlockSpec((1, LANES), const2),
            pl.BlockSpec((tm, tm), const2),
        ],
        out_specs=[
            pl.BlockSpec((1, tm, d), lambda b, i: (b, i, 0)),
            pl.BlockSpec((1, tm, LANES), lambda b, i: (b, i, 0)),
            pl.BlockSpec((1, LANES), const2),
        ],
        out_shape=[
            jax.ShapeDtypeStruct((bsz, seq, d), F32),
            jax.ShapeDtypeStruct((bsz, seq, LANES), F32),
            jax.ShapeDtypeStruct((1, LANES), F32),
        ],
        scratch_shapes=[pltpu.VMEM((1, LANES), F32)],
        compiler_params=_params("arbitrary", "arbitrary"),
        name="router",
    )(x, mod, nw.reshape(1, d), rw, rb, tri)


def _row_copy(src, src_row, dst, dst_row, sem):
    return pltpu.make_async_copy(src.at[pl.ds(src_row, 1), :], dst.at[pl.ds(dst_row, 1), :], sem)


def _dispatch_kernel(tt, pos_ref, h_ref, xs_hbm, sem):
    def issue(t, carry):
        _row_copy(h_ref, t, xs_hbm, pos_ref[0, 0, 2 * t], sem).start()
        _row_copy(h_ref, t, xs_hbm, pos_ref[0, 0, 2 * t + 1], sem).start()
        return carry

    lax.fori_loop(0, tt, issue, 0, unroll=8)
    for _ in range(2):
        pltpu.make_async_copy(h_ref, xs_hbm.at[pl.ds(0, tt), :], sem).wait()


def _dispatch(h2d, pos, tt):
    n_tok, d = h2d.shape
    return pl.pallas_call(
        functools.partial(_dispatch_kernel, tt),
        grid=(n_tok // tt,),
        in_specs=[
            pl.BlockSpec((1, 1, 2 * tt), lambda i: (i, 0, 0), memory_space=pltpu.SMEM),
            pl.BlockSpec((tt, d), lambda i: (i, 0)),
        ],
        out_specs=pl.BlockSpec(memory_space=pl.ANY),
        out_shape=jax.ShapeDtypeStruct((2 * n_tok, d), F32),
        scratch_shapes=[pltpu.SemaphoreType.DMA(())],
        compiler_params=_params("arbitrary"),
        name="dispatch",
    )(pos.reshape(n_tok // tt, 1, 2 * tt), h2d)


def _expert_kernel(tm, tile_ref, exp_ref, lo_ref, hi_ref, first_ref, valid_ref,
                   xs_ref, w1_ref, w3_ref, w2_ref, y_ref, g_ref):
    s = pl.program_id(0)

    @pl.when(valid_ref[s] == 1)
    def _():
        h = xs_ref[...].astype(BF16)
        res = _swiglu_tile(h, lambda c0: w1_ref[0, :, c0:c0 + MXU_N],
                           lambda c0: w3_ref[0, :, c0:c0 + MXU_N], w2_ref[0], g_ref)
        row = tile_ref[s] * tm + lax.broadcasted_iota(jnp.int32, (tm, 1), 0)
        mine = jnp.logical_and(row >= lo_ref[s], row < hi_ref[s])

        @pl.when(first_ref[s] == 1)
        def _():
            y_ref[...] = jnp.where(mine, res, 0.0)

        @pl.when(first_ref[s] == 0)
        def _():
            y_ref[...] = jnp.where(mine, res, y_ref[...])


def _experts(xs, steps, w1, w3, w2, tm):
    n_rows, d = xs.shape
    d_ff = w1.shape[2]
    n_steps = steps[0].shape[0]
    grid_spec = pltpu.PrefetchScalarGridSpec(
        num_scalar_prefetch=6,
        grid=(n_steps,),
        in_specs=[
            pl.BlockSpec((tm, d), lambda s, tile, exp, lo, hi, first, valid: (tile[s], 0)),
            pl.BlockSpec((1, d, d_ff), lambda s, tile, exp, lo, hi, first, valid: (exp[s], 0, 0)),
            pl.BlockSpec((1, d, d_ff), lambda s, tile, exp, lo, hi, first, valid: (exp[s], 0, 0)),
            pl.BlockSpec((1, d_ff, d), lambda s, tile, exp, lo, hi, first, valid: (exp[s], 0, 0)),
        ],
        out_specs=pl.BlockSpec((tm, d), lambda s, tile, exp, lo, hi, first, valid: (tile[s], 0)),
        scratch_shapes=[pltpu.VMEM((tm, d_ff), BF16)],
    )
    return pl.pallas_call(
        functools.partial(_expert_kernel, tm),
        grid_spec=grid_spec,
        out_shape=jax.ShapeDtypeStruct((n_rows, d), F32),
        compiler_params=_params("arbitrary"),
        name="experts",
    )(*steps, xs, w1, w3, w2)


def _combine_kernel(tt, final, pos_ref, y_hbm, x_ref, mod_ref, meta_ref, fw_ref, o_ref, buf, sem):
    def issue(t, carry):
        _row_copy(y_hbm, pos_ref[0, 0, 2 * t], buf.at[0], t, sem).start()
        _row_copy(y_hbm, pos_ref[0, 0, 2 * t + 1], buf.at[1], t, sem).start()
        return carry

    lax.fori_loop(0, tt, issue, 0, unroll=8)
    for k in range(2):
        pltpu.make_async_copy(y_hbm.at[pl.ds(0, tt), :], buf.at[k], sem).wait()
    p1 = meta_ref[0, :, META_P:META_P + 1]
    p2 = meta_ref[0, :, META_P + 1:META_P + 2]
    out = x_ref[0] + mod_ref[0, 5:6, :] * (p1 * buf[0] + p2 * buf[1])
    o_ref[0] = _final_norm(out, fw_ref[...]) if final else out


def _combine(y, pos, x, mod, meta, fw, tt, final):
    bsz, seq, d = x.shape
    per_b = seq // tt
    return pl.pallas_call(
        functools.partial(_combine_kernel, tt, final),
        grid=(bsz, per_b),
        in_specs=[
            pl.BlockSpec((1, 1, 2 * tt), lambda b, i: (b * per_b + i, 0, 0), memory_space=pltpu.SMEM),
            pl.BlockSpec(memory_space=pl.ANY),
            pl.BlockSpec((1, tt, d), lambda b, i: (b, i, 0)),
            pl.BlockSpec((1, 6, d), lambda b, i: (b, 0, 0)),
            pl.BlockSpec((1, tt, LANES), lambda b, i: (b, i, 0)),
            pl.BlockSpec((1, d), lambda b, i: (0, 0)),
        ],
        out_specs=pl.BlockSpec((1, tt, d), lambda b, i: (b, i, 0)),
        out_shape=jax.ShapeDtypeStruct((bsz, seq, d), F32),
        scratch_shapes=[pltpu.VMEM((2, tt, d), F32), pltpu.SemaphoreType.DMA(())],
        compiler_params=_params("arbitrary", "arbitrary"),
        name="combine",
    )(pos.reshape(bsz * per_b, 1, 2 * tt), y, x, mod, meta, fw.reshape(1, d))


def _expert_steps(counts, n_rows, tm):
    n_tiles = n_rows // tm
    n_steps = n_tiles + N_EXPERTS - 1
    ends = jnp.cumsum(counts)
    starts = ends - counts
    first_tile = starts // tm
    last_tile = jnp.maximum(ends - 1, 0) // tm
    tiles_e = jnp.where(counts > 0, last_tile - first_tile + 1, 0)
    step_end = jnp.cumsum(tiles_e)
    step_start = step_end - tiles_e
    total = step_end[-1]
    s = jnp.arange(n_steps, dtype=jnp.int32)
    sc = jnp.minimum(s, jnp.maximum(total - 1, 0))
    e_s = jnp.minimum(jnp.sum((sc[:, None] >= step_end[None, :]).astype(jnp.int32), axis=1), N_EXPERTS - 1)
    tile_s = jnp.clip(first_tile[e_s] + (sc - step_start[e_s]), 0, n_tiles - 1)
    valid = (s < total).astype(jnp.int32)
    prev_tile = jnp.concatenate([jnp.full((1,), -1, jnp.int32), tile_s[:-1]])
    first = jnp.logical_and(tile_s != prev_tile, valid == 1).astype(jnp.int32)
    as_i32 = lambda v: v.astype(jnp.int32)
    return (as_i32(tile_s), as_i32(e_s), as_i32(starts[e_s]), as_i32(ends[e_s]), first, valid)


def _moe(x, mod, nw, fw, router_w, router_b, w1, w3, w2, tm_route, tt, tm_exp, final):
    bsz, seq, d = x.shape
    n_tok = bsz * seq
    h, meta, cnt = _router(x, mod, nw, router_w, router_b, tm_route)
    counts = cnt[0, :N_EXPERTS].astype(jnp.int32)
    starts = jnp.cumsum(counts) - counts
    meta2 = meta.reshape(n_tok, LANES)
    experts = meta2[:, META_E:META_E + 2].astype(jnp.int32)
    ranks = meta2[:, META_R:META_R + 2].astype(jnp.int32)
    onehot = experts[:, :, None] == jnp.arange(N_EXPERTS, dtype=jnp.int32)[None, None, :]
    pos = jnp.sum(jnp.where(onehot, starts[None, None, :], 0), axis=-1) + ranks
    xs = _dispatch(h.reshape(n_tok, d), pos, tt)
    y = _experts(xs, _expert_steps(counts, 2 * n_tok, tm_exp), w1, w3, w2, tm_exp)
    return _combine(y, pos, x, mod, meta, fw, tt, final)


def _pick(n, pref):
    t = min(n, pref)
    assert n % t == 0, (n, pref)
    return t


def kernel(x, c, ada_w, ada_b, norm_mix_w, norm_ffn_w, w_in, conv_w, hgrn_lb_logits, hgrn_norm_w,
           sgu_norm_w, sgu_w, sgu_b, w_br_a, w_br_b, w_br_c, w_o, ffn_w1, ffn_w3, ffn_w2,
           moe_router_w, moe_router_b, moe_w1, moe_w3, moe_w2, final_norm_w):
    depth = ada_w.shape[0]
    bsz, seq, d = x.shape
    assert w_in.shape[-1] == OFF_GATE + 3 * d and seq % CHUNK == 0
    assert ffn_w1.shape[-1] % MXU_N == 0 and moe_w1.shape[-1] % MXU_N == 0

    tm_in = _pick(seq, 1024)
    ts = _pick(seq, 512)
    tm_ffn = _pick(seq, 512)
    tm_route = _pick(seq, 1024)
    tt = _pick(seq, 256)
    tm_exp = _pick(2 * bsz * seq, 512)

    mod_all = _adaln(c, ada_w, ada_b).reshape(depth, bsz, 6, d)
    tables = _level_tables()
    bf = lambda w: w.astype(BF16)

    for l in range(depth):
        mod = mod_all[l]
        ya, hq, gc, gates = _inproj(l, x, mod, norm_mix_w[l], bf(w_in[l]), conv_w[l], hgrn_lb_logits,
                                    sgu_norm_w[l], tm_in)
        x = _mixer(ya, hq, gc, gates, x, mod, hgrn_norm_w[l], sgu_w[l], sgu_b[l], tables,
                   bf(w_br_a[l]), bf(w_br_b[l]), bf(w_br_c[l]), bf(w_o[l]), ts)
        final = l == depth - 1
        j = l // 2
        if l % 2 == 0:
            x = _ffn(x, mod, norm_ffn_w[l], final_norm_w, bf(ffn_w1[j]), bf(ffn_w3[j]), bf(ffn_w2[j]),
                     tm_ffn, final)
        else:
            x = _moe(x, mod, norm_ffn_w[l], final_norm_w, moe_router_w[j], moe_router_b[j],
                     bf(moe_w1[j]), bf(moe_w3[j]), bf(moe_w2[j]), tm_route, tt, tm_exp, final)
    return x
```

```python
import functools

import numpy as np
import jax
import jax.numpy as jnp
from jax import lax
from jax.experimental import pallas as pl
from jax.experimental.pallas import tpu as pltpu

F32 = jnp.float32
BF16 = jnp.bfloat16

EPS = 1e-6
LB_FLOOR = 1e-30

D_A = 512
CONV_WIDTH = 3
H_B = 4
DK_B = 128
DV_B = 128
D_BK = H_B * DK_B
D_BV = H_B * DV_B
G_C = 4
GC_CH = 128
D_C = G_C * GC_CH
SGU_CHUNK = 128
N_EXPERTS = 8

OFF_BQ = 3 * D_A
OFF_BG = OFF_BQ + 2 * D_BK + D_BV
OFF_GATE = OFF_BG + D_BV + 2 * D_C

CHUNK = 128
N_LEVELS = 7
assert 1 << N_LEVELS == CHUNK and CHUNK == SGU_CHUNK

LANES = 128
MXU_N = 256
VMEM_LIMIT = 60 * 1024 * 1024
NEG_BIG = -1e30


def _sigmoid(v):
    return 0.5 + 0.5 * jnp.tanh(0.5 * v)


def _silu(v):
    return v * _sigmoid(v)


def _dot(a, b):
    return jnp.dot(a, b, preferred_element_type=F32)


def _dot_nt(a, b):
    return lax.dot_general(a, b, (((1,), (1,)), ((), ())), preferred_element_type=F32)


def _dot_tn(a, b):
    return lax.dot_general(a, b, (((0,), (0,)), ((), ())), preferred_element_type=F32)


def _norm_mod(x, nw, shift, scale):
    ms = jnp.mean(x * x, axis=-1, keepdims=True)
    return (x * lax.rsqrt(ms + EPS) * nw) * (1.0 + scale) + shift


def _params(*sem):
    return pltpu.CompilerParams(dimension_semantics=sem, vmem_limit_bytes=VMEM_LIMIT)


def _adaln_kernel(c_ref, w_ref, b_ref, o_ref):
    ca = _silu(c_ref[...]).astype(BF16)
    o_ref[0] = _dot(ca, w_ref[0].astype(BF16)) + b_ref[0]


def _adaln(c, ada_w, ada_b):
    depth, d, n = ada_w.shape
    bsz = c.shape[0]
    tn = n // 4
    return pl.pallas_call(
        _adaln_kernel,
        grid=(depth, n // tn),
        in_specs=[
            pl.BlockSpec((bsz, d), lambda l, j: (0, 0)),
            pl.BlockSpec((1, d, tn), lambda l, j: (l, 0, j)),
            pl.BlockSpec((1, 1, tn), lambda l, j: (l, 0, j)),
        ],
        out_specs=pl.BlockSpec((1, bsz, tn), lambda l, j: (l, 0, j)),
        out_shape=jax.ShapeDtypeStruct((depth, bsz, n), F32),
        compiler_params=_params("arbitrary", "arbitrary"),
        name="adaln",
    )(c, ada_w, ada_b.reshape(depth, 1, n))


HQ_Q, HQ_F, HQ_K, HQ_V = 0, D_BK, 2 * D_BK, 3 * D_BK
GC_G, GC_U, GC_V = 0, D_BV, D_BV + D_C
assert D_BK == D_BV == D_C == D_A


def _inproj_kernel(layer, x_ref, mod_ref, nw_ref, w_ref, convw_ref, lbl_ref, snw_ref,
                   ya_ref, hq_ref, gc_ref, gates_ref, zbuf):
    tm = x_ref.shape[1]
    d = x_ref.shape[2]

    @pl.when(pl.program_id(1) == 0)
    def _():
        zbuf[0:8, :] = jnp.zeros((8, D_A), F32)

    h = _norm_mod(x_ref[0], nw_ref[...], mod_ref[0, 0:1, :], mod_ref[0, 1:2, :]).astype(BF16)

    def group(off):
        return _dot(h, w_ref[:, off:off + D_A])

    z = group(D_A) * group(2 * D_A)
    zbuf[8:8 + tm, :] = z
    y = (convw_ref[0:1, :] * z + convw_ref[1:2, :] * zbuf[7:7 + tm, :]
         + convw_ref[2:3, :] * zbuf[6:6 + tm, :])
    ya_ref[0] = (group(0) * y).astype(BF16)
    zbuf[0:8, :] = zbuf[tm:tm + 8, :]

    lg = lbl_ref[...]
    pe = jnp.exp(lg - jnp.max(lg, axis=0, keepdims=True))
    p = pe / jnp.sum(pe, axis=0, keepdims=True)
    lb = jnp.clip(jnp.sum(p[0:layer + 1], axis=0, keepdims=True) - p[0:1], 0.0, 1.0)
    hq_ref[0, :, HQ_Q:HQ_Q + D_BK] = _silu(group(OFF_BQ)).astype(BF16)
    hq_ref[0, :, HQ_V:HQ_V + D_BV] = group(OFF_BQ + 2 * D_BK).astype(BF16)
    sig = _sigmoid(group(OFF_BQ + D_BK))
    hq_ref[0, :, HQ_F:HQ_F + D_BK] = jnp.log(jnp.maximum(lb, LB_FLOOR) + (1.0 - lb) * sig).astype(BF16)
    hq_ref[0, :, HQ_K:HQ_K + D_BK] = ((1.0 - lb) * (1.0 - sig)).astype(BF16)

    gc_ref[0, :, GC_G:GC_G + D_BV] = _silu(group(OFF_BG)).astype(BF16)
    gc_ref[0, :, GC_U:GC_U + D_C] = group(OFF_BG + D_BV).astype(BF16)
    cv = group(OFF_BG + D_BV + D_C)
    vn = cv * lax.rsqrt(jnp.mean(cv * cv, axis=-1, keepdims=True) + EPS) * snw_ref[...]
    gc_ref[0, :, GC_V:GC_V + D_C] = vn.astype(BF16)

    for g in range(3 * d // D_A):
        gates_ref[0, :, g * D_A:(g + 1) * D_A] = _sigmoid(group(OFF_GATE + g * D_A)).astype(BF16)


def _inproj(layer, x, mod, nw, w_bf16, conv_w, lb_logits, sgu_nw, tm):
    bsz, seq, d = x.shape
    n = w_bf16.shape[1]
    assert n == OFF_GATE + 3 * d and (3 * d) % D_A == 0
    row = lambda b, i: (b, i, 0)
    const2 = lambda b, i: (0, 0)
    out = lambda width: jax.ShapeDtypeStruct((bsz, seq, width), BF16)
    return pl.pallas_call(
        functools.partial(_inproj_kernel, layer),
        grid=(bsz, seq // tm),
        in_specs=[
            pl.BlockSpec((1, tm, d), row),
            pl.BlockSpec((1, 6, d), lambda b, i: (b, 0, 0)),
            pl.BlockSpec((1, d), const2),
            pl.BlockSpec((d, n), const2, pipeline_mode=pl.Buffered(1)),
            pl.BlockSpec(conv_w.shape, const2),
            pl.BlockSpec(lb_logits.shape, const2),
            pl.BlockSpec((1, D_C), const2),
        ],
        out_specs=[
            pl.BlockSpec((1, tm, D_A), row),
            pl.BlockSpec((1, tm, 4 * D_BK), row),
            pl.BlockSpec((1, tm, 3 * D_C), row),
            pl.BlockSpec((1, tm, 3 * d), row),
        ],
        out_shape=[out(D_A), out(4 * D_BK), out(3 * D_C), out(3 * d)],
        scratch_shapes=[pltpu.VMEM((tm + 8, D_A), F32)],
        compiler_params=_params("arbitrary", "arbitrary"),
        name="inproj",
    )(x, mod, nw.reshape(1, d), w_bf16, conv_w, lb_logits, sgu_nw.reshape(1, D_C))


def _level_tables():
    t = np.arange(CHUNK)
    sum_mat = np.zeros((N_LEVELS + 1, CHUNK, CHUNK), np.float32)
    sum_mat[0] = (t[None, :] <= t[:, None])
    up = np.zeros((N_LEVELS + 1, CHUNK, LANES), np.float32)
    low = np.zeros((N_LEVELS + 1, CHUNK, LANES), np.float32)
    pair = np.zeros((N_LEVELS + 1, CHUNK, CHUNK), np.float32)
    pair[0] = np.eye(CHUNK)
    for j in range(1, N_LEVELS + 1):
        blk, half = 1 << j, 1 << (j - 1)
        base = (t // blk) * blk
        m = base + half - 1
        is_up = (t - base) >= half
        u = t[None, :]
        upper_rows = (u > m[:, None]) & (u <= t[:, None])
        lower_rows = (u > t[:, None]) & (u <= m[:, None])
        sum_mat[j] = np.where(is_up[:, None], upper_rows, lower_rows)
        up[j] = is_up[:, None]
        low[j] = ~is_up[:, None]
        pair[j] = (t[:, None] // blk) == (t[None, :] // blk)
    pairc = np.zeros((N_LEVELS + 1, CHUNK // 2, CHUNK), np.float32)
    for j in range(1, N_LEVELS + 1):
        pairc[j] = pair[j][np.nonzero(((t >> (j - 1)) & 1) == 1)[0]]
    return (jnp.asarray(sum_mat.reshape((N_LEVELS + 1) * CHUNK, CHUNK), BF16),
            jnp.asarray(up), jnp.asarray(low), jnp.asarray(pair), jnp.asarray(pairc))


HEAD_GROUPS = ((0, 1), (2, 3))
SUBLANES = 8
MERGE_ROWS = 128


def _row_blocks(m):
    return [m[SUBLANES * r:SUBLANES * (r + 1), :] for r in range(m.shape[0] // SUBLANES)]


def _mixer_kernel(ts,
                  ya_ref, hq_ref, gc_ref, gates_ref, x_ref, mod_ref, hnw_ref, sguw_ref, sgub_ref,
                  summat_ref, up_ref, low_ref, pair_ref, pairc_ref, wa_ref, wb_ref, wc_ref, wo_ref,
                  o_ref,
                  st_ref, yb_ref, yc_ref, wsm_ref, br_ref, mg_ref, kt_ref):
    n_chunks = ts // CHUNK
    d = x_ref.shape[-1]

    @pl.when(pl.program_id(1) == 0)
    def _():
        st_ref[...] = jnp.zeros_like(st_ref)

    tril = (lax.broadcasted_iota(jnp.int32, (SGU_CHUNK, SGU_CHUNK), 0)
            >= lax.broadcasted_iota(jnp.int32, (SGU_CHUNK, SGU_CHUNK), 1))
    for g in range(G_C):
        wsm_ref[g] = jnp.where(tril, sguw_ref[g], 0.0).astype(BF16)

    def chunk_body(c, carry):
        r0 = pl.multiple_of(c * CHUNK, CHUNK)
        rows = pl.ds(r0, CHUNK)

        for g in range(G_C):
            cs = slice(g * GC_CH, (g + 1) * GC_CH)
            vn = gc_ref[0, rows, GC_V + g * GC_CH:GC_V + (g + 1) * GC_CH]
            u = gc_ref[0, rows, GC_U + g * GC_CH:GC_U + (g + 1) * GC_CH].astype(F32)
            yc_ref[rows, cs] = (u * (_dot(wsm_ref[g], vn) + sgub_ref[g])).astype(BF16)

        lv = _dot(summat_ref[...], hq_ref[0, rows, HQ_F:HQ_F + D_BK])

        for heads in HEAD_GROUPS:
            cs = {h: slice(h * DK_B, (h + 1) * DK_B) for h in heads}
            q_bf = {h: hq_ref[0, rows, HQ_Q + h * DK_B:HQ_Q + (h + 1) * DK_B] for h in heads}
            k_bf = {h: hq_ref[0, rows, HQ_K + h * DK_B:HQ_K + (h + 1) * DK_B] for h in heads}
            v_h = {h: hq_ref[0, rows, HQ_V + h * DV_B:HQ_V + (h + 1) * DV_B] for h in heads}
            q_h = {h: q_bf[h].astype(F32) for h in heads}
            k_h = {h: k_bf[h].astype(F32) for h in heads}
            scores = {}
            for h in heads:
                slot = h * (N_LEVELS + 1)
                kt_ref[slot] = k_bf[h].T
                scores[h] = _row_blocks(pair_ref[0] * _dot(q_bf[h], kt_ref[slot]))
            for j in range(1, N_LEVELS + 1):
                half = 1 << (j - 1)
                for h in heads:
                    e = jnp.exp(lv[j * CHUNK:(j + 1) * CHUNK, cs[h]])
                    slot = h * (N_LEVELS + 1) + j
                    if half < SUBLANES:
                        q_j = (q_h[h] * e * up_ref[j]).astype(BF16)
                        k_j = (k_h[h] * e * low_ref[j]).astype(BF16)
                        kt_ref[slot] = k_j.T
                        s_j = _row_blocks(pair_ref[j] * _dot(q_j, kt_ref[slot]))
                        scores[h] = [a + b for a, b in zip(scores[h], s_j)]
                    else:
                        zero = jnp.zeros((half, DK_B), F32)
                        k_parts, q_parts, up_blocks = [], [], []
                        for a in range(0, CHUNK, 2 * half):
                            k_parts += [k_h[h][a:a + half] * e[a:a + half], zero]
                            q_parts.append(q_h[h][a + half:a + 2 * half] * e[a + half:a + 2 * half])
                            up_blocks += list(range((a + half) // SUBLANES, (a + 2 * half) // SUBLANES))
                        kt_ref[slot] = jnp.concatenate(k_parts, axis=0).astype(BF16).T
                        q_c = jnp.concatenate(q_parts, axis=0).astype(BF16)
                        s_c = _row_blocks(pairc_ref[j] * _dot(q_c, kt_ref[slot]))
                        for r, blk in zip(up_blocks, s_c):
                            scores[h][r] = scores[h][r] + blk
            for h in heads:
                b_h = lv[0:CHUNK, cs[h]]
                st = st_ref[h]
                o = _dot_nt((q_h[h] * jnp.exp(b_h)).astype(BF16), st.astype(BF16))
                o = o + _dot(jnp.concatenate(scores[h], axis=0).astype(BF16), v_h[h])
                b_last = b_h[CHUNK - 1:CHUNK, :]
                k_dec = (k_h[h] * jnp.exp(b_last - b_h)).astype(BF16)
                st_ref[h] = st * jnp.exp(b_last) + _dot_tn(v_h[h], k_dec)
                on = o * lax.rsqrt(jnp.mean(o * o, axis=-1, keepdims=True) + EPS) * hnw_ref[:, cs[h]]
                sg = gc_ref[0, rows, GC_G + h * DV_B:GC_G + (h + 1) * DV_B].astype(F32)
                yb_ref[rows, cs[h]] = (on * sg).astype(BF16)
        return carry

    lax.fori_loop(0, n_chunks, chunk_body, 0)

    br_ref[0] = _dot(ya_ref[0], wa_ref[...])
    br_ref[1] = _dot(yb_ref[...], wb_ref[...])
    br_ref[2] = _dot(yc_ref[...], wc_ref[...])

    def merge_body(r, carry):
        rows = pl.ds(pl.multiple_of(r * MERGE_ROWS, MERGE_ROWS), MERGE_ROWS)
        merged = gates_ref[0, rows, 0:d].astype(F32) * br_ref[0, rows, :]
        merged = merged + gates_ref[0, rows, d:2 * d].astype(F32) * br_ref[1, rows, :]
        merged = merged + gates_ref[0, rows, 2 * d:3 * d].astype(F32) * br_ref[2, rows, :]
        mg_ref[rows, :] = merged.astype(BF16)
        return carry

    lax.fori_loop(0, ts // MERGE_ROWS, merge_body, 0)
    o_ref[0] = x_ref[0] + mod_ref[0, 2:3, :] * _dot(mg_ref[...], wo_ref[...])


def _mixer(ya, hq, gc, gates, x, mod, hgrn_nw, sgu_w, sgu_b, tables, wa, wb, wc, wo, ts):
    bsz, seq, d = x.shape
    sum_mat, up, low, pair, pairc = tables
    sgub = jnp.broadcast_to(sgu_b[:, :, None], (G_C, SGU_CHUNK, GC_CH))
    row = lambda b, i: (b, i, 0)
    const2 = lambda b, i: (0, 0)
    const3 = lambda b, i: (0, 0, 0)
    return pl.pallas_call(
        functools.partial(_mixer_kernel, ts),
        grid=(bsz, seq // ts),
        in_specs=[
            pl.BlockSpec((1, ts, ya.shape[-1]), row),
            pl.BlockSpec((1, ts, hq.shape[-1]), row),
            pl.BlockSpec((1, ts, gc.shape[-1]), row),
            pl.BlockSpec((1, ts, gates.shape[-1]), row),
            pl.BlockSpec((1, ts, d), row),
            pl.BlockSpec((1, 6, d), lambda b, i: (b, 0, 0)),
            pl.BlockSpec((1, D_BV), const2),
            pl.BlockSpec(sgu_w.shape, const3),
            pl.BlockSpec(sgub.shape, const3),
            pl.BlockSpec(sum_mat.shape, const2),
            pl.BlockSpec(up.shape, const3),
            pl.BlockSpec(low.shape, const3),
            pl.BlockSpec(pair.shape, const3),
            pl.BlockSpec(pairc.shape, const3),
            pl.BlockSpec(wa.shape, const2),
            pl.BlockSpec(wb.shape, const2),
            pl.BlockSpec(wc.shape, const2),
            pl.BlockSpec(wo.shape, const2),
        ],
        out_specs=pl.BlockSpec((1, ts, d), row),
        out_shape=jax.ShapeDtypeStruct((bsz, seq, d), F32),
        scratch_shapes=[
            pltpu.VMEM((H_B, DV_B, DK_B), F32),
            pltpu.VMEM((ts, D_BV), BF16),
            pltpu.VMEM((ts, D_C), BF16),
            pltpu.VMEM((G_C, SGU_CHUNK, SGU_CHUNK), BF16),
            pltpu.VMEM((3, ts, d), F32),
            pltpu.VMEM((ts, d), BF16),
            pltpu.VMEM((H_B * (N_LEVELS + 1), DK_B, CHUNK), BF16),
        ],
        compiler_params=_params("arbitrary", "arbitrary"),
        name="mixer",
    )(ya, hq, gc, gates, x, mod, hgrn_nw.reshape(1, D_BV), sgu_w, sgub,
      sum_mat, up, low, pair, pairc, wa, wb, wc, wo)


def _swiglu_tile(h, w1_at, w3_at, w2, g_ref):
    d_ff = g_ref.shape[1]
    for c0 in range(0, d_ff, MXU_N):
        a = _dot(h, w1_at(c0))
        g_ref[:, c0:c0 + MXU_N] = (_silu(a) * _dot(h, w3_at(c0))).astype(BF16)
    return _dot(g_ref[...], w2)


def _final_norm(v, fw):
    return v * lax.rsqrt(jnp.mean(v * v, axis=-1, keepdims=True) + EPS) * fw


def _ffn_kernel(final, x_ref, mod_ref, nw_ref, fw_ref, w1_ref, w3_ref, w2_ref, o_ref, g_ref):
    x = x_ref[0]
    h = _norm_mod(x, nw_ref[...], mod_ref[0, 3:4, :], mod_ref[0, 4:5, :]).astype(BF16)
    y = _swiglu_tile(h, lambda c0: w1_ref[:, c0:c0 + MXU_N], lambda c0: w3_ref[:, c0:c0 + MXU_N],
                     w2_ref[...], g_ref)
    out = x + mod_ref[0, 5:6, :] * y
    o_ref[0] = _final_norm(out, fw_ref[...]) if final else out


def _ffn(x, mod, nw, fw, w1, w3, w2, tm, final):
    bsz, seq, d = x.shape
    d_ff = w1.shape[1]
    const2 = lambda b, i: (0, 0)
    resident = dict(pipeline_mode=pl.Buffered(1))
    return pl.pallas_call(
        functools.partial(_ffn_kernel, final),
        grid=(bsz, seq // tm),
        in_specs=[
            pl.BlockSpec((1, tm, d), lambda b, i: (b, i, 0)),
            pl.BlockSpec((1, 6, d), lambda b, i: (b, 0, 0)),
            pl.BlockSpec((1, d), const2),
            pl.BlockSpec((1, d), const2),
            pl.BlockSpec((d, d_ff), const2, **resident),
            pl.BlockSpec((d, d_ff), const2, **resident),
            pl.BlockSpec((d_ff, d), const2, **resident),
        ],
        out_specs=pl.BlockSpec((1, tm, d), lambda b, i: (b, i, 0)),
        out_shape=jax.ShapeDtypeStruct((bsz, seq, d), F32),
        scratch_shapes=[pltpu.VMEM((tm, d_ff), BF16)],
        compiler_params=_params("arbitrary", "arbitrary"),
        name="ffn",
    )(x, mod, nw.reshape(1, d), fw.reshape(1, d), w1, w3, w2)


META_E, META_P, META_R = 0, 2, 4


def _router_kernel(x_ref, mod_ref, nw_ref, rw_ref, rb_ref, tri_ref, h_ref, meta_ref, cnt_ref, carry):
    first = jnp.logical_and(pl.program_id(0) == 0, pl.program_id(1) == 0)

    @pl.when(first)
    def _():
        carry[...] = jnp.zeros_like(carry)

    h = _norm_mod(x_ref[0], nw_ref[...], mod_ref[0, 3:4, :], mod_ref[0, 4:5, :])
    h_ref[0] = h
    h_hi = h.astype(BF16)
    h_lo = (h - h_hi.astype(F32)).astype(BF16)
    part = _dot(h_hi, rw_ref[...])
    logits = (part[:, 0:LANES] + part[:, LANES:2 * LANES]) + _dot(h_lo, rw_ref[:, 0:LANES]) + rb_ref[...]
    lane = lax.broadcasted_iota(jnp.int32, logits.shape, 1)
    m1 = jnp.max(logits, axis=-1, keepdims=True)
    i1 = jnp.min(jnp.where(logits == m1, lane, LANES), axis=-1, keepdims=True)
    rest = jnp.where(lane == i1, -jnp.inf, logits)
    m2 = jnp.max(rest, axis=-1, keepdims=True)
    i2 = jnp.min(jnp.where(rest == m2, lane, LANES), axis=-1, keepdims=True)
    e2 = jnp.exp(m2 - m1)
    p1 = 1.0 / (1.0 + e2)
    p2 = e2 / (1.0 + e2)
    sel1 = lane == i1
    sel2 = lane == i2
    onehot = jnp.where(jnp.logical_or(sel1, sel2), 1.0, 0.0)
    before = _dot(tri_ref[...], onehot.astype(BF16)) + carry[...]
    r1 = jnp.sum(jnp.where(sel1, before, 0.0), axis=-1, keepdims=True)
    r2 = jnp.sum(jnp.where(sel2, before, 0.0), axis=-1, keepdims=True)
    carry[...] = carry[...] + jnp.sum(onehot, axis=0, keepdims=True)
    cnt_ref[...] = carry[...]
    rec = jnp.where(lane == META_E, i1.astype(F32), 0.0)
    rec = jnp.where(lane == META_E + 1, i2.astype(F32), rec)
    rec = jnp.where(lane == META_P, p1, rec)
    rec = jnp.where(lane == META_P + 1, p2, rec)
    rec = jnp.where(lane == META_R, r1, rec)
    rec = jnp.where(lane == META_R + 1, r2, rec)
    meta_ref[0] = rec


def _router(x, mod, nw, router_w, router_b, tm):
    bsz, seq, d = x.shape
    ne = router_w.shape[1]
    rw = jnp.zeros((d, LANES), F32).at[:, :ne].set(router_w)
    rw_hi = rw.astype(BF16)
    rw = jnp.concatenate([rw_hi, (rw - rw_hi.astype(F32)).astype(BF16)], axis=1)
    rb = jnp.full((1, LANES), NEG_BIG, F32).at[0, :ne].set(router_b)
    tri = jnp.asarray(np.tril(np.ones((tm, tm), np.float32), -1), BF16)
    const2 = lambda b, i: (0, 0)
    return pl.pallas_call(
        _router_kernel,
        grid=(bsz, seq // tm),
        in_specs=[
            pl.BlockSpec((1, tm, d), lambda b, i: (b, i, 0)),
            pl.BlockSpec((1, 6, d), lambda b, i: (b, 0, 0)),
            pl.BlockSpec((1, d), const2),
            pl.BlockSpec((d, 2 * LANES), const2),
            pl.BlockSpec((1, LANES), const2),
            pl.BlockSpec((tm, tm), const2),
        ],
        out_specs=[
            pl.BlockSpec((1, tm, d), lambda b, i: (b, i, 0)),
            pl.BlockSpec((1, tm, LANES), lambda b, i: (b, i, 0)),
            pl.BlockSpec((1, LANES), const2),
        ],
        out_shape=[
            jax.ShapeDtypeStruct((bsz, seq, d), F32),
            jax.ShapeDtypeStruct((bsz, seq, LANES), F32),
            jax.ShapeDtypeStruct((1, LANES), F32),
        ],
        scratch_shapes=[pltpu.VMEM((1, LANES), F32)],
        compiler_params=_params("arbitrary", "arbitrary"),
        name="router",
    )(x, mod, nw.reshape(1, d), rw, rb, tri)


def _row_copy(src, src_row, dst, dst_row, sem):
    return pltpu.make_async_copy(src.at[pl.ds(src_row, 1), :], dst.at[pl.ds(dst_row, 1), :], sem)


def _dispatch_kernel(tt, pos_ref, h_ref, xs_hbm, sem):
    def issue(t, carry):
        _row_copy(h_ref, t, xs_hbm, pos_ref[0, 0, 2 * t], sem).start()
        _row_copy(h_ref, t, xs_hbm, pos_ref[0, 0, 2 * t + 1], sem).start()
        return carry

    lax.fori_loop(0, tt, issue, 0, unroll=8)
    for _ in range(2):
        pltpu.make_async_copy(h_ref, xs_hbm.at[pl.ds(0, tt), :], sem).wait()


def _dispatch(h2d, pos, tt):
    n_tok, d = h2d.shape
    return pl.pallas_call(
        functools.partial(_dispatch_kernel, tt),
        grid=(n_tok // tt,),
        in_specs=[
            pl.BlockSpec((1, 1, 2 * tt), lambda i: (i, 0, 0), memory_space=pltpu.SMEM),
            pl.BlockSpec((tt, d), lambda i: (i, 0)),
        ],
        out_specs=pl.BlockSpec(memory_space=pl.ANY),
        out_shape=jax.ShapeDtypeStruct((2 * n_tok, d), F32),
        scratch_shapes=[pltpu.SemaphoreType.DMA(())],
        compiler_params=_params("arbitrary"),
        name="dispatch",
    )(pos.reshape(n_tok // tt, 1, 2 * tt), h2d)


def _expert_kernel(tm, tile_ref, exp_ref, lo_ref, hi_ref, first_ref, valid_ref,
                   xs_ref, w1_ref, w3_ref, w2_ref, y_ref, g_ref):
    s = pl.program_id(0)

    @pl.when(valid_ref[s] == 1)
    def _():
        h = xs_ref[...].astype(BF16)
        res = _swiglu_tile(h, lambda c0: w1_ref[0, :, c0:c0 + MXU_N],
                           lambda c0: w3_ref[0, :, c0:c0 + MXU_N], w2_ref[0], g_ref)
        row = tile_ref[s] * tm + lax.broadcasted_iota(jnp.int32, (tm, 1), 0)
        mine = jnp.logical_and(row >= lo_ref[s], row < hi_ref[s])

        @pl.when(first_ref[s] == 1)
        def _():
            y_ref[...] = jnp.where(mine, res, 0.0)

        @pl.when(first_ref[s] == 0)
        def _():
            y_ref[...] = jnp.where(mine, res, y_ref[...])


def _experts(xs, steps, w1, w3, w2, tm):
    n_rows, d = xs.shape
    d_ff = w1.shape[2]
    n_steps = steps[0].shape[0]
    grid_spec = pltpu.PrefetchScalarGridSpec(
        num_scalar_prefetch=6,
        grid=(n_steps,),
        in_specs=[
            pl.BlockSpec((tm, d), lambda s, tile, exp, lo, hi, first, valid: (tile[s], 0)),
            pl.BlockSpec((1, d, d_ff), lambda s, tile, exp, lo, hi, first, valid: (exp[s], 0, 0)),
            pl.BlockSpec((1, d, d_ff), lambda s, tile, exp, lo, hi, first, valid: (exp[s], 0, 0)),
            pl.BlockSpec((1, d_ff, d), lambda s, tile, exp, lo, hi, first, valid: (exp[s], 0, 0)),
        ],
        out_specs=pl.BlockSpec((tm, d), lambda s, tile, exp, lo, hi, first, valid: (tile[s], 0)),
        scratch_shapes=[pltpu.VMEM((tm, d_ff), BF16)],
    )
    return pl.pallas_call(
        functools.partial(_expert_kernel, tm),
        grid_spec=grid_spec,
        out_shape=jax.ShapeDtypeStruct((n_rows, d), F32),
        compiler_params=_params("arbitrary"),
        name="experts",
    )(*steps, xs, w1, w3, w2)


def _combine_kernel(tt, final, pos_ref, y_hbm, x_ref, mod_ref, meta_ref, fw_ref, o_ref, buf, sem):
    def issue(t, carry):
        _row_copy(y_hbm, pos_ref[0, 0, 2 * t], buf.at[0], t, sem).start()
        _row_copy(y_hbm, pos_ref[0, 0, 2 * t + 1], buf.at[1], t, sem).start()
        return carry

    lax.fori_loop(0, tt, issue, 0, unroll=8)
    for k in range(2):
        pltpu.make_async_copy(y_hbm.at[pl.ds(0, tt), :], buf.at[k], sem).wait()
    p1 = meta_ref[0, :, META_P:META_P + 1]
    p2 = meta_ref[0, :, META_P + 1:META_P + 2]
    out = x_ref[0] + mod_ref[0, 5:6, :] * (p1 * buf[0] + p2 * buf[1])
    o_ref[0] = _final_norm(out, fw_ref[...]) if final else out


def _combine(y, pos, x, mod, meta, fw, tt, final):
    bsz, seq, d = x.shape
    per_b = seq // tt
    return pl.pallas_call(
        functools.partial(_combine_kernel, tt, final),
        grid=(bsz, per_b),
        in_specs=[
            pl.BlockSpec((1, 1, 2 * tt), lambda b, i: (b * per_b + i, 0, 0), memory_space=pltpu.SMEM),
            pl.BlockSpec(memory_space=pl.ANY),
            pl.BlockSpec((1, tt, d), lambda b, i: (b, i, 0)),
            pl.BlockSpec((1, 6, d), lambda b, i: (b, 0, 0)),
            pl.BlockSpec((1, tt, LANES), lambda b, i: (b, i, 0)),
            pl.BlockSpec((1, d), lambda b, i: (0, 0)),
        ],
        out_specs=pl.BlockSpec((1, tt, d), lambda b, i: (b, i, 0)),
        out_shape=jax.ShapeDtypeStruct((bsz, seq, d), F32),
        scratch_shapes=[pltpu.VMEM((2, tt, d), F32), pltpu.SemaphoreType.DMA(())],
        compiler_params=_params("arbitrary", "arbitrary"),
        name="combine",
    )(pos.reshape(bsz * per_b, 1, 2 * tt), y, x, mod, meta, fw.reshape(1, d))


def _expert_steps(counts, n_rows, tm):
    n_tiles = n_rows // tm
    n_steps = n_tiles + N_EXPERTS - 1
    ends = jnp.cumsum(counts)
    starts = ends - counts
    first_tile = starts // tm
    last_tile = jnp.maximum(ends - 1, 0) // tm
    tiles_e = jnp.where(counts > 0, last_tile - first_tile + 1, 0)
    step_end = jnp.cumsum(tiles_e)
    step_start = step_end - tiles_e
    total = step_end[-1]
    s = jnp.arange(n_steps, dtype=jnp.int32)
    sc = jnp.minimum(s, jnp.maximum(total - 1, 0))
    e_s = jnp.minimum(jnp.sum((sc[:, None] >= step_end[None, :]).astype(jnp.int32), axis=1), N_EXPERTS - 1)
    tile_s = jnp.clip(first_tile[e_s] + (sc - step_start[e_s]), 0, n_tiles - 1)
    valid = (s < total).astype(jnp.int32)
    prev_tile = jnp.concatenate([jnp.full((1,), -1, jnp.int32), tile_s[:-1]])
    first = jnp.logical_and(tile_s != prev_tile, valid == 1).astype(jnp.int32)
    as_i32 = lambda v: v.astype(jnp.int32)
    return (as_i32(tile_s), as_i32(e_s), as_i32(starts[e_s]), as_i32(ends[e_s]), first, valid)


def _moe(x, mod, nw, fw, router_w, router_b, w1, w3, w2, tm_route, tt, tm_exp, final):
    bsz, seq, d = x.shape
    n_tok = bsz * seq
    h, meta, cnt = _router(x, mod, nw, router_w, router_b, tm_route)
    counts = cnt[0, :N_EXPERTS].astype(jnp.int32)
    starts = jnp.cumsum(counts) - counts
    meta2 = meta.reshape(n_tok, LANES)
    experts = meta2[:, META_E:META_E + 2].astype(jnp.int32)
    ranks = meta2[:, META_R:META_R + 2].astype(jnp.int32)
    onehot = experts[:, :, None] == jnp.arange(N_EXPERTS, dtype=jnp.int32)[None, None, :]
    pos = jnp.sum(jnp.where(onehot, starts[None, None, :], 0), axis=-1) + ranks
    xs = _dispatch(h.reshape(n_tok, d), pos, tt)
    y = _experts(xs, _expert_steps(counts, 2 * n_tok, tm_exp), w1, w3, w2, tm_exp)
    return _combine(y, pos, x, mod, meta, fw, tt, final)


def _pick(n, pref):
    t = min(n, pref)
    assert n % t == 0, (n, pref)
    return t


def kernel(x, c, ada_w, ada_b, norm_mix_w, norm_ffn_w, w_in, conv_w, hgrn_lb_logits, hgrn_norm_w,
           sgu_norm_w, sgu_w, sgu_b, w_br_a, w_br_b, w_br_c, w_o, ffn_w1, ffn_w3, ffn_w2,
           moe_router_w, moe_router_b, moe_w1, moe_w3, moe_w2, final_norm_w):
    depth = ada_w.shape[0]
    bsz, seq, d = x.shape
    assert w_in.shape[-1] == OFF_GATE + 3 * d and seq % CHUNK == 0
    assert ffn_w1.shape[-1] % MXU_N == 0 and moe_w1.shape[-1] % MXU_N == 0

    tm_in = _pick(seq, 512)
    ts = _pick(seq, 512)
    tm_ffn = _pick(seq, 512)
    tm_route = _pick(seq, 512)
    tt = _pick(seq, 512)
    tm_exp = _pick(2 * bsz * seq, 512)

    mod_all = _adaln(c, ada_w, ada_b).reshape(depth, bsz, 6, d)
    tables = _level_tables()
    bf = lambda w: w.astype(BF16)

    for l in range(depth):
        mod = mod_all[l]
        ya, hq, gc, gates = _inproj(l, x, mod, norm_mix_w[l], bf(w_in[l]), conv_w[l], hgrn_lb_logits,
                                    sgu_norm_w[l], tm_in)
        x = _mixer(ya, hq, gc, gates, x, mod, hgrn_norm_w[l], sgu_w[l], sgu_b[l], tables,
                   bf(w_br_a[l]), bf(w_br_b[l]), bf(w_br_c[l]), bf(w_o[l]), ts)
        final = l == depth - 1
        j = l // 2
        if l % 2 == 0:
            x = _ffn(x, mod, norm_ffn_w[l], final_norm_w, bf(ffn_w1[j]), bf(ffn_w3[j]), bf(ffn_w2[j]),
                     tm_ffn, final)
        else:
            x = _moe(x, mod, norm_ffn_w[l], final_norm_w, moe_router_w[j], moe_router_b[j],
                     bf(moe_w1[j]), bf(moe_w3[j]), bf(moe_w2[j]), tm_route, tt, tm_exp, final)
    return x
```

```python
import functools

import numpy as np
import jax
import jax.numpy as jnp
from jax import lax
from jax.experimental import pallas as pl
from jax.experimental.pallas import tpu as pltpu

F32 = jnp.float32
BF16 = jnp.bfloat16

EPS = 1e-6
LB_FLOOR = 1e-30

D_A = 512
CONV_WIDTH = 3
H_B = 4
DK_B = 128
DV_B = 128
D_BK = H_B * DK_B
D_BV = H_B * DV_B
G_C = 4
GC_CH = 128
D_C = G_C * GC_CH
SGU_CHUNK = 128
N_EXPERTS = 8

OFF_BQ = 3 * D_A
OFF_BG = OFF_BQ + 2 * D_BK + D_BV
OFF_GATE = OFF_BG + D_BV + 2 * D_C

CHUNK = 128
N_LEVELS = 7
assert 1 << N_LEVELS == CHUNK and CHUNK == SGU_CHUNK

LANES = 128
MXU_N = 256
VMEM_LIMIT = 60 * 1024 * 1024
NEG_BIG = -1e30


def _sigmoid(v):
    return 0.5 + 0.5 * jnp.tanh(0.5 * v)


def _silu(v):
    return v * _sigmoid(v)


def _dot(a, b):
    return jnp.dot(a, b, preferred_element_type=F32)


def _dot_nt(a, b):
    return lax.dot_general(a, b, (((1,), (1,)), ((), ())), preferred_element_type=F32)


def _dot_tn(a, b):
    return lax.dot_general(a, b, (((0,), (0,)), ((), ())), preferred_element_type=F32)


def _norm_mod(x, nw, shift, scale):
    ms = jnp.mean(x * x, axis=-1, keepdims=True)
    return (x * lax.rsqrt(ms + EPS) * nw) * (1.0 + scale) + shift


def _params(*sem):
    return pltpu.CompilerParams(dimension_semantics=sem, vmem_limit_bytes=VMEM_LIMIT)


def _adaln_kernel(c_ref, w_ref, b_ref, o_ref):
    ca = _silu(c_ref[...]).astype(BF16)
    o_ref[0] = _dot(ca, w_ref[0].astype(BF16)) + b_ref[0]


def _adaln(c, ada_w, ada_b):
    depth, d, n = ada_w.shape
    bsz = c.shape[0]
    tn = n // 4
    return pl.pallas_call(
        _adaln_kernel,
        grid=(depth, n // tn),
        in_specs=[
            pl.BlockSpec((bsz, d), lambda l, j: (0, 0)),
            pl.BlockSpec((1, d, tn), lambda l, j: (l, 0, j)),
            pl.BlockSpec((1, 1, tn), lambda l, j: (l, 0, j)),
        ],
        out_specs=pl.BlockSpec((1, bsz, tn), lambda l, j: (l, 0, j)),
        out_shape=jax.ShapeDtypeStruct((depth, bsz, n), F32),
        compiler_params=_params("arbitrary", "arbitrary"),
        name="adaln",
    )(c, ada_w, ada_b.reshape(depth, 1, n))


HQ_Q, HQ_F, HQ_K, HQ_V = 0, D_BK, 2 * D_BK, 3 * D_BK
GC_G, GC_U, GC_V = 0, D_BV, D_BV + D_C
assert D_BK == D_BV == D_C == D_A


def _inproj_kernel(layer, x_ref, mod_ref, nw_ref, w_ref, convw_ref, lbl_ref, snw_ref,
                   ya_ref, hq_ref, gc_ref, gates_ref, zbuf):
    tm = x_ref.shape[1]
    d = x_ref.shape[2]

    @pl.when(pl.program_id(1) == 0)
    def _():
        zbuf[0:8, :] = jnp.zeros((8, D_A), F32)

    h = _norm_mod(x_ref[0], nw_ref[...], mod_ref[0, 0:1, :], mod_ref[0, 1:2, :]).astype(BF16)

    def group(off):
        return _dot(h, w_ref[:, off:off + D_A])

    z = group(D_A) * group(2 * D_A)
    zbuf[8:8 + tm, :] = z
    y = (convw_ref[0:1, :] * z + convw_ref[1:2, :] * zbuf[7:7 + tm, :]
         + convw_ref[2:3, :] * zbuf[6:6 + tm, :])
    ya_ref[0] = (group(0) * y).astype(BF16)
    zbuf[0:8, :] = zbuf[tm:tm + 8, :]

    lg = lbl_ref[...]
    pe = jnp.exp(lg - jnp.max(lg, axis=0, keepdims=True))
    p = pe / jnp.sum(pe, axis=0, keepdims=True)
    lb = jnp.clip(jnp.sum(p[0:layer + 1], axis=0, keepdims=True) - p[0:1], 0.0, 1.0)
    hq_ref[0, :, HQ_Q:HQ_Q + D_BK] = _silu(group(OFF_BQ)).astype(BF16)
    hq_ref[0, :, HQ_V:HQ_V + D_BV] = group(OFF_BQ + 2 * D_BK).astype(BF16)
    sig = _sigmoid(group(OFF_BQ + D_BK))
    hq_ref[0, :, HQ_F:HQ_F + D_BK] = jnp.log(jnp.maximum(lb, LB_FLOOR) + (1.0 - lb) * sig).astype(BF16)
    hq_ref[0, :, HQ_K:HQ_K + D_BK] = ((1.0 - lb) * (1.0 - sig)).astype(BF16)

    gc_ref[0, :, GC_G:GC_G + D_BV] = _silu(group(OFF_BG)).astype(BF16)
    gc_ref[0, :, GC_U:GC_U + D_C] = group(OFF_BG + D_BV).astype(BF16)
    cv = group(OFF_BG + D_BV + D_C)
    vn = cv * lax.rsqrt(jnp.mean(cv * cv, axis=-1, keepdims=True) + EPS) * snw_ref[...]
    gc_ref[0, :, GC_V:GC_V + D_C] = vn.astype(BF16)

    for g in range(3 * d // D_A):
        gates_ref[0, :, g * D_A:(g + 1) * D_A] = _sigmoid(group(OFF_GATE + g * D_A)).astype(BF16)


def _inproj(layer, x, mod, nw, w_bf16, conv_w, lb_logits, sgu_nw, tm):
    bsz, seq, d = x.shape
    n = w_bf16.shape[1]
    assert n == OFF_GATE + 3 * d and (3 * d) % D_A == 0
    row = lambda b, i: (b, i, 0)
    const2 = lambda b, i: (0, 0)
    out = lambda width: jax.ShapeDtypeStruct((bsz, seq, width), BF16)
    return pl.pallas_call(
        functools.partial(_inproj_kernel, layer),
        grid=(bsz, seq // tm),
        in_specs=[
            pl.BlockSpec((1, tm, d), row),
            pl.BlockSpec((1, 6, d), lambda b, i: (b, 0, 0)),
            pl.BlockSpec((1, d), const2),
            pl.BlockSpec((d, n), const2, pipeline_mode=pl.Buffered(1)),
            pl.BlockSpec(conv_w.shape, const2),
            pl.BlockSpec(lb_logits.shape, const2),
            pl.BlockSpec((1, D_C), const2),
        ],
        out_specs=[
            pl.BlockSpec((1, tm, D_A), row),
            pl.BlockSpec((1, tm, 4 * D_BK), row),
            pl.BlockSpec((1, tm, 3 * D_C), row),
            pl.BlockSpec((1, tm, 3 * d), row),
        ],
        out_shape=[out(D_A), out(4 * D_BK), out(3 * D_C), out(3 * d)],
        scratch_shapes=[pltpu.VMEM((tm + 8, D_A), F32)],
        compiler_params=_params("arbitrary", "arbitrary"),
        name="inproj",
    )(x, mod, nw.reshape(1, d), w_bf16, conv_w, lb_logits, sgu_nw.reshape(1, D_C))


def _level_tables():
    t = np.arange(CHUNK)
    sum_mat = np.zeros((N_LEVELS + 1, CHUNK, CHUNK), np.float32)
    sum_mat[0] = (t[None, :] <= t[:, None])
    up = np.zeros((N_LEVELS + 1, CHUNK, LANES), np.float32)
    low = np.zeros((N_LEVELS + 1, CHUNK, LANES), np.float32)
    pair = np.zeros((N_LEVELS + 1, CHUNK, CHUNK), np.float32)
    pair[0] = np.eye(CHUNK)
    for j in range(1, N_LEVELS + 1):
        blk, half = 1 << j, 1 << (j - 1)
        base = (t // blk) * blk
        m = base + half - 1
        is_up = (t - base) >= half
        u = t[None, :]
        upper_rows = (u > m[:, None]) & (u <= t[:, None])
        lower_rows = (u > t[:, None]) & (u <= m[:, None])
        sum_mat[j] = np.where(is_up[:, None], upper_rows, lower_rows)
        up[j] = is_up[:, None]
        low[j] = ~is_up[:, None]
        pair[j] = (t[:, None] // blk) == (t[None, :] // blk)
    pairc = np.zeros((N_LEVELS + 1, CHUNK // 2, CHUNK), np.float32)
    for j in range(1, N_LEVELS + 1):
        pairc[j] = pair[j][np.nonzero(((t >> (j - 1)) & 1) == 1)[0]]
    return (jnp.asarray(sum_mat.reshape((N_LEVELS + 1) * CHUNK, CHUNK), BF16),
            jnp.asarray(up), jnp.asarray(low), jnp.asarray(pair), jnp.asarray(pairc))


HEAD_GROUPS = ((0, 1), (2, 3))
SUBLANES = 8
MERGE_ROWS = 128


def _row_blocks(m):
    return [m[SUBLANES * r:SUBLANES * (r + 1), :] for r in range(m.shape[0] // SUBLANES)]


def _mixer_kernel(ts,
                  ya_ref, hq_ref, gc_ref, gates_ref, x_ref, mod_ref, hnw_ref, sguw_ref, sgub_ref,
                  summat_ref, up_ref, low_ref, pair_ref, pairc_ref, wa_ref, wb_ref, wc_ref, wo_ref,
                  o_ref,
                  st_ref, yb_ref, yc_ref, wsm_ref, br_ref, mg_ref, kt_ref):
    n_chunks = ts // CHUNK
    d = x_ref.shape[-1]

    @pl.when(pl.program_id(1) == 0)
    def _():
        st_ref[...] = jnp.zeros_like(st_ref)

    tril = (lax.broadcasted_iota(jnp.int32, (SGU_CHUNK, SGU_CHUNK), 0)
            >= lax.broadcasted_iota(jnp.int32, (SGU_CHUNK, SGU_CHUNK), 1))
    for g in range(G_C):
        wsm_ref[g] = jnp.where(tril, sguw_ref[g], 0.0).astype(BF16)

    def chunk_body(c, carry):
        r0 = pl.multiple_of(c * CHUNK, CHUNK)
        rows = pl.ds(r0, CHUNK)

        for g in range(G_C):
            cs = slice(g * GC_CH, (g + 1) * GC_CH)
            vn = gc_ref[0, rows, GC_V + g * GC_CH:GC_V + (g + 1) * GC_CH]
            u = gc_ref[0, rows, GC_U + g * GC_CH:GC_U + (g + 1) * GC_CH].astype(F32)
            yc_ref[rows, cs] = (u * (_dot(wsm_ref[g], vn) + sgub_ref[g])).astype(BF16)

        lv = _dot(summat_ref[...], hq_ref[0, rows, HQ_F:HQ_F + D_BK])

        for heads in HEAD_GROUPS:
            cs = {h: slice(h * DK_B, (h + 1) * DK_B) for h in heads}
            q_bf = {h: hq_ref[0, rows, HQ_Q + h * DK_B:HQ_Q + (h + 1) * DK_B] for h in heads}
            k_bf = {h: hq_ref[0, rows, HQ_K + h * DK_B:HQ_K + (h + 1) * DK_B] for h in heads}
            v_h = {h: hq_ref[0, rows, HQ_V + h * DV_B:HQ_V + (h + 1) * DV_B] for h in heads}
            q_h = {h: q_bf[h].astype(F32) for h in heads}
            k_h = {h: k_bf[h].astype(F32) for h in heads}
            scores = {}
            for h in heads:
                slot = h * (N_LEVELS + 1)
                kt_ref[slot] = k_bf[h].T
                scores[h] = _row_blocks(pair_ref[0] * _dot(q_bf[h], kt_ref[slot]))
            for j in range(1, N_LEVELS + 1):
                half = 1 << (j - 1)
                for h in heads:
                    e = jnp.exp(lv[j * CHUNK:(j + 1) * CHUNK, cs[h]])
                    slot = h * (N_LEVELS + 1) + j
                    if half < SUBLANES:
                        q_j = (q_h[h] * e * up_ref[j]).astype(BF16)
                        k_j = (k_h[h] * e * low_ref[j]).astype(BF16)
                        kt_ref[slot] = k_j.T
                        s_j = _row_blocks(pair_ref[j] * _dot(q_j, kt_ref[slot]))
                        scores[h] = [a + b for a, b in zip(scores[h], s_j)]
                    else:
                        zero = jnp.zeros((half, DK_B), F32)
                        k_parts, q_parts, up_blocks = [], [], []
                        for a in range(0, CHUNK, 2 * half):
                            k_parts += [k_h[h][a:a + half] * e[a:a + half], zero]
                            q_parts.append(q_h[h][a + half:a + 2 * half] * e[a + half:a + 2 * half])
                            up_blocks += list(range((a + half) // SUBLANES, (a + 2 * half) // SUBLANES))
                        kt_ref[slot] = jnp.concatenate(k_parts, axis=0).astype(BF16).T
                        q_c = jnp.concatenate(q_parts, axis=0).astype(BF16)
                        s_c = _row_blocks(pairc_ref[j] * _dot(q_c, kt_ref[slot]))
                        for r, blk in zip(up_blocks, s_c):
                            scores[h][r] = scores[h][r] + blk
            for h in heads:
                b_h = lv[0:CHUNK, cs[h]]
                st = st_ref[h]
                o = _dot_nt((q_h[h] * jnp.exp(b_h)).astype(BF16), st.astype(BF16))
                o = o + _dot(jnp.concatenate(scores[h], axis=0).astype(BF16), v_h[h])
                b_last = b_h[CHUNK - 1:CHUNK, :]
                k_dec = (k_h[h] * jnp.exp(b_last - b_h)).astype(BF16)
                st_ref[h] = st * jnp.exp(b_last) + _dot_tn(v_h[h], k_dec)
                on = o * lax.rsqrt(jnp.mean(o * o, axis=-1, keepdims=True) + EPS) * hnw_ref[:, cs[h]]
                sg = gc_ref[0, rows, GC_G + h * DV_B:GC_G + (h + 1) * DV_B].astype(F32)
                yb_ref[rows, cs[h]] = (on * sg).astype(BF16)
        return carry

    lax.fori_loop(0, n_chunks, chunk_body, 0)

    br_ref[0] = _dot(ya_ref[0], wa_ref[...])
    br_ref[1] = _dot(yb_ref[...], wb_ref[...])
    br_ref[2] = _dot(yc_ref[...], wc_ref[...])

    def merge_body(r, carry):
        rows = pl.ds(pl.multiple_of(r * MERGE_ROWS, MERGE_ROWS), MERGE_ROWS)
        merged = gates_ref[0, rows, 0:d].astype(F32) * br_ref[0, rows, :]
        merged = merged + gates_ref[0, rows, d:2 * d].astype(F32) * br_ref[1, rows, :]
        merged = merged + gates_ref[0, rows, 2 * d:3 * d].astype(F32) * br_ref[2, rows, :]
        mg_ref[rows, :] = merged.astype(BF16)
        return carry

    lax.fori_loop(0, ts // MERGE_ROWS, merge_body, 0)
    o_ref[0] = x_ref[0] + mod_ref[0, 2:3, :] * _dot(mg_ref[...], wo_ref[...])


def _mixer(ya, hq, gc, gates, x, mod, hgrn_nw, sgu_w, sgu_b, tables, wa, wb, wc, wo, ts):
    bsz, seq, d = x.shape
    sum_mat, up, low, pair, pairc = tables
    sgub = jnp.broadcast_to(sgu_b[:, :, None], (G_C, SGU_CHUNK, GC_CH))
    row = lambda b, i: (b, i, 0)
    const2 = lambda b, i: (0, 0)
    const3 = lambda b, i: (0, 0, 0)
    return pl.pallas_call(
        functools.partial(_mixer_kernel, ts),
        grid=(bsz, seq // ts),
        in_specs=[
            pl.BlockSpec((1, ts, ya.shape[-1]), row),
            pl.BlockSpec((1, ts, hq.shape[-1]), row),
            pl.BlockSpec((1, ts, gc.shape[-1]), row),
            pl.BlockSpec((1, ts, gates.shape[-1]), row),
            pl.BlockSpec((1, ts, d), row),
            pl.BlockSpec((1, 6, d), lambda b, i: (b, 0, 0)),
            pl.BlockSpec((1, D_BV), const2),
            pl.BlockSpec(sgu_w.shape, const3),
            pl.BlockSpec(sgub.shape, const3),
            pl.BlockSpec(sum_mat.shape, const2),
            pl.BlockSpec(up.shape, const3),
            pl.BlockSpec(low.shape, const3),
            pl.BlockSpec(pair.shape, const3),
            pl.BlockSpec(pairc.shape, const3),
            pl.BlockSpec(wa.shape, const2),
            pl.BlockSpec(wb.shape, const2),
            pl.BlockSpec(wc.shape, const2),
            pl.BlockSpec(wo.shape, const2),
        ],
        out_specs=pl.BlockSpec((1, ts, d), row),
        out_shape=jax.ShapeDtypeStruct((bsz, seq, d), F32),
        scratch_shapes=[
            pltpu.VMEM((H_B, DV_B, DK_B), F32),
            pltpu.VMEM((ts, D_BV), BF16),
            pltpu.VMEM((ts, D_C), BF16),
            pltpu.VMEM((G_C, SGU_CHUNK, SGU_CHUNK), BF16),
            pltpu.VMEM((3, ts, d), F32),
            pltpu.VMEM((ts, d), BF16),
            pltpu.VMEM((H_B * (N_LEVELS + 1), DK_B, CHUNK), BF16),
        ],
        compiler_params=_params("arbitrary", "arbitrary"),
        name="mixer",
    )(ya, hq, gc, gates, x, mod, hgrn_nw.reshape(1, D_BV), sgu_w, sgub,
      sum_mat, up, low, pair, pairc, wa, wb, wc, wo)


def _swiglu_tile(h, w1_at, w3_at, w2, g_ref):
    d_ff = g_ref.shape[1]
    for c0 in range(0, d_ff, MXU_N):
        a = _dot(h, w1_at(c0))
        g_ref[:, c0:c0 + MXU_N] = (_silu(a) * _dot(h, w3_at(c0))).astype(BF16)
    return _dot(g_ref[...], w2)


def _final_norm(v, fw):
    return v * lax.rsqrt(jnp.mean(v * v, axis=-1, keepdims=True) + EPS) * fw


def _ffn_kernel(final, x_ref, mod_ref, nw_ref, fw_ref, w1_ref, w3_ref, w2_ref, o_ref, g_ref):
    x = x_ref[0]
    h = _norm_mod(x, nw_ref[...], mod_ref[0, 3:4, :], mod_ref[0, 4:5, :]).astype(BF16)
    y = _swiglu_tile(h, lambda c0: w1_ref[:, c0:c0 + MXU_N], lambda c0: w3_ref[:, c0:c0 + MXU_N],
                     w2_ref[...], g_ref)
    out = x + mod_ref[0, 5:6, :] * y
    o_ref[0] = _final_norm(out, fw_ref[...]) if final else out


def _ffn(x, mod, nw, fw, w1, w3, w2, tm, final):
    bsz, seq, d = x.shape
    d_ff = w1.shape[1]
    const2 = lambda b, i: (0, 0)
    resident = dict(pipeline_mode=pl.Buffered(1))
    return pl.pallas_call(
        functools.partial(_ffn_kernel, final),
        grid=(bsz, seq // tm),
        in_specs=[
            pl.BlockSpec((1, tm, d), lambda b, i: (b, i, 0)),
            pl.BlockSpec((1, 6, d), lambda b, i: (b, 0, 0)),
            pl.BlockSpec((1, d), const2),
            pl.BlockSpec((1, d), const2),
            pl.BlockSpec((d, d_ff), const2, **resident),
            pl.BlockSpec((d, d_ff), const2, **resident),
            pl.BlockSpec((d_ff, d), const2, **resident),
        ],
        out_specs=pl.BlockSpec((1, tm, d), lambda b, i: (b, i, 0)),
        out_shape=jax.ShapeDtypeStruct((bsz, seq, d), F32),
        scratch_shapes=[pltpu.VMEM((tm, d_ff), BF16)],
        compiler_params=_params("arbitrary", "arbitrary"),
        name="ffn",
    )(x, mod, nw.reshape(1, d), fw.reshape(1, d), w1, w3, w2)


META_E, META_P, META_R = 0, 2, 4


def _router_kernel(x_ref, mod_ref, nw_ref, rw_ref, rb_ref, tri_ref, h_ref, meta_ref, cnt_ref, carry):
    first = jnp.logical_and(pl.program_id(0) == 0, pl.program_id(1) == 0)

    @pl.when(first)
    def _():
        carry[...] = jnp.zeros_like(carry)

    h = _norm_mod(x_ref[0], nw_ref[...], mod_ref[0, 3:4, :], mod_ref[0, 4:5, :])
    h_ref[0] = h
    h_hi = h.astype(BF16)
    h_lo = (h - h_hi.astype(F32)).astype(BF16)
    part = _dot(h_hi, rw_ref[...])
    logits = (part[:, 0:LANES] + part[:, LANES:2 * LANES]) + _dot(h_lo, rw_ref[:, 0:LANES]) + rb_ref[...]
    lane = lax.broadcasted_iota(jnp.int32, logits.shape, 1)
    m1 = jnp.max(logits, axis=-1, keepdims=True)
    i1 = jnp.min(jnp.where(logits == m1, lane, LANES), axis=-1, keepdims=True)
    rest = jnp.where(lane == i1, -jnp.inf, logits)
    m2 = jnp.max(rest, axis=-1, keepdims=True)
    i2 = jnp.min(jnp.where(rest == m2, lane, LANES), axis=-1, keepdims=True)
    e2 = jnp.exp(m2 - m1)
    p1 = 1.0 / (1.0 + e2)
    p2 = e2 / (1.0 + e2)
    sel1 = lane == i1
    sel2 = lane == i2
    onehot = jnp.where(jnp.logical_or(sel1, sel2), 1.0, 0.0)
    before = _dot(tri_ref[...], onehot.astype(BF16)) + carry[...]
    r1 = jnp.sum(jnp.where(sel1, before, 0.0), axis=-1, keepdims=True)
    r2 = jnp.sum(jnp.where(sel2, before, 0.0), axis=-1, keepdims=True)
    carry[...] = carry[...] + jnp.sum(onehot, axis=0, keepdims=True)
    cnt_ref[...] = carry[...]
    rec = jnp.where(lane == META_E, i1.astype(F32), 0.0)
    rec = jnp.where(lane == META_E + 1, i2.astype(F32), rec)
    rec = jnp.where(lane == META_P, p1, rec)
    rec = jnp.where(lane == META_P + 1, p2, rec)
    rec = jnp.where(lane == META_R, r1, rec)
    rec = jnp.where(lane == META_R + 1, r2, rec)
    meta_ref[0] = rec


def _router(x, mod, nw, router_w, router_b, tm):
    bsz, seq, d = x.shape
    ne = router_w.shape[1]
    rw = jnp.zeros((d, LANES), F32).at[:, :ne].set(router_w)
    rw_hi = rw.astype(BF16)
    rw = jnp.concatenate([rw_hi, (rw - rw_hi.astype(F32)).astype(BF16)], axis=1)
    rb = jnp.full((1, LANES), NEG_BIG, F32).at[0, :ne].set(router_b)
    tri = jnp.asarray(np.tril(np.ones((tm, tm), np.float32), -1), BF16)
    const2 = lambda b, i: (0, 0)
    return pl.pallas_call(
        _router_kernel,
        grid=(bsz, seq // tm),
        in_specs=[
            pl.BlockSpec((1, tm, d), lambda b, i: (b, i, 0)),
            pl.BlockSpec((1, 6, d), lambda b, i: (b, 0, 0)),
            pl.BlockSpec((1, d), const2),
            pl.BlockSpec((d, 2 * LANES), const2),
            pl.BlockSpec((1, LANES), const2),
            pl.BlockSpec((tm, tm), const2),
        ],
        out_specs=[
            pl.BlockSpec((1, tm, d), lambda b, i: (b, i, 0)),
            pl.BlockSpec((1, tm, LANES), lambda b, i: (b, i, 0)),
            pl.BlockSpec((1, LANES), const2),
        ],
        out_shape=[
            jax.ShapeDtypeStruct((bsz, seq, d), F32),
            jax.ShapeDtypeStruct((bsz, seq, LANES), F32),
            jax.ShapeDtypeStruct((1, LANES), F32),
        ],
        scratch_shapes=[pltpu.VMEM((1, LANES), F32)],
        compiler_params=_params("arbitrary", "arbitrary"),
        name="router",
    )(x, mod, nw.reshape(1, d), rw, rb, tri)


def _row_copy(src, src_row, dst, dst_row, sem):
    return pltpu.make_async_copy(src.at[pl.ds(src_row, 1), :], dst.at[pl.ds(dst_row, 1), :], sem)


def _dispatch_kernel(tt, pos_ref, h_ref, xs_hbm, sem):
    def issue(t, carry):
        _row_copy(h_ref, t, xs_hbm, pos_ref[0, 0, 2 * t], sem).start(priority=0)
        _row_copy(h_ref, t, xs_hbm, pos_ref[0, 0, 2 * t + 1], sem).start(priority=1)
        return carry

    lax.fori_loop(0, tt, issue, 0, unroll=8)
    for _ in range(2):
        pltpu.make_async_copy(h_ref, xs_hbm.at[pl.ds(0, tt), :], sem).wait()


def _dispatch(h2d, pos, tt):
    n_tok, d = h2d.shape
    return pl.pallas_call(
        functools.partial(_dispatch_kernel, tt),
        grid=(n_tok // tt,),
        in_specs=[
            pl.BlockSpec((1, 1, 2 * tt), lambda i: (i, 0, 0), memory_space=pltpu.SMEM),
            pl.BlockSpec((tt, d), lambda i: (i, 0)),
        ],
        out_specs=pl.BlockSpec(memory_space=pl.ANY),
        out_shape=jax.ShapeDtypeStruct((2 * n_tok, d), F32),
        scratch_shapes=[pltpu.SemaphoreType.DMA(())],
        compiler_params=_params("arbitrary"),
        name="dispatch",
    )(pos.reshape(n_tok // tt, 1, 2 * tt), h2d)


def _expert_kernel(tm, tile_ref, exp_ref, lo_ref, hi_ref, first_ref, valid_ref,
                   xs_ref, w1_ref, w3_ref, w2_ref, y_ref, g_ref):
    s = pl.program_id(0)

    @pl.when(valid_ref[s] == 1)
    def _():
        h = xs_ref[...].astype(BF16)
        res = _swiglu_tile(h, lambda c0: w1_ref[0, :, c0:c0 + MXU_N],
                           lambda c0: w3_ref[0, :, c0:c0 + MXU_N], w2_ref[0], g_ref)
        row = tile_ref[s] * tm + lax.broadcasted_iota(jnp.int32, (tm, 1), 0)
        mine = jnp.logical_and(row >= lo_ref[s], row < hi_ref[s])

        @pl.when(first_ref[s] == 1)
        def _():
            y_ref[...] = jnp.where(mine, res, 0.0)

        @pl.when(first_ref[s] == 0)
        def _():
            y_ref[...] = jnp.where(mine, res, y_ref[...])


def _experts(xs, steps, w1, w3, w2, tm):
    n_rows, d = xs.shape
    d_ff = w1.shape[2]
    n_steps = steps[0].shape[0]
    grid_spec = pltpu.PrefetchScalarGridSpec(
        num_scalar_prefetch=6,
        grid=(n_steps,),
        in_specs=[
            pl.BlockSpec((tm, d), lambda s, tile, exp, lo, hi, first, valid: (tile[s], 0)),
            pl.BlockSpec((1, d, d_ff), lambda s, tile, exp, lo, hi, first, valid: (exp[s], 0, 0)),
            pl.BlockSpec((1, d, d_ff), lambda s, tile, exp, lo, hi, first, valid: (exp[s], 0, 0)),
            pl.BlockSpec((1, d_ff, d), lambda s, tile, exp, lo, hi, first, valid: (exp[s], 0, 0)),
        ],
        out_specs=pl.BlockSpec((tm, d), lambda s, tile, exp, lo, hi, first, valid: (tile[s], 0)),
        scratch_shapes=[pltpu.VMEM((tm, d_ff), BF16)],
    )
    return pl.pallas_call(
        functools.partial(_expert_kernel, tm),
        grid_spec=grid_spec,
        out_shape=jax.ShapeDtypeStruct((n_rows, d), F32),
        compiler_params=_params("arbitrary"),
        name="experts",
    )(*steps, xs, w1, w3, w2)


def _combine_kernel(tt, final, pos_ref, nxt_ref, y_hbm, x_ref, mod_ref, meta_ref, fw_ref, o_ref, buf, sem):
    step = pl.program_id(0) * pl.num_programs(1) + pl.program_id(1)
    n_steps = pl.num_programs(0) * pl.num_programs(1)
    slot = step % 2

    def gather(p_ref, t, s):
        _row_copy(y_hbm, p_ref[0, 0, 2 * t], buf.at[s, 0], t, sem.at[s]).start(priority=0)
        _row_copy(y_hbm, p_ref[0, 0, 2 * t + 1], buf.at[s, 1], t, sem.at[s]).start(priority=1)

    def drain(s):
        for k in range(2):
            pltpu.make_async_copy(y_hbm.at[pl.ds(0, tt), :], buf.at[s, k], sem.at[s]).wait()

    @pl.when(step == 0)
    def _():
        def body(t, carry):
            gather(pos_ref, t, 0)
            return carry

        lax.fori_loop(0, tt, body, 0, unroll=8)

    drain(slot)
    for t in range(tt):
        gather(nxt_ref, t, 1 - slot)
    p1 = meta_ref[0, :, META_P:META_P + 1]
    p2 = meta_ref[0, :, META_P + 1:META_P + 2]
    out = x_ref[0] + mod_ref[0, 5:6, :] * (p1 * buf[slot, 0] + p2 * buf[slot, 1])
    o_ref[0] = _final_norm(out, fw_ref[...]) if final else out

    @pl.when(step == n_steps - 1)
    def _():
        drain(1 - slot)


def _combine(y, pos, x, mod, meta, fw, tt, final):
    bsz, seq, d = x.shape
    per_b = seq // tt
    last = bsz * per_b - 1
    return pl.pallas_call(
        functools.partial(_combine_kernel, tt, final),
        grid=(bsz, per_b),
        in_specs=[
            pl.BlockSpec((1, 1, 2 * tt), lambda b, i: (b * per_b + i, 0, 0), memory_space=pltpu.SMEM),
            pl.BlockSpec((1, 1, 2 * tt), lambda b, i: (jnp.minimum(b * per_b + i + 1, last), 0, 0),
                         memory_space=pltpu.SMEM),
            pl.BlockSpec(memory_space=pl.ANY),
            pl.BlockSpec((1, tt, d), lambda b, i: (b, i, 0)),
            pl.BlockSpec((1, 6, d), lambda b, i: (b, 0, 0)),
            pl.BlockSpec((1, tt, LANES), lambda b, i: (b, i, 0)),
            pl.BlockSpec((1, d), lambda b, i: (0, 0)),
        ],
        out_specs=pl.BlockSpec((1, tt, d), lambda b, i: (b, i, 0)),
        out_shape=jax.ShapeDtypeStruct((bsz, seq, d), F32),
        scratch_shapes=[pltpu.VMEM((2, 2, tt, d), F32), pltpu.SemaphoreType.DMA((2,))],
        compiler_params=_params("arbitrary", "arbitrary"),
        name="combine",
    )(pos.reshape(bsz * per_b, 1, 2 * tt), pos.reshape(bsz * per_b, 1, 2 * tt), y, x, mod, meta,
      fw.reshape(1, d))


def _expert_steps(counts, n_rows, tm):
    n_tiles = n_rows // tm
    n_steps = n_tiles + N_EXPERTS - 1
    ends = jnp.cumsum(counts)
    starts = ends - counts
    first_tile = starts // tm
    last_tile = jnp.maximum(ends - 1, 0) // tm
    tiles_e = jnp.where(counts > 0, last_tile - first_tile + 1, 0)
    step_end = jnp.cumsum(tiles_e)
    step_start = step_end - tiles_e
    total = step_end[-1]
    s = jnp.arange(n_steps, dtype=jnp.int32)
    sc = jnp.minimum(s, jnp.maximum(total - 1, 0))
    e_s = jnp.minimum(jnp.sum((sc[:, None] >= step_end[None, :]).astype(jnp.int32), axis=1), N_EXPERTS - 1)
    tile_s = jnp.clip(first_tile[e_s] + (sc - step_start[e_s]), 0, n_tiles - 1)
    valid = (s < total).astype(jnp.int32)
    prev_tile = jnp.concatenate([jnp.full((1,), -1, jnp.int32), tile_s[:-1]])
    first = jnp.logical_and(tile_s != prev_tile, valid == 1).astype(jnp.int32)
    as_i32 = lambda v: v.astype(jnp.int32)
    return (as_i32(tile_s), as_i32(e_s), as_i32(starts[e_s]), as_i32(ends[e_s]), first, valid)


def _moe(x, mod, nw, fw, router_w, router_b, w1, w3, w2, tm_route, tt, tm_exp, final):
    bsz, seq, d = x.shape
    n_tok = bsz * seq
    h, meta, cnt = _router(x, mod, nw, router_w, router_b, tm_route)
    counts = cnt[0, :N_EXPERTS].astype(jnp.int32)
    starts = jnp.cumsum(counts) - counts
    meta2 = meta.reshape(n_tok, LANES)
    experts = meta2[:, META_E:META_E + 2].astype(jnp.int32)
    ranks = meta2[:, META_R:META_R + 2].astype(jnp.int32)
    onehot = experts[:, :, None] == jnp.arange(N_EXPERTS, dtype=jnp.int32)[None, None, :]
    pos = jnp.sum(jnp.where(onehot, starts[None, None, :], 0), axis=-1) + ranks
    xs = _dispatch(h.reshape(n_tok, d), pos, _pick(n_tok, 2 * tt))
    y = _experts(xs, _expert_steps(counts, 2 * n_tok, tm_exp), w1, w3, w2, tm_exp)
    return _combine(y, pos, x, mod, meta, fw, tt, final)


def _pick(n, pref):
    t = min(n, pref)
    assert n % t == 0, (n, pref)
    return t


def kernel(x, c, ada_w, ada_b, norm_mix_w, norm_ffn_w, w_in, conv_w, hgrn_lb_logits, hgrn_norm_w,
           sgu_norm_w, sgu_w, sgu_b, w_br_a, w_br_b, w_br_c, w_o, ffn_w1, ffn_w3, ffn_w2,
           moe_router_w, moe_router_b, moe_w1, moe_w3, moe_w2, final_norm_w):
    depth = ada_w.shape[0]
    bsz, seq, d = x.shape
    assert w_in.shape[-1] == OFF_GATE + 3 * d and seq % CHUNK == 0
    assert ffn_w1.shape[-1] % MXU_N == 0 and moe_w1.shape[-1] % MXU_N == 0

    tm_in = _pick(seq, 512)
    ts = _pick(seq, 512)
    tm_ffn = _pick(seq, 512)
    tm_route = _pick(seq, 512)
    tt = _pick(seq, 512)
    tm_exp = _pick(2 * bsz * seq, 512)

    mod_all = _adaln(c, ada_w, ada_b).reshape(depth, bsz, 6, d)
    tables = _level_tables()
    bf = lambda w: w.astype(BF16)

    for l in range(depth):
        mod = mod_all[l]
        ya, hq, gc, gates = _inproj(l, x, mod, norm_mix_w[l], bf(w_in[l]), conv_w[l], hgrn_lb_logits,
                                    sgu_norm_w[l], tm_in)
        x = _mixer(ya, hq, gc, gates, x, mod, hgrn_norm_w[l], sgu_w[l], sgu_b[l], tables,
                   bf(w_br_a[l]), bf(w_br_b[l]), bf(w_br_c[l]), bf(w_o[l]), ts)
        final = l == depth - 1
        j = l // 2
        if l % 2 == 0:
            x = _ffn(x, mod, norm_ffn_w[l], final_norm_w, bf(ffn_w1[j]), bf(ffn_w3[j]), bf(ffn_w2[j]),
                     tm_ffn, final)
        else:
            x = _moe(x, mod, norm_ffn_w[l], final_norm_w, moe_router_w[j], moe_router_b[j],
                     bf(moe_w1[j]), bf(moe_w3[j]), bf(moe_w2[j]), tm_route, tt, tm_exp, final)
    return x
```

```python
import functools

import numpy as np
import jax
import jax.numpy as jnp
from jax import lax
from jax.experimental import pallas as pl
from jax.experimental.pallas import tpu as pltpu

F32 = jnp.float32
BF16 = jnp.bfloat16

EPS = 1e-6
LB_FLOOR = 1e-30

D_A = 512
CONV_WIDTH = 3
H_B = 4
DK_B = 128
DV_B = 128
D_BK = H_B * DK_B
D_BV = H_B * DV_B
G_C = 4
GC_CH = 128
D_C = G_C * GC_CH
SGU_CHUNK = 128
N_EXPERTS = 8

OFF_BQ = 3 * D_A
OFF_BG = OFF_BQ + 2 * D_BK + D_BV
OFF_GATE = OFF_BG + D_BV + 2 * D_C

CHUNK = 128
N_LEVELS = 7
assert 1 << N_LEVELS == CHUNK and CHUNK == SGU_CHUNK

LANES = 128
MXU_N = 256
VMEM_LIMIT = 60 * 1024 * 1024
NEG_BIG = -1e30


def _sigmoid(v):
    return 0.5 + 0.5 * jnp.tanh(0.5 * v)


def _silu(v):
    return v * _sigmoid(v)


def _dot(a, b):
    return jnp.dot(a, b, preferred_element_type=F32)


def _dot_nt(a, b):
    return lax.dot_general(a, b, (((1,), (1,)), ((), ())), preferred_element_type=F32)


def _dot_tn(a, b):
    return lax.dot_general(a, b, (((0,), (0,)), ((), ())), preferred_element_type=F32)


def _norm_mod(x, nw, shift, scale):
    ms = jnp.mean(x * x, axis=-1, keepdims=True)
    return (x * lax.rsqrt(ms + EPS) * nw) * (1.0 + scale) + shift


def _params(*sem):
    return pltpu.CompilerParams(dimension_semantics=sem, vmem_limit_bytes=VMEM_LIMIT)


def _adaln_kernel(c_ref, w_ref, b_ref, o_ref):
    ca = _silu(c_ref[...]).astype(BF16)
    o_ref[0] = _dot(ca, w_ref[0].astype(BF16)) + b_ref[0]


def _adaln(c, ada_w, ada_b):
    depth, d, n = ada_w.shape
    bsz = c.shape[0]
    tn = n // 4
    return pl.pallas_call(
        _adaln_kernel,
        grid=(depth, n // tn),
        in_specs=[
            pl.BlockSpec((bsz, d), lambda l, j: (0, 0)),
            pl.BlockSpec((1, d, tn), lambda l, j: (l, 0, j)),
            pl.BlockSpec((1, 1, tn), lambda l, j: (l, 0, j)),
        ],
        out_specs=pl.BlockSpec((1, bsz, tn), lambda l, j: (l, 0, j)),
        out_shape=jax.ShapeDtypeStruct((depth, bsz, n), F32),
        compiler_params=_params("arbitrary", "arbitrary"),
        name="adaln",
    )(c, ada_w, ada_b.reshape(depth, 1, n))


HQ_Q, HQ_F, HQ_K, HQ_V = 0, D_BK, 2 * D_BK, 3 * D_BK
GC_G, GC_U, GC_V = 0, D_BV, D_BV + D_C
assert D_BK == D_BV == D_C == D_A


def _inproj_kernel(layer, x_ref, mod_ref, nw_ref, w_ref, convw_ref, lbl_ref, snw_ref,
                   ya_ref, hq_ref, gc_ref, gates_ref, zbuf):
    tm = x_ref.shape[1]
    d = x_ref.shape[2]

    @pl.when(pl.program_id(1) == 0)
    def _():
        zbuf[0:8, :] = jnp.zeros((8, D_A), F32)

    h = _norm_mod(x_ref[0], nw_ref[...], mod_ref[0, 0:1, :], mod_ref[0, 1:2, :]).astype(BF16)

    def group(off):
        return _dot(h, w_ref[:, off:off + D_A])

    z = group(D_A) * group(2 * D_A)
    zbuf[8:8 + tm, :] = z
    y = (convw_ref[0:1, :] * z + convw_ref[1:2, :] * zbuf[7:7 + tm, :]
         + convw_ref[2:3, :] * zbuf[6:6 + tm, :])
    ya_ref[0] = (group(0) * y).astype(BF16)
    zbuf[0:8, :] = zbuf[tm:tm + 8, :]

    lg = lbl_ref[...]
    pe = jnp.exp(lg - jnp.max(lg, axis=0, keepdims=True))
    p = pe / jnp.sum(pe, axis=0, keepdims=True)
    lb = jnp.clip(jnp.sum(p[0:layer + 1], axis=0, keepdims=True) - p[0:1], 0.0, 1.0)
    hq_ref[0, :, HQ_Q:HQ_Q + D_BK] = _silu(group(OFF_BQ)).astype(BF16)
    hq_ref[0, :, HQ_V:HQ_V + D_BV] = group(OFF_BQ + 2 * D_BK).astype(BF16)
    sig = _sigmoid(group(OFF_BQ + D_BK))
    hq_ref[0, :, HQ_F:HQ_F + D_BK] = jnp.log(jnp.maximum(lb, LB_FLOOR) + (1.0 - lb) * sig).astype(BF16)
    hq_ref[0, :, HQ_K:HQ_K + D_BK] = ((1.0 - lb) * (1.0 - sig)).astype(BF16)

    gc_ref[0, :, GC_G:GC_G + D_BV] = _silu(group(OFF_BG)).astype(BF16)
    gc_ref[0, :, GC_U:GC_U + D_C] = group(OFF_BG + D_BV).astype(BF16)
    cv = group(OFF_BG + D_BV + D_C)
    vn = cv * lax.rsqrt(jnp.mean(cv * cv, axis=-1, keepdims=True) + EPS) * snw_ref[...]
    gc_ref[0, :, GC_V:GC_V + D_C] = vn.astype(BF16)

    for g in range(3 * d // D_A):
        gates_ref[0, :, g * D_A:(g + 1) * D_A] = _sigmoid(group(OFF_GATE + g * D_A)).astype(BF16)


def _inproj(layer, x, mod, nw, w_bf16, conv_w, lb_logits, sgu_nw, tm):
    bsz, seq, d = x.shape
    n = w_bf16.shape[1]
    assert n == OFF_GATE + 3 * d and (3 * d) % D_A == 0
    row = lambda b, i: (b, i, 0)
    const2 = lambda b, i: (0, 0)
    out = lambda width: jax.ShapeDtypeStruct((bsz, seq, width), BF16)
    return pl.pallas_call(
        functools.partial(_inproj_kernel, layer),
        grid=(bsz, seq // tm),
        in_specs=[
            pl.BlockSpec((1, tm, d), row),
            pl.BlockSpec((1, 6, d), lambda b, i: (b, 0, 0)),
            pl.BlockSpec((1, d), const2),
            pl.BlockSpec((d, n), const2, pipeline_mode=pl.Buffered(1)),
            pl.BlockSpec(conv_w.shape, const2),
            pl.BlockSpec(lb_logits.shape, const2),
            pl.BlockSpec((1, D_C), const2),
        ],
        out_specs=[
            pl.BlockSpec((1, tm, D_A), row),
            pl.BlockSpec((1, tm, 4 * D_BK), row),
            pl.BlockSpec((1, tm, 3 * D_C), row),
            pl.BlockSpec((1, tm, 3 * d), row),
        ],
        out_shape=[out(D_A), out(4 * D_BK), out(3 * D_C), out(3 * d)],
        scratch_shapes=[pltpu.VMEM((tm + 8, D_A), F32)],
        compiler_params=_params("arbitrary", "arbitrary"),
        name="inproj",
    )(x, mod, nw.reshape(1, d), w_bf16, conv_w, lb_logits, sgu_nw.reshape(1, D_C))


def _level_tables():
    t = np.arange(CHUNK)
    sum_mat = np.zeros((N_LEVELS + 1, CHUNK, CHUNK), np.float32)
    sum_mat[0] = (t[None, :] <= t[:, None])
    up = np.zeros((N_LEVELS + 1, CHUNK, LANES), np.float32)
    low = np.zeros((N_LEVELS + 1, CHUNK, LANES), np.float32)
    pair = np.zeros((N_LEVELS + 1, CHUNK, CHUNK), np.float32)
    pair[0] = np.eye(CHUNK)
    for j in range(1, N_LEVELS + 1):
        blk, half = 1 << j, 1 << (j - 1)
        base = (t // blk) * blk
        m = base + half - 1
        is_up = (t - base) >= half
        u = t[None, :]
        upper_rows = (u > m[:, None]) & (u <= t[:, None])
        lower_rows = (u > t[:, None]) & (u <= m[:, None])
        sum_mat[j] = np.where(is_up[:, None], upper_rows, lower_rows)
        up[j] = is_up[:, None]
        low[j] = ~is_up[:, None]
        pair[j] = (t[:, None] // blk) == (t[None, :] // blk)
    pairc = np.zeros((N_LEVELS + 1, CHUNK // 2, CHUNK), np.float32)
    for j in range(1, N_LEVELS + 1):
        pairc[j] = pair[j][np.nonzero(((t >> (j - 1)) & 1) == 1)[0]]
    return (jnp.asarray(sum_mat.reshape((N_LEVELS + 1) * CHUNK, CHUNK), BF16),
            jnp.asarray(up), jnp.asarray(low), jnp.asarray(pair), jnp.asarray(pairc))


HEAD_GROUPS = ((0, 1), (2, 3))
SUBLANES = 8
MERGE_ROWS = 128


def _row_blocks(m):
    return [m[SUBLANES * r:SUBLANES * (r + 1), :] for r in range(m.shape[0] // SUBLANES)]


def _mixer_kernel(ts,
                  ya_ref, hq_ref, gc_ref, gates_ref, x_ref, mod_ref, hnw_ref, sguw_ref, sgub_ref,
                  summat_ref, up_ref, low_ref, pair_ref, pairc_ref, wa_ref, wb_ref, wc_ref, wo_ref,
                  o_ref,
                  st_ref, yb_ref, yc_ref, wsm_ref, br_ref, mg_ref, kt_ref):
    n_chunks = ts // CHUNK
    d = x_ref.shape[-1]

    @pl.when(pl.program_id(1) == 0)
    def _():
        st_ref[...] = jnp.zeros_like(st_ref)

    tril = (lax.broadcasted_iota(jnp.int32, (SGU_CHUNK, SGU_CHUNK), 0)
            >= lax.broadcasted_iota(jnp.int32, (SGU_CHUNK, SGU_CHUNK), 1))
    for g in range(G_C):
        wsm_ref[g] = jnp.where(tril, sguw_ref[g], 0.0).astype(BF16)

    def chunk_body(c, carry):
        r0 = pl.multiple_of(c * CHUNK, CHUNK)
        rows = pl.ds(r0, CHUNK)

        for g in range(G_C):
            cs = slice(g * GC_CH, (g + 1) * GC_CH)
            vn = gc_ref[0, rows, GC_V + g * GC_CH:GC_V + (g + 1) * GC_CH]
            u = gc_ref[0, rows, GC_U + g * GC_CH:GC_U + (g + 1) * GC_CH].astype(F32)
            yc_ref[rows, cs] = (u * (_dot(wsm_ref[g], vn) + sgub_ref[g])).astype(BF16)

        lv = _dot(summat_ref[...], hq_ref[0, rows, HQ_F:HQ_F + D_BK])

        for heads in HEAD_GROUPS:
            cs = {h: slice(h * DK_B, (h + 1) * DK_B) for h in heads}
            q_bf = {h: hq_ref[0, rows, HQ_Q + h * DK_B:HQ_Q + (h + 1) * DK_B] for h in heads}
            k_bf = {h: hq_ref[0, rows, HQ_K + h * DK_B:HQ_K + (h + 1) * DK_B] for h in heads}
            v_h = {h: hq_ref[0, rows, HQ_V + h * DV_B:HQ_V + (h + 1) * DV_B] for h in heads}
            q_h = {h: q_bf[h].astype(F32) for h in heads}
            k_h = {h: k_bf[h].astype(F32) for h in heads}
            scores = {}
            for h in heads:
                slot = h * (N_LEVELS + 1)
                kt_ref[slot] = k_bf[h].T
                scores[h] = _row_blocks(pair_ref[0] * _dot(q_bf[h], kt_ref[slot]))
            for j in range(1, N_LEVELS + 1):
                half = 1 << (j - 1)
                for h in heads:
                    e = jnp.exp(lv[j * CHUNK:(j + 1) * CHUNK, cs[h]])
                    slot = h * (N_LEVELS + 1) + j
                    if half < SUBLANES:
                        q_j = (q_h[h] * e * up_ref[j]).astype(BF16)
                        k_j = (k_h[h] * e * low_ref[j]).astype(BF16)
                        kt_ref[slot] = k_j.T
                        s_j = _row_blocks(pair_ref[j] * _dot(q_j, kt_ref[slot]))
                        scores[h] = [a + b for a, b in zip(scores[h], s_j)]
                    else:
                        zero = jnp.zeros((half, DK_B), F32)
                        k_parts, q_parts, up_blocks = [], [], []
                        for a in range(0, CHUNK, 2 * half):
                            k_parts += [k_h[h][a:a + half] * e[a:a + half], zero]
                            q_parts.append(q_h[h][a + half:a + 2 * half] * e[a + half:a + 2 * half])
                            up_blocks += list(range((a + half) // SUBLANES, (a + 2 * half) // SUBLANES))
                        kt_ref[slot] = jnp.concatenate(k_parts, axis=0).astype(BF16).T
                        q_c = jnp.concatenate(q_parts, axis=0).astype(BF16)
                        s_c = _row_blocks(pairc_ref[j] * _dot(q_c, kt_ref[slot]))
                        for r, blk in zip(up_blocks, s_c):
                            scores[h][r] = scores[h][r] + blk
            for h in heads:
                b_h = lv[0:CHUNK, cs[h]]
                st = st_ref[h]
                o = _dot_nt((q_h[h] * jnp.exp(b_h)).astype(BF16), st.astype(BF16))
                o = o + _dot(jnp.concatenate(scores[h], axis=0).astype(BF16), v_h[h])
                b_last = b_h[CHUNK - 1:CHUNK, :]
                k_dec = (k_h[h] * jnp.exp(b_last - b_h)).astype(BF16)
                st_ref[h] = st * jnp.exp(b_last) + _dot_tn(v_h[h], k_dec)
                on = o * lax.rsqrt(jnp.mean(o * o, axis=-1, keepdims=True) + EPS) * hnw_ref[:, cs[h]]
                sg = gc_ref[0, rows, GC_G + h * DV_B:GC_G + (h + 1) * DV_B].astype(F32)
                yb_ref[rows, cs[h]] = (on * sg).astype(BF16)
        return carry

    lax.fori_loop(0, n_chunks, chunk_body, 0)

    br_ref[0] = _dot(ya_ref[0], wa_ref[...])
    br_ref[1] = _dot(yb_ref[...], wb_ref[...])
    br_ref[2] = _dot(yc_ref[...], wc_ref[...])

    def merge_body(r, carry):
        rows = pl.ds(pl.multiple_of(r * MERGE_ROWS, MERGE_ROWS), MERGE_ROWS)
        merged = gates_ref[0, rows, 0:d].astype(F32) * br_ref[0, rows, :]
        merged = merged + gates_ref[0, rows, d:2 * d].astype(F32) * br_ref[1, rows, :]
        merged = merged + gates_ref[0, rows, 2 * d:3 * d].astype(F32) * br_ref[2, rows, :]
        mg_ref[rows, :] = merged.astype(BF16)
        return carry

    lax.fori_loop(0, ts // MERGE_ROWS, merge_body, 0)
    o_ref[0] = x_ref[0] + mod_ref[0, 2:3, :] * _dot(mg_ref[...], wo_ref[...])


def _mixer(ya, hq, gc, gates, x, mod, hgrn_nw, sgu_w, sgu_b, tables, wa, wb, wc, wo, ts):
    bsz, seq, d = x.shape
    sum_mat, up, low, pair, pairc = tables
    sgub = jnp.broadcast_to(sgu_b[:, :, None], (G_C, SGU_CHUNK, GC_CH))
    row = lambda b, i: (b, i, 0)
    const2 = lambda b, i: (0, 0)
    const3 = lambda b, i: (0, 0, 0)
    return pl.pallas_call(
        functools.partial(_mixer_kernel, ts),
        grid=(bsz, seq // ts),
        in_specs=[
            pl.BlockSpec((1, ts, ya.shape[-1]), row),
            pl.BlockSpec((1, ts, hq.shape[-1]), row),
            pl.BlockSpec((1, ts, gc.shape[-1]), row),
            pl.BlockSpec((1, ts, gates.shape[-1]), row),
            pl.BlockSpec((1, ts, d), row),
            pl.BlockSpec((1, 6, d), lambda b, i: (b, 0, 0)),
            pl.BlockSpec((1, D_BV), const2),
            pl.BlockSpec(sgu_w.shape, const3),
            pl.BlockSpec(sgub.shape, const3),
            pl.BlockSpec(sum_mat.shape, const2),
            pl.BlockSpec(up.shape, const3),
            pl.BlockSpec(low.shape, const3),
            pl.BlockSpec(pair.shape, const3),
            pl.BlockSpec(pairc.shape, const3),
            pl.BlockSpec(wa.shape, const2),
            pl.BlockSpec(wb.shape, const2),
            pl.BlockSpec(wc.shape, const2),
            pl.BlockSpec(wo.shape, const2),
        ],
        out_specs=pl.BlockSpec((1, ts, d), row),
        out_shape=jax.ShapeDtypeStruct((bsz, seq, d), F32),
        scratch_shapes=[
            pltpu.VMEM((H_B, DV_B, DK_B), F32),
            pltpu.VMEM((ts, D_BV), BF16),
            pltpu.VMEM((ts, D_C), BF16),
            pltpu.VMEM((G_C, SGU_CHUNK, SGU_CHUNK), BF16),
            pltpu.VMEM((3, ts, d), F32),
            pltpu.VMEM((ts, d), BF16),
            pltpu.VMEM((H_B * (N_LEVELS + 1), DK_B, CHUNK), BF16),
        ],
        compiler_params=_params("arbitrary", "arbitrary"),
        name="mixer",
    )(ya, hq, gc, gates, x, mod, hgrn_nw.reshape(1, D_BV), sgu_w, sgub,
      sum_mat, up, low, pair, pairc, wa, wb, wc, wo)


def _swiglu_tile(h, w1_at, w3_at, w2, g_ref):
    d_ff = g_ref.shape[1]
    for c0 in range(0, d_ff, MXU_N):
        a = _dot(h, w1_at(c0))
        g_ref[:, c0:c0 + MXU_N] = (_silu(a) * _dot(h, w3_at(c0))).astype(BF16)
    return _dot(g_ref[...], w2)


def _final_norm(v, fw):
    return v * lax.rsqrt(jnp.mean(v * v, axis=-1, keepdims=True) + EPS) * fw


def _ffn_kernel(final, x_ref, mod_ref, nw_ref, fw_ref, w1_ref, w3_ref, w2_ref, o_ref, g_ref):
    x = x_ref[0]
    h = _norm_mod(x, nw_ref[...], mod_ref[0, 3:4, :], mod_ref[0, 4:5, :]).astype(BF16)
    y = _swiglu_tile(h, lambda c0: w1_ref[:, c0:c0 + MXU_N], lambda c0: w3_ref[:, c0:c0 + MXU_N],
                     w2_ref[...], g_ref)
    out = x + mod_ref[0, 5:6, :] * y
    o_ref[0] = _final_norm(out, fw_ref[...]) if final else out


def _ffn(x, mod, nw, fw, w1, w3, w2, tm, final):
    bsz, seq, d = x.shape
    d_ff = w1.shape[1]
    const2 = lambda b, i: (0, 0)
    resident = dict(pipeline_mode=pl.Buffered(1))
    return pl.pallas_call(
        functools.partial(_ffn_kernel, final),
        grid=(bsz, seq // tm),
        in_specs=[
            pl.BlockSpec((1, tm, d), lambda b, i: (b, i, 0)),
            pl.BlockSpec((1, 6, d), lambda b, i: (b, 0, 0)),
            pl.BlockSpec((1, d), const2),
            pl.BlockSpec((1, d), const2),
            pl.BlockSpec((d, d_ff), const2, **resident),
            pl.BlockSpec((d, d_ff), const2, **resident),
            pl.BlockSpec((d_ff, d), const2, **resident),
        ],
        out_specs=pl.BlockSpec((1, tm, d), lambda b, i: (b, i, 0)),
        out_shape=jax.ShapeDtypeStruct((bsz, seq, d), F32),
        scratch_shapes=[pltpu.VMEM((tm, d_ff), BF16)],
        compiler_params=_params("arbitrary", "arbitrary"),
        name="ffn",
    )(x, mod, nw.reshape(1, d), fw.reshape(1, d), w1, w3, w2)


META_E, META_P, META_R = 0, 2, 4


def _router_kernel(x_ref, mod_ref, nw_ref, rw_ref, rb_ref, tri_ref, h_ref, meta_ref, rt_ref, cnt_ref, carry):
    first = jnp.logical_and(pl.program_id(0) == 0, pl.program_id(1) == 0)

    @pl.when(first)
    def _():
        carry[...] = jnp.zeros_like(carry)

    h = _norm_mod(x_ref[0], nw_ref[...], mod_ref[0, 3:4, :], mod_ref[0, 4:5, :])
    h_ref[0] = h
    h_hi = h.astype(BF16)
    h_lo = (h - h_hi.astype(F32)).astype(BF16)
    part = _dot(h_hi, rw_ref[...])
    logits = (part[:, 0:LANES] + part[:, LANES:2 * LANES]) + _dot(h_lo, rw_ref[:, 0:LANES]) + rb_ref[...]
    lane = lax.broadcasted_iota(jnp.int32, logits.shape, 1)
    m1 = jnp.max(logits, axis=-1, keepdims=True)
    i1 = jnp.min(jnp.where(logits == m1, lane, LANES), axis=-1, keepdims=True)
    rest = jnp.where(lane == i1, -jnp.inf, logits)
    m2 = jnp.max(rest, axis=-1, keepdims=True)
    i2 = jnp.min(jnp.where(rest == m2, lane, LANES), axis=-1, keepdims=True)
    e2 = jnp.exp(m2 - m1)
    p1 = 1.0 / (1.0 + e2)
    p2 = e2 / (1.0 + e2)
    sel1 = lane == i1
    sel2 = lane == i2
    onehot = jnp.where(jnp.logical_or(sel1, sel2), 1.0, 0.0)
    before = _dot(tri_ref[...], onehot.astype(BF16)) + carry[...]
    r1 = jnp.sum(jnp.where(sel1, before, 0.0), axis=-1, keepdims=True)
    r2 = jnp.sum(jnp.where(sel2, before, 0.0), axis=-1, keepdims=True)
    carry[...] = carry[...] + jnp.sum(onehot, axis=0, keepdims=True)
    cnt_ref[...] = carry[...]
    rec = jnp.where(lane == META_E, i1.astype(F32), 0.0)
    rec = jnp.where(lane == META_E + 1, i2.astype(F32), rec)
    rec = jnp.where(lane == META_P, p1, rec)
    rec = jnp.where(lane == META_P + 1, p2, rec)
    rec = jnp.where(lane == META_R, r1, rec)
    rec = jnp.where(lane == META_R + 1, r2, rec)
    meta_ref[0] = rec
    rt_ref[...] = rec.T[0:SUBLANES, :]


def _router(x, mod, nw, router_w, router_b, tm):
    bsz, seq, d = x.shape
    ne = router_w.shape[1]
    rw = jnp.zeros((d, LANES), F32).at[:, :ne].set(router_w)
    rw_hi = rw.astype(BF16)
    rw = jnp.concatenate([rw_hi, (rw - rw_hi.astype(F32)).astype(BF16)], axis=1)
    rb = jnp.full((1, LANES), NEG_BIG, F32).at[0, :ne].set(router_b)
    tri = jnp.asarray(np.tril(np.ones((tm, tm), np.float32), -1), BF16)
    const2 = lambda b, i: (0, 0)
    return pl.pallas_call(
        _router_kernel,
        grid=(bsz, seq // tm),
        in_specs=[
            pl.BlockSpec((1, tm, d), lambda b, i: (b, i, 0)),
            pl.BlockSpec((1, 6, d), lambda b, i: (b, 0, 0)),
            pl.BlockSpec((1, d), const2),
            pl.BlockSpec((d, 2 * LANES), const2),
            pl.BlockSpec((1, LANES), const2),
            pl.BlockSpec((tm, tm), const2),
        ],
        out_specs=[
            pl.BlockSpec((1, tm, d), lambda b, i: (b, i, 0)),
            pl.BlockSpec((1, tm, LANES), lambda b, i: (b, i, 0)),
            pl.BlockSpec((SUBLANES, tm), lambda b, i: (0, b * (seq // tm) + i)),
            pl.BlockSpec((1, LANES), const2),
        ],
        out_shape=[
            jax.ShapeDtypeStruct((bsz, seq, d), F32),
            jax.ShapeDtypeStruct((bsz, seq, LANES), F32),
            jax.ShapeDtypeStruct((SUBLANES, bsz * seq), F32),
            jax.ShapeDtypeStruct((1, LANES), F32),
        ],
        scratch_shapes=[pltpu.VMEM((1, LANES), F32)],
        compiler_params=_params("arbitrary", "arbitrary"),
        name="router",
    )(x, mod, nw.reshape(1, d), rw, rb, tri)


def _row_copy(src, src_row, dst, dst_row, sem):
    return pltpu.make_async_copy(src.at[pl.ds(src_row, 1), :], dst.at[pl.ds(dst_row, 1), :], sem)


def _dispatch_kernel(tt, p0_ref, p1_ref, h_ref, xs_hbm, sem):
    def issue(t, carry):
        _row_copy(h_ref, t, xs_hbm, p0_ref[0, 0, t], sem).start(priority=0)
        _row_copy(h_ref, t, xs_hbm, p1_ref[0, 0, t], sem).start(priority=1)
        return carry

    lax.fori_loop(0, tt, issue, 0, unroll=8)
    for _ in range(2):
        pltpu.make_async_copy(h_ref, xs_hbm.at[pl.ds(0, tt), :], sem).wait()


def _dispatch(h2d, pos, tt):
    n_tok, d = h2d.shape
    idx = pl.BlockSpec((1, 1, tt), lambda i: (i, 0, 0), memory_space=pltpu.SMEM)
    return pl.pallas_call(
        functools.partial(_dispatch_kernel, tt),
        grid=(n_tok // tt,),
        in_specs=[
            idx,
            idx,
            pl.BlockSpec((tt, d), lambda i: (i, 0)),
        ],
        out_specs=pl.BlockSpec(memory_space=pl.ANY),
        out_shape=jax.ShapeDtypeStruct((2 * n_tok, d), F32),
        scratch_shapes=[pltpu.SemaphoreType.DMA(())],
        compiler_params=_params("arbitrary"),
        name="dispatch",
    )(pos[0].reshape(n_tok // tt, 1, tt), pos[1].reshape(n_tok // tt, 1, tt), h2d)


def _expert_kernel(tm, tile_ref, exp_ref, lo_ref, hi_ref, first_ref, valid_ref,
                   xs_ref, w1_ref, w3_ref, w2_ref, y_ref, g_ref):
    s = pl.program_id(0)

    @pl.when(valid_ref[s] == 1)
    def _():
        h = xs_ref[...].astype(BF16)
        res = _swiglu_tile(h, lambda c0: w1_ref[0, :, c0:c0 + MXU_N],
                           lambda c0: w3_ref[0, :, c0:c0 + MXU_N], w2_ref[0], g_ref)
        row = tile_ref[s] * tm + lax.broadcasted_iota(jnp.int32, (tm, 1), 0)
        mine = jnp.logical_and(row >= lo_ref[s], row < hi_ref[s])

        @pl.when(first_ref[s] == 1)
        def _():
            y_ref[...] = jnp.where(mine, res, 0.0)

        @pl.when(first_ref[s] == 0)
        def _():
            y_ref[...] = jnp.where(mine, res, y_ref[...])


def _experts(xs, steps, w1, w3, w2, tm):
    n_rows, d = xs.shape
    d_ff = w1.shape[2]
    n_steps = steps[0].shape[0]
    grid_spec = pltpu.PrefetchScalarGridSpec(
        num_scalar_prefetch=6,
        grid=(n_steps,),
        in_specs=[
            pl.BlockSpec((tm, d), lambda s, tile, exp, lo, hi, first, valid: (tile[s], 0)),
            pl.BlockSpec((1, d, d_ff), lambda s, tile, exp, lo, hi, first, valid: (exp[s], 0, 0)),
            pl.BlockSpec((1, d, d_ff), lambda s, tile, exp, lo, hi, first, valid: (exp[s], 0, 0)),
            pl.BlockSpec((1, d_ff, d), lambda s, tile, exp, lo, hi, first, valid: (exp[s], 0, 0)),
        ],
        out_specs=pl.BlockSpec((tm, d), lambda s, tile, exp, lo, hi, first, valid: (tile[s], 0)),
        scratch_shapes=[pltpu.VMEM((tm, d_ff), BF16)],
    )
    return pl.pallas_call(
        functools.partial(_expert_kernel, tm),
        grid_spec=grid_spec,
        out_shape=jax.ShapeDtypeStruct((n_rows, d), F32),
        compiler_params=_params("arbitrary"),
        name="experts",
    )(*steps, xs, w1, w3, w2)


def _combine_kernel(tt, final, p0_ref, p1_ref, n0_ref, n1_ref, y_hbm, x_ref, mod_ref, meta_ref, fw_ref,
                    o_ref, buf, sem):
    step = pl.program_id(0) * pl.num_programs(1) + pl.program_id(1)
    n_steps = pl.num_programs(0) * pl.num_programs(1)
    slot = step % 2

    def gather(refs, t, s):
        _row_copy(y_hbm, refs[0][0, 0, t], buf.at[s, 0], t, sem.at[s]).start(priority=0)
        _row_copy(y_hbm, refs[1][0, 0, t], buf.at[s, 1], t, sem.at[s]).start(priority=1)

    def drain(s):
        for k in range(2):
            pltpu.make_async_copy(y_hbm.at[pl.ds(0, tt), :], buf.at[s, k], sem.at[s]).wait()

    @pl.when(step == 0)
    def _():
        def body(t, carry):
            gather((p0_ref, p1_ref), t, 0)
            return carry

        lax.fori_loop(0, tt, body, 0, unroll=8)

    drain(slot)
    for t in range(tt):
        gather((n0_ref, n1_ref), t, 1 - slot)
    p1 = meta_ref[0, :, META_P:META_P + 1]
    p2 = meta_ref[0, :, META_P + 1:META_P + 2]
    out = x_ref[0] + mod_ref[0, 5:6, :] * (p1 * buf[slot, 0] + p2 * buf[slot, 1])
    o_ref[0] = _final_norm(out, fw_ref[...]) if final else out

    @pl.when(step == n_steps - 1)
    def _():
        drain(1 - slot)


def _combine(y, pos, x, mod, meta, fw, tt, final):
    bsz, seq, d = x.shape
    per_b = seq // tt
    last = bsz * per_b - 1
    cur = pl.BlockSpec((1, 1, tt), lambda b, i: (b * per_b + i, 0, 0), memory_space=pltpu.SMEM)
    nxt = pl.BlockSpec((1, 1, tt), lambda b, i: (jnp.minimum(b * per_b + i + 1, last), 0, 0),
                       memory_space=pltpu.SMEM)
    p0 = pos[0].reshape(bsz * per_b, 1, tt)
    p1 = pos[1].reshape(bsz * per_b, 1, tt)
    return pl.pallas_call(
        functools.partial(_combine_kernel, tt, final),
        grid=(bsz, per_b),
        in_specs=[
            cur,
            cur,
            nxt,
            nxt,
            pl.BlockSpec(memory_space=pl.ANY),
            pl.BlockSpec((1, tt, d), lambda b, i: (b, i, 0)),
            pl.BlockSpec((1, 6, d), lambda b, i: (b, 0, 0)),
            pl.BlockSpec((1, tt, LANES), lambda b, i: (b, i, 0)),
            pl.BlockSpec((1, d), lambda b, i: (0, 0)),
        ],
        out_specs=pl.BlockSpec((1, tt, d), lambda b, i: (b, i, 0)),
        out_shape=jax.ShapeDtypeStruct((bsz, seq, d), F32),
        scratch_shapes=[pltpu.VMEM((2, 2, tt, d), F32), pltpu.SemaphoreType.DMA((2,))],
        compiler_params=_params("arbitrary", "arbitrary"),
        name="combine",
    )(p0, p1, p0, p1, y, x, mod, meta, fw.reshape(1, d))


def _expert_steps(counts, n_rows, tm):
    n_tiles = n_rows // tm
    n_steps = n_tiles + N_EXPERTS - 1
    ends = jnp.cumsum(counts)
    starts = ends - counts
    first_tile = starts // tm
    last_tile = jnp.maximum(ends - 1, 0) // tm
    tiles_e = jnp.where(counts > 0, last_tile - first_tile + 1, 0)
    step_end = jnp.cumsum(tiles_e)
    step_start = step_end - tiles_e
    total = step_end[-1]
    s = jnp.arange(n_steps, dtype=jnp.int32)
    sc = jnp.minimum(s, jnp.maximum(total - 1, 0))
    e_s = jnp.minimum(jnp.sum((sc[:, None] >= step_end[None, :]).astype(jnp.int32), axis=1), N_EXPERTS - 1)
    tile_s = jnp.clip(first_tile[e_s] + (sc - step_start[e_s]), 0, n_tiles - 1)
    valid = (s < total).astype(jnp.int32)
    prev_tile = jnp.concatenate([jnp.full((1,), -1, jnp.int32), tile_s[:-1]])
    first = jnp.logical_and(tile_s != prev_tile, valid == 1).astype(jnp.int32)
    as_i32 = lambda v: v.astype(jnp.int32)
    return (as_i32(tile_s), as_i32(e_s), as_i32(starts[e_s]), as_i32(ends[e_s]), first, valid)


def _moe(x, mod, nw, fw, router_w, router_b, w1, w3, w2, tm_route, tt, tm_exp, final):
    bsz, seq, d = x.shape
    n_tok = bsz * seq
    h, meta, rt, cnt = _router(x, mod, nw, router_w, router_b, tm_route)
    counts = cnt[0, :N_EXPERTS].astype(jnp.int32)
    starts = jnp.cumsum(counts) - counts
    experts = rt[META_E:META_E + 2].astype(jnp.int32)
    pos = rt[META_R:META_R + 2].astype(jnp.int32)
    for e in range(N_EXPERTS):
        pos = pos + jnp.where(experts == e, starts[e], 0)
    xs = _dispatch(h.reshape(n_tok, d), pos, _pick(n_tok, 2 * tt))
    y = _experts(xs, _expert_steps(counts, 2 * n_tok, tm_exp), w1, w3, w2, tm_exp)
    return _combine(y, pos, x, mod, meta, fw, tt, final)


def _pick(n, pref):
    t = min(n, pref)
    assert n % t == 0, (n, pref)
    return t


def kernel(x, c, ada_w, ada_b, norm_mix_w, norm_ffn_w, w_in, conv_w, hgrn_lb_logits, hgrn_norm_w,
           sgu_norm_w, sgu_w, sgu_b, w_br_a, w_br_b, w_br_c, w_o, ffn_w1, ffn_w3, ffn_w2,
           moe_router_w, moe_router_b, moe_w1, moe_w3, moe_w2, final_norm_w):
    depth = ada_w.shape[0]
    bsz, seq, d = x.shape
    assert w_in.shape[-1] == OFF_GATE + 3 * d and seq % CHUNK == 0
    assert ffn_w1.shape[-1] % MXU_N == 0 and moe_w1.shape[-1] % MXU_N == 0

    tm_in = _pick(seq, 512)
    ts = _pick(seq, 512)
    tm_ffn = _pick(seq, 512)
    tm_route = _pick(seq, 512)
    tt = _pick(seq, 512)
    tm_exp = _pick(2 * bsz * seq, 512)

    mod_all = _adaln(c, ada_w, ada_b).reshape(depth, bsz, 6, d)
    tables = _level_tables()
    bf = lambda w: w.astype(BF16)

    for l in range(depth):
        mod = mod_all[l]
        ya, hq, gc, gates = _inproj(l, x, mod, norm_mix_w[l], bf(w_in[l]), conv_w[l], hgrn_lb_logits,
                                    sgu_norm_w[l], tm_in)
        x = _mixer(ya, hq, gc, gates, x, mod, hgrn_norm_w[l], sgu_w[l], sgu_b[l], tables,
                   bf(w_br_a[l]), bf(w_br_b[l]), bf(w_br_c[l]), bf(w_o[l]), ts)
        final = l == depth - 1
        j = l // 2
        if l % 2 == 0:
            x = _ffn(x, mod, norm_ffn_w[l], final_norm_w, bf(ffn_w1[j]), bf(ffn_w3[j]), bf(ffn_w2[j]),
                     tm_ffn, final)
        else:
            x = _moe(x, mod, norm_ffn_w[l], final_norm_w, moe_router_w[j], moe_router_b[j],
                     bf(moe_w1[j]), bf(moe_w3[j]), bf(moe_w2[j]), tm_route, tt, tm_exp, final)
    return x
```

```python
import functools

import numpy as np
import jax
import jax.numpy as jnp
from jax import lax
from jax.experimental import pallas as pl
from jax.experimental.pallas import tpu as pltpu

F32 = jnp.float32
BF16 = jnp.bfloat16

EPS = 1e-6
LB_FLOOR = 1e-30

D_A = 512
CONV_WIDTH = 3
H_B = 4
DK_B = 128
DV_B = 128
D_BK = H_B * DK_B
D_BV = H_B * DV_B
G_C = 4
GC_CH = 128
D_C = G_C * GC_CH
SGU_CHUNK = 128
N_EXPERTS = 8

OFF_BQ = 3 * D_A
OFF_BG = OFF_BQ + 2 * D_BK + D_BV
OFF_GATE = OFF_BG + D_BV + 2 * D_C

CHUNK = 128
N_LEVELS = 7
assert 1 << N_LEVELS == CHUNK and CHUNK == SGU_CHUNK

LANES = 128
MXU_N = 256
VMEM_LIMIT = 60 * 1024 * 1024
NEG_BIG = -1e30


def _sigmoid(v):
    return 0.5 + 0.5 * jnp.tanh(0.5 * v)


def _silu(v):
    return v * _sigmoid(v)


def _dot(a, b):
    return jnp.dot(a, b, preferred_element_type=F32)


def _dot_nt(a, b):
    return lax.dot_general(a, b, (((1,), (1,)), ((), ())), preferred_element_type=F32)


def _dot_tn(a, b):
    return lax.dot_general(a, b, (((0,), (0,)), ((), ())), preferred_element_type=F32)


def _norm_mod(x, nw, shift, scale):
    ms = jnp.mean(x * x, axis=-1, keepdims=True)
    return (x * lax.rsqrt(ms + EPS) * nw) * (1.0 + scale) + shift


def _params(*sem):
    return pltpu.CompilerParams(dimension_semantics=sem, vmem_limit_bytes=VMEM_LIMIT)


def _adaln_kernel(c_ref, w_ref, b_ref, o_ref):
    ca = _silu(c_ref[...]).astype(BF16)
    o_ref[0] = _dot(ca, w_ref[0].astype(BF16)) + b_ref[0]


def _adaln(c, ada_w, ada_b):
    depth, d, n = ada_w.shape
    bsz = c.shape[0]
    tn = n // 4
    return pl.pallas_call(
        _adaln_kernel,
        grid=(depth, n // tn),
        in_specs=[
            pl.BlockSpec((bsz, d), lambda l, j: (0, 0)),
            pl.BlockSpec((1, d, tn), lambda l, j: (l, 0, j)),
            pl.BlockSpec((1, 1, tn), lambda l, j: (l, 0, j)),
        ],
        out_specs=pl.BlockSpec((1, bsz, tn), lambda l, j: (l, 0, j)),
        out_shape=jax.ShapeDtypeStruct((depth, bsz, n), F32),
        compiler_params=_params("arbitrary", "arbitrary"),
        name="adaln",
    )(c, ada_w, ada_b.reshape(depth, 1, n))


HQ_Q, HQ_F, HQ_K, HQ_V = 0, D_BK, 2 * D_BK, 3 * D_BK
GC_G, GC_U, GC_V = 0, D_BV, D_BV + D_C
assert D_BK == D_BV == D_C == D_A


def _inproj_kernel(layer, x_ref, mod_ref, nw_ref, w_ref, convw_ref, lbl_ref, snw_ref,
                   ya_ref, hq_ref, gc_ref, gates_ref, zbuf):
    tm = x_ref.shape[1]
    d = x_ref.shape[2]

    @pl.when(pl.program_id(1) == 0)
    def _():
        zbuf[0:8, :] = jnp.zeros((8, D_A), F32)

    h = _norm_mod(x_ref[0], nw_ref[...], mod_ref[0, 0:1, :], mod_ref[0, 1:2, :]).astype(BF16)

    def group(off):
        return _dot(h, w_ref[:, off:off + D_A])

    z = group(D_A) * group(2 * D_A)
    zbuf[8:8 + tm, :] = z
    y = (convw_ref[0:1, :] * z + convw_ref[1:2, :] * zbuf[7:7 + tm, :]
         + convw_ref[2:3, :] * zbuf[6:6 + tm, :])
    ya_ref[0] = (group(0) * y).astype(BF16)
    zbuf[0:8, :] = zbuf[tm:tm + 8, :]

    lg = lbl_ref[...]
    pe = jnp.exp(lg - jnp.max(lg, axis=0, keepdims=True))
    p = pe / jnp.sum(pe, axis=0, keepdims=True)
    lb = jnp.clip(jnp.sum(p[0:layer + 1], axis=0, keepdims=True) - p[0:1], 0.0, 1.0)
    hq_ref[0, :, HQ_Q:HQ_Q + D_BK] = _silu(group(OFF_BQ)).astype(BF16)
    hq_ref[0, :, HQ_V:HQ_V + D_BV] = group(OFF_BQ + 2 * D_BK).astype(BF16)
    sig = _sigmoid(group(OFF_BQ + D_BK))
    hq_ref[0, :, HQ_F:HQ_F + D_BK] = jnp.log(jnp.maximum(lb, LB_FLOOR) + (1.0 - lb) * sig).astype(BF16)
    hq_ref[0, :, HQ_K:HQ_K + D_BK] = ((1.0 - lb) * (1.0 - sig)).astype(BF16)

    gc_ref[0, :, GC_G:GC_G + D_BV] = _silu(group(OFF_BG)).astype(BF16)
    gc_ref[0, :, GC_U:GC_U + D_C] = group(OFF_BG + D_BV).astype(BF16)
    cv = group(OFF_BG + D_BV + D_C)
    vn = cv * lax.rsqrt(jnp.mean(cv * cv, axis=-1, keepdims=True) + EPS) * snw_ref[...]
    gc_ref[0, :, GC_V:GC_V + D_C] = vn.astype(BF16)

    for g in range(3 * d // D_A):
        gates_ref[0, :, g * D_A:(g + 1) * D_A] = _sigmoid(group(OFF_GATE + g * D_A)).astype(BF16)


def _inproj(layer, x, mod, nw, w_bf16, conv_w, lb_logits, sgu_nw, tm):
    bsz, seq, d = x.shape
    n = w_bf16.shape[1]
    assert n == OFF_GATE + 3 * d and (3 * d) % D_A == 0
    row = lambda b, i: (b, i, 0)
    const2 = lambda b, i: (0, 0)
    out = lambda width: jax.ShapeDtypeStruct((bsz, seq, width), BF16)
    return pl.pallas_call(
        functools.partial(_inproj_kernel, layer),
        grid=(bsz, seq // tm),
        in_specs=[
            pl.BlockSpec((1, tm, d), row),
            pl.BlockSpec((1, 6, d), lambda b, i: (b, 0, 0)),
            pl.BlockSpec((1, d), const2),
            pl.BlockSpec((d, n), const2, pipeline_mode=pl.Buffered(1)),
            pl.BlockSpec(conv_w.shape, const2),
            pl.BlockSpec(lb_logits.shape, const2),
            pl.BlockSpec((1, D_C), const2),
        ],
        out_specs=[
            pl.BlockSpec((1, tm, D_A), row),
            pl.BlockSpec((1, tm, 4 * D_BK), row),
            pl.BlockSpec((1, tm, 3 * D_C), row),
            pl.BlockSpec((1, tm, 3 * d), row),
        ],
        out_shape=[out(D_A), out(4 * D_BK), out(3 * D_C), out(3 * d)],
        scratch_shapes=[pltpu.VMEM((tm + 8, D_A), F32)],
        compiler_params=_params("arbitrary", "arbitrary"),
        name="inproj",
    )(x, mod, nw.reshape(1, d), w_bf16, conv_w, lb_logits, sgu_nw.reshape(1, D_C))


def _level_tables():
    t = np.arange(CHUNK)
    sum_mat = np.zeros((N_LEVELS + 1, CHUNK, CHUNK), np.float32)
    sum_mat[0] = (t[None, :] <= t[:, None])
    up = np.zeros((N_LEVELS + 1, CHUNK, LANES), np.float32)
    low = np.zeros((N_LEVELS + 1, CHUNK, LANES), np.float32)
    pair = np.zeros((N_LEVELS + 1, CHUNK, CHUNK), np.float32)
    pair[0] = np.eye(CHUNK)
    for j in range(1, N_LEVELS + 1):
        blk, half = 1 << j, 1 << (j - 1)
        base = (t // blk) * blk
        m = base + half - 1
        is_up = (t - base) >= half
        u = t[None, :]
        upper_rows = (u > m[:, None]) & (u <= t[:, None])
        lower_rows = (u > t[:, None]) & (u <= m[:, None])
        sum_mat[j] = np.where(is_up[:, None], upper_rows, lower_rows)
        up[j] = is_up[:, None]
        low[j] = ~is_up[:, None]
        pair[j] = (t[:, None] // blk) == (t[None, :] // blk)
    pairc = np.zeros((N_LEVELS + 1, CHUNK // 2, CHUNK), np.float32)
    for j in range(1, N_LEVELS + 1):
        pairc[j] = pair[j][np.nonzero(((t >> (j - 1)) & 1) == 1)[0]]
    return (jnp.asarray(sum_mat.reshape((N_LEVELS + 1) * CHUNK, CHUNK), BF16),
            jnp.asarray(up), jnp.asarray(low), jnp.asarray(pair), jnp.asarray(pairc))


HEAD_GROUPS = ((0, 1), (2, 3))
SUBLANES = 8
MERGE_ROWS = 128


def _row_blocks(m):
    return [m[SUBLANES * r:SUBLANES * (r + 1), :] for r in range(m.shape[0] // SUBLANES)]


def _mixer_kernel(ts,
                  ya_ref, hq_ref, gc_ref, gates_ref, x_ref, mod_ref, hnw_ref, sguw_ref, sgub_ref,
                  summat_ref, up_ref, low_ref, pair_ref, pairc_ref, wa_ref, wb_ref, wc_ref, wo_ref,
                  o_ref,
                  st_ref, yb_ref, yc_ref, wsm_ref, br_ref, mg_ref, kt_ref):
    n_chunks = ts // CHUNK
    d = x_ref.shape[-1]

    @pl.when(pl.program_id(1) == 0)
    def _():
        st_ref[...] = jnp.zeros_like(st_ref)

    tril = (lax.broadcasted_iota(jnp.int32, (SGU_CHUNK, SGU_CHUNK), 0)
            >= lax.broadcasted_iota(jnp.int32, (SGU_CHUNK, SGU_CHUNK), 1))
    for g in range(G_C):
        wsm_ref[g] = jnp.where(tril, sguw_ref[g], 0.0).astype(BF16)

    def chunk_body(c, carry):
        r0 = pl.multiple_of(c * CHUNK, CHUNK)
        rows = pl.ds(r0, CHUNK)

        for g in range(G_C):
            cs = slice(g * GC_CH, (g + 1) * GC_CH)
            vn = gc_ref[0, rows, GC_V + g * GC_CH:GC_V + (g + 1) * GC_CH]
            u = gc_ref[0, rows, GC_U + g * GC_CH:GC_U + (g + 1) * GC_CH].astype(F32)
            yc_ref[rows, cs] = (u * (_dot(wsm_ref[g], vn) + sgub_ref[g])).astype(BF16)

        lv = _dot(summat_ref[...], hq_ref[0, rows, HQ_F:HQ_F + D_BK])

        for heads in HEAD_GROUPS:
            cs = {h: slice(h * DK_B, (h + 1) * DK_B) for h in heads}
            q_bf = {h: hq_ref[0, rows, HQ_Q + h * DK_B:HQ_Q + (h + 1) * DK_B] for h in heads}
            k_bf = {h: hq_ref[0, rows, HQ_K + h * DK_B:HQ_K + (h + 1) * DK_B] for h in heads}
            v_h = {h: hq_ref[0, rows, HQ_V + h * DV_B:HQ_V + (h + 1) * DV_B] for h in heads}
            q_h = {h: q_bf[h].astype(F32) for h in heads}
            k_h = {h: k_bf[h].astype(F32) for h in heads}
            scores = {}
            for h in heads:
                slot = h * (N_LEVELS + 1)
                kt_ref[slot] = k_bf[h].T
                scores[h] = _row_blocks(pair_ref[0] * _dot(q_bf[h], kt_ref[slot]))
            for j in range(1, N_LEVELS + 1):
                half = 1 << (j - 1)
                for h in heads:
                    e = jnp.exp(lv[j * CHUNK:(j + 1) * CHUNK, cs[h]])
                    slot = h * (N_LEVELS + 1) + j
                    if half < SUBLANES:
                        q_j = (q_h[h] * e * up_ref[j]).astype(BF16)
                        k_j = (k_h[h] * e * low_ref[j]).astype(BF16)
                        kt_ref[slot] = k_j.T
                        s_j = _row_blocks(pair_ref[j] * _dot(q_j, kt_ref[slot]))
                        scores[h] = [a + b for a, b in zip(scores[h], s_j)]
                    else:
                        zero = jnp.zeros((half, DK_B), F32)
                        k_parts, q_parts, up_blocks = [], [], []
                        for a in range(0, CHUNK, 2 * half):
                            k_parts += [k_h[h][a:a + half] * e[a:a + half], zero]
                            q_parts.append(q_h[h][a + half:a + 2 * half] * e[a + half:a + 2 * half])
                            up_blocks += list(range((a + half) // SUBLANES, (a + 2 * half) // SUBLANES))
                        kt_ref[slot] = jnp.concatenate(k_parts, axis=0).astype(BF16).T
                        q_c = jnp.concatenate(q_parts, axis=0).astype(BF16)
                        s_c = _row_blocks(pairc_ref[j] * _dot(q_c, kt_ref[slot]))
                        for r, blk in zip(up_blocks, s_c):
                            scores[h][r] = scores[h][r] + blk
            for h in heads:
                b_h = lv[0:CHUNK, cs[h]]
                st = st_ref[h]
                o = _dot_nt((q_h[h] * jnp.exp(b_h)).astype(BF16), st.astype(BF16))
                o = o + _dot(jnp.concatenate(scores[h], axis=0).astype(BF16), v_h[h])
                b_last = b_h[CHUNK - 1:CHUNK, :]
                k_dec = (k_h[h] * jnp.exp(b_last - b_h)).astype(BF16)
                st_ref[h] = st * jnp.exp(b_last) + _dot_tn(v_h[h], k_dec)
                on = o * lax.rsqrt(jnp.mean(o * o, axis=-1, keepdims=True) + EPS) * hnw_ref[:, cs[h]]
                sg = gc_ref[0, rows, GC_G + h * DV_B:GC_G + (h + 1) * DV_B].astype(F32)
                yb_ref[rows, cs[h]] = (on * sg).astype(BF16)
        return carry

    lax.fori_loop(0, n_chunks, chunk_body, 0)

    br_ref[0] = _dot(ya_ref[0], wa_ref[...])
    br_ref[1] = _dot(yb_ref[...], wb_ref[...])
    br_ref[2] = _dot(yc_ref[...], wc_ref[...])

    def merge_body(r, carry):
        rows = pl.ds(pl.multiple_of(r * MERGE_ROWS, MERGE_ROWS), MERGE_ROWS)
        merged = gates_ref[0, rows, 0:d].astype(F32) * br_ref[0, rows, :]
        merged = merged + gates_ref[0, rows, d:2 * d].astype(F32) * br_ref[1, rows, :]
        merged = merged + gates_ref[0, rows, 2 * d:3 * d].astype(F32) * br_ref[2, rows, :]
        mg_ref[rows, :] = merged.astype(BF16)
        return carry

    lax.fori_loop(0, ts // MERGE_ROWS, merge_body, 0)
    o_ref[0] = x_ref[0] + mod_ref[0, 2:3, :] * _dot(mg_ref[...], wo_ref[...])


def _mixer(ya, hq, gc, gates, x, mod, hgrn_nw, sgu_w, sgu_b, tables, wa, wb, wc, wo, ts):
    bsz, seq, d = x.shape
    sum_mat, up, low, pair, pairc = tables
    sgub = jnp.broadcast_to(sgu_b[:, :, None], (G_C, SGU_CHUNK, GC_CH))
    row = lambda b, i: (b, i, 0)
    const2 = lambda b, i: (0, 0)
    const3 = lambda b, i: (0, 0, 0)
    return pl.pallas_call(
        functools.partial(_mixer_kernel, ts),
        grid=(bsz, seq // ts),
        in_specs=[
            pl.BlockSpec((1, ts, ya.shape[-1]), row),
            pl.BlockSpec((1, ts, hq.shape[-1]), row),
            pl.BlockSpec((1, ts, gc.shape[-1]), row),
            pl.BlockSpec((1, ts, gates.shape[-1]), row),
            pl.BlockSpec((1, ts, d), row),
            pl.BlockSpec((1, 6, d), lambda b, i: (b, 0, 0)),
            pl.BlockSpec((1, D_BV), const2),
            pl.BlockSpec(sgu_w.shape, const3),
            pl.BlockSpec(sgub.shape, const3),
            pl.BlockSpec(sum_mat.shape, const2),
            pl.BlockSpec(up.shape, const3),
            pl.BlockSpec(low.shape, const3),
            pl.BlockSpec(pair.shape, const3),
            pl.BlockSpec(pairc.shape, const3),
            pl.BlockSpec(wa.shape, const2),
            pl.BlockSpec(wb.shape, const2),
            pl.BlockSpec(wc.shape, const2),
            pl.BlockSpec(wo.shape, const2),
        ],
        out_specs=pl.BlockSpec((1, ts, d), row),
        out_shape=jax.ShapeDtypeStruct((bsz, seq, d), F32),
        scratch_shapes=[
            pltpu.VMEM((H_B, DV_B, DK_B), F32),
            pltpu.VMEM((ts, D_BV), BF16),
            pltpu.VMEM((ts, D_C), BF16),
            pltpu.VMEM((G_C, SGU_CHUNK, SGU_CHUNK), BF16),
            pltpu.VMEM((3, ts, d), F32),
            pltpu.VMEM((ts, d), BF16),
            pltpu.VMEM((H_B * (N_LEVELS + 1), DK_B, CHUNK), BF16),
        ],
        compiler_params=_params("arbitrary", "arbitrary"),
        name="mixer",
    )(ya, hq, gc, gates, x, mod, hgrn_nw.reshape(1, D_BV), sgu_w, sgub,
      sum_mat, up, low, pair, pairc, wa, wb, wc, wo)


def _swiglu_tile(h, w1_at, w3_at, w2, g_ref):
    d_ff = g_ref.shape[1]
    for c0 in range(0, d_ff, MXU_N):
        a = _dot(h, w1_at(c0))
        g_ref[:, c0:c0 + MXU_N] = (_silu(a) * _dot(h, w3_at(c0))).astype(BF16)
    return _dot(g_ref[...], w2)


def _final_norm(v, fw):
    return v * lax.rsqrt(jnp.mean(v * v, axis=-1, keepdims=True) + EPS) * fw


def _ffn_kernel(final, x_ref, mod_ref, nw_ref, fw_ref, w1_ref, w3_ref, w2_ref, o_ref, g_ref):
    x = x_ref[0]
    h = _norm_mod(x, nw_ref[...], mod_ref[0, 3:4, :], mod_ref[0, 4:5, :]).astype(BF16)
    y = _swiglu_tile(h, lambda c0: w1_ref[:, c0:c0 + MXU_N], lambda c0: w3_ref[:, c0:c0 + MXU_N],
                     w2_ref[...], g_ref)
    out = x + mod_ref[0, 5:6, :] * y
    o_ref[0] = _final_norm(out, fw_ref[...]) if final else out


def _ffn(x, mod, nw, fw, w1, w3, w2, tm, final):
    bsz, seq, d = x.shape
    d_ff = w1.shape[1]
    const2 = lambda b, i: (0, 0)
    resident = dict(pipeline_mode=pl.Buffered(1))
    return pl.pallas_call(
        functools.partial(_ffn_kernel, final),
        grid=(bsz, seq // tm),
        in_specs=[
            pl.BlockSpec((1, tm, d), lambda b, i: (b, i, 0)),
            pl.BlockSpec((1, 6, d), lambda b, i: (b, 0, 0)),
            pl.BlockSpec((1, d), const2),
            pl.BlockSpec((1, d), const2),
            pl.BlockSpec((d, d_ff), const2, **resident),
            pl.BlockSpec((d, d_ff), const2, **resident),
            pl.BlockSpec((d_ff, d), const2, **resident),
        ],
        out_specs=pl.BlockSpec((1, tm, d), lambda b, i: (b, i, 0)),
        out_shape=jax.ShapeDtypeStruct((bsz, seq, d), F32),
        scratch_shapes=[pltpu.VMEM((tm, d_ff), BF16)],
        compiler_params=_params("arbitrary", "arbitrary"),
        name="ffn",
    )(x, mod, nw.reshape(1, d), fw.reshape(1, d), w1, w3, w2)


META_E, META_P, META_R = 0, 2, 4


def _router_kernel(x_ref, mod_ref, nw_ref, rw_ref, rb_ref, tri_ref, h_ref, meta_ref, rt_ref, cnt_ref, carry):
    first = jnp.logical_and(pl.program_id(0) == 0, pl.program_id(1) == 0)

    @pl.when(first)
    def _():
        carry[...] = jnp.zeros_like(carry)

    h = _norm_mod(x_ref[0], nw_ref[...], mod_ref[0, 3:4, :], mod_ref[0, 4:5, :])
    h_ref[0] = h
    h_hi = h.astype(BF16)
    h_lo = (h - h_hi.astype(F32)).astype(BF16)
    part = _dot(h_hi, rw_ref[...])
    logits = (part[:, 0:LANES] + part[:, LANES:2 * LANES]) + _dot(h_lo, rw_ref[:, 0:LANES]) + rb_ref[...]
    lane = lax.broadcasted_iota(jnp.int32, logits.shape, 1)
    m1 = jnp.max(logits, axis=-1, keepdims=True)
    i1 = jnp.min(jnp.where(logits == m1, lane, LANES), axis=-1, keepdims=True)
    rest = jnp.where(lane == i1, -jnp.inf, logits)
    m2 = jnp.max(rest, axis=-1, keepdims=True)
    i2 = jnp.min(jnp.where(rest == m2, lane, LANES), axis=-1, keepdims=True)
    e2 = jnp.exp(m2 - m1)
    p1 = 1.0 / (1.0 + e2)
    p2 = e2 / (1.0 + e2)
    sel1 = lane == i1
    sel2 = lane == i2
    onehot = jnp.where(jnp.logical_or(sel1, sel2), 1.0, 0.0)
    before = _dot(tri_ref[...], onehot.astype(BF16)) + carry[...]
    r1 = jnp.sum(jnp.where(sel1, before, 0.0), axis=-1, keepdims=True)
    r2 = jnp.sum(jnp.where(sel2, before, 0.0), axis=-1, keepdims=True)
    carry[...] = carry[...] + jnp.sum(onehot, axis=0, keepdims=True)
    cnt_ref[...] = carry[...]
    rec = jnp.where(lane == META_E, i1.astype(F32), 0.0)
    rec = jnp.where(lane == META_E + 1, i2.astype(F32), rec)
    rec = jnp.where(lane == META_P, p1, rec)
    rec = jnp.where(lane == META_P + 1, p2, rec)
    rec = jnp.where(lane == META_R, r1, rec)
    rec = jnp.where(lane == META_R + 1, r2, rec)
    meta_ref[0] = rec
    rt_ref[...] = rec.T[0:SUBLANES, :]


def _router(x, mod, nw, router_w, router_b, tm):
    bsz, seq, d = x.shape
    ne = router_w.shape[1]
    rw = jnp.zeros((d, LANES), F32).at[:, :ne].set(router_w)
    rw_hi = rw.astype(BF16)
    rw = jnp.concatenate([rw_hi, (rw - rw_hi.astype(F32)).astype(BF16)], axis=1)
    rb = jnp.full((1, LANES), NEG_BIG, F32).at[0, :ne].set(router_b)
    tri = jnp.asarray(np.tril(np.ones((tm, tm), np.float32), -1), BF16)
    const2 = lambda b, i: (0, 0)
    return pl.pallas_call(
        _router_kernel,
        grid=(bsz, seq // tm),
        in_specs=[
            pl.BlockSpec((1, tm, d), lambda b, i: (b, i, 0)),
            pl.BlockSpec((1, 6, d), lambda b, i: (b, 0, 0)),
            pl.BlockSpec((1, d), const2),
            pl.BlockSpec((d, 2 * LANES), const2),
            pl.BlockSpec((1, LANES), const2),
            pl.BlockSpec((tm, tm), const2),
        ],
        out_specs=[
            pl.BlockSpec((1, tm, d), lambda b, i: (b, i, 0)),
            pl.BlockSpec((1, tm, LANES), lambda b, i: (b, i, 0)),
            pl.BlockSpec((SUBLANES, tm), lambda b, i: (0, b * (seq // tm) + i)),
            pl.BlockSpec((1, LANES), const2),
        ],
        out_shape=[
            jax.ShapeDtypeStruct((bsz, seq, d), F32),
            jax.ShapeDtypeStruct((bsz, seq, LANES), F32),
            jax.ShapeDtypeStruct((SUBLANES, bsz * seq), F32),
            jax.ShapeDtypeStruct((1, LANES), F32),
        ],
        scratch_shapes=[pltpu.VMEM((1, LANES), F32)],
        compiler_params=_params("arbitrary", "arbitrary"),
        name="router",
    )(x, mod, nw.reshape(1, d), rw, rb, tri)


def _row_copy(src, src_row, dst, dst_row, sem):
    return pltpu.make_async_copy(src.at[pl.ds(src_row, 1), :], dst.at[pl.ds(dst_row, 1), :], sem)


def _dispatch_kernel(tt, p0_ref, p1_ref, h_ref, xs_hbm, sem):
    def issue(t, carry):
        _row_copy(h_ref, t, xs_hbm, p0_ref[0, 0, t], sem).start(priority=0)
        _row_copy(h_ref, t, xs_hbm, p1_ref[0, 0, t], sem).start(priority=1)
        return carry

    lax.fori_loop(0, tt, issue, 0, unroll=8)
    for _ in range(2):
        pltpu.make_async_copy(h_ref, xs_hbm.at[pl.ds(0, tt), :], sem).wait()


def _dispatch(h2d, pos, tt):
    n_tok, d = h2d.shape
    idx = pl.BlockSpec((1, 1, tt), lambda i: (i, 0, 0), memory_space=pltpu.SMEM)
    return pl.pallas_call(
        functools.partial(_dispatch_kernel, tt),
        grid=(n_tok // tt,),
        in_specs=[
            idx,
            idx,
            pl.BlockSpec((tt, d), lambda i: (i, 0)),
        ],
        out_specs=pl.BlockSpec(memory_space=pl.ANY),
        out_shape=jax.ShapeDtypeStruct((2 * n_tok, d), F32),
        scratch_shapes=[pltpu.SemaphoreType.DMA(())],
        compiler_params=_params("arbitrary"),
        name="dispatch",
    )(pos[0].reshape(n_tok // tt, 1, tt), pos[1].reshape(n_tok // tt, 1, tt), h2d)


def _expert_kernel(tm, tile_ref, exp_ref, lo_ref, hi_ref, first_ref, valid_ref, slot_ref, stage_ref, next_ref,
                   xs_ref, w1_hbm, w3_hbm, w2_hbm, y_ref, g_ref, wb1, wb3, wb2, stg1, stg3, stg2, sem):
    s = pl.program_id(0)
    d_ff = g_ref.shape[1]
    n_chunks = d_ff // MXU_N

    def chunk_copies(e, c, k):
        cols = pl.ds(c * MXU_N, MXU_N)
        return (pltpu.make_async_copy(w1_hbm.at[e, :, cols], stg1.at[k], sem.at[0, k]),
                pltpu.make_async_copy(w3_hbm.at[e, :, cols], stg3.at[k], sem.at[1, k]),
                pltpu.make_async_copy(w2_hbm.at[e, cols, :], stg2.at[k], sem.at[2, k]))

    def start_chunk(e, c):
        for cp in chunk_copies(e, c, c % 2):
            cp.start()

    def land_chunk(e, c, dst):
        for cp in chunk_copies(e, c, c % 2):
            cp.wait()
        cols = slice(c * MXU_N, (c + 1) * MXU_N)
        wb1[dst, :, cols] = stg1[c % 2].astype(BF16)
        wb3[dst, :, cols] = stg3[c % 2].astype(BF16)
        wb2[dst, cols, :] = stg2[c % 2].astype(BF16)

    @pl.when(s == 0)
    def _():
        e0 = exp_ref[0]
        start_chunk(e0, 0)
        for c in range(n_chunks):
            if c + 1 < n_chunks:
                start_chunk(e0, c + 1)
            land_chunk(e0, c, slot_ref[0])

    def run(stage):
        cur = slot_ref[s]
        nxt = next_ref[s]
        h = xs_ref[...].astype(BF16)
        if stage:
            start_chunk(nxt, 0)
        for c in range(n_chunks):
            cols = slice(c * MXU_N, (c + 1) * MXU_N)
            a = _dot(h, wb1[cur, :, cols])
            g_ref[:, cols] = (_silu(a) * _dot(h, wb3[cur, :, cols])).astype(BF16)
            if stage:
                if c + 1 < n_chunks:
                    start_chunk(nxt, c + 1)
                land_chunk(nxt, c, 1 - cur)
        res = _dot(g_ref[...], wb2[cur])
        row = tile_ref[s] * tm + lax.broadcasted_iota(jnp.int32, (tm, 1), 0)
        mine = jnp.logical_and(row >= lo_ref[s], row < hi_ref[s])

        @pl.when(first_ref[s] == 1)
        def _():
            y_ref[...] = jnp.where(mine, res, 0.0)

        @pl.when(first_ref[s] == 0)
        def _():
            y_ref[...] = jnp.where(mine, res, y_ref[...])

    @pl.when(jnp.logical_and(valid_ref[s] == 1, stage_ref[s] == 1))
    def _():
        run(True)

    @pl.when(jnp.logical_and(valid_ref[s] == 1, stage_ref[s] == 0))
    def _():
        run(False)


def _experts(xs, steps, w1, w3, w2, tm):
    n_rows, d = xs.shape
    d_ff = w1.shape[2]
    n_steps = steps[0].shape[0]
    tile_map = lambda s, tile, *_: (tile[s], 0)
    grid_spec = pltpu.PrefetchScalarGridSpec(
        num_scalar_prefetch=len(steps),
        grid=(n_steps,),
        in_specs=[
            pl.BlockSpec((tm, d), tile_map),
            pl.BlockSpec(memory_space=pl.ANY),
            pl.BlockSpec(memory_space=pl.ANY),
            pl.BlockSpec(memory_space=pl.ANY),
        ],
        out_specs=pl.BlockSpec((tm, d), tile_map),
        scratch_shapes=[
            pltpu.VMEM((tm, d_ff), BF16),
            pltpu.VMEM((2, d, d_ff), BF16),
            pltpu.VMEM((2, d, d_ff), BF16),
            pltpu.VMEM((2, d_ff, d), BF16),
            pltpu.VMEM((2, d, MXU_N), F32),
            pltpu.VMEM((2, d, MXU_N), F32),
            pltpu.VMEM((2, MXU_N, d), F32),
            pltpu.SemaphoreType.DMA((3, 2)),
        ],
    )
    return pl.pallas_call(
        functools.partial(_expert_kernel, tm),
        grid_spec=grid_spec,
        out_shape=jax.ShapeDtypeStruct((n_rows, d), F32),
        compiler_params=_params("arbitrary"),
        name="experts",
    )(*steps, xs, w1, w3, w2)


def _combine_kernel(tt, final, p0_ref, p1_ref, n0_ref, n1_ref, y_hbm, x_ref, mod_ref, meta_ref, fw_ref,
                    o_ref, buf, sem):
    step = pl.program_id(0) * pl.num_programs(1) + pl.program_id(1)
    n_steps = pl.num_programs(0) * pl.num_programs(1)
    slot = step % 2

    def gather(refs, t, s):
        _row_copy(y_hbm, refs[0][0, 0, t], buf.at[s, 0], t, sem.at[s]).start(priority=0)
        _row_copy(y_hbm, refs[1][0, 0, t], buf.at[s, 1], t, sem.at[s]).start(priority=1)

    def drain(s):
        for k in range(2):
            pltpu.make_async_copy(y_hbm.at[pl.ds(0, tt), :], buf.at[s, k], sem.at[s]).wait()

    @pl.when(step == 0)
    def _():
        def body(t, carry):
            gather((p0_ref, p1_ref), t, 0)
            return carry

        lax.fori_loop(0, tt, body, 0, unroll=8)

    drain(slot)
    for t in range(tt):
        gather((n0_ref, n1_ref), t, 1 - slot)
    p1 = meta_ref[0, :, META_P:META_P + 1]
    p2 = meta_ref[0, :, META_P + 1:META_P + 2]
    out = x_ref[0] + mod_ref[0, 5:6, :] * (p1 * buf[slot, 0] + p2 * buf[slot, 1])
    o_ref[0] = _final_norm(out, fw_ref[...]) if final else out

    @pl.when(step == n_steps - 1)
    def _():
        drain(1 - slot)


def _combine(y, pos, x, mod, meta, fw, tt, final):
    bsz, seq, d = x.shape
    per_b = seq // tt
    last = bsz * per_b - 1
    cur = pl.BlockSpec((1, 1, tt), lambda b, i: (b * per_b + i, 0, 0), memory_space=pltpu.SMEM)
    nxt = pl.BlockSpec((1, 1, tt), lambda b, i: (jnp.minimum(b * per_b + i + 1, last), 0, 0),
                       memory_space=pltpu.SMEM)
    p0 = pos[0].reshape(bsz * per_b, 1, tt)
    p1 = pos[1].reshape(bsz * per_b, 1, tt)
    return pl.pallas_call(
        functools.partial(_combine_kernel, tt, final),
        grid=(bsz, per_b),
        in_specs=[
            cur,
            cur,
            nxt,
            nxt,
            pl.BlockSpec(memory_space=pl.ANY),
            pl.BlockSpec((1, tt, d), lambda b, i: (b, i, 0)),
            pl.BlockSpec((1, 6, d), lambda b, i: (b, 0, 0)),
            pl.BlockSpec((1, tt, LANES), lambda b, i: (b, i, 0)),
            pl.BlockSpec((1, d), lambda b, i: (0, 0)),
        ],
        out_specs=pl.BlockSpec((1, tt, d), lambda b, i: (b, i, 0)),
        out_shape=jax.ShapeDtypeStruct((bsz, seq, d), F32),
        scratch_shapes=[pltpu.VMEM((2, 2, tt, d), F32), pltpu.SemaphoreType.DMA((2,))],
        compiler_params=_params("arbitrary", "arbitrary"),
        name="combine",
    )(p0, p1, p0, p1, y, x, mod, meta, fw.reshape(1, d))


def _expert_steps(counts, n_rows, tm):
    n_tiles = n_rows // tm
    n_steps = n_tiles + N_EXPERTS - 1
    ends = jnp.cumsum(counts)
    starts = ends - counts
    first_tile = starts // tm
    last_tile = jnp.maximum(ends - 1, 0) // tm
    tiles_e = jnp.where(counts > 0, last_tile - first_tile + 1, 0)
    step_end = jnp.cumsum(tiles_e)
    step_start = step_end - tiles_e
    total = step_end[-1]
    s = jnp.arange(n_steps, dtype=jnp.int32)
    sc = jnp.minimum(s, jnp.maximum(total - 1, 0))
    e_s = jnp.minimum(jnp.sum((sc[:, None] >= step_end[None, :]).astype(jnp.int32), axis=1), N_EXPERTS - 1)
    tile_s = jnp.clip(first_tile[e_s] + (sc - step_start[e_s]), 0, n_tiles - 1)
    valid = (s < total).astype(jnp.int32)
    prev_tile = jnp.concatenate([jnp.full((1,), -1, jnp.int32), tile_s[:-1]])
    first = jnp.logical_and(tile_s != prev_tile, valid == 1).astype(jnp.int32)
    has_rows = counts > 0
    order = jnp.cumsum(has_rows.astype(jnp.int32)) - 1
    ids = jnp.arange(N_EXPERTS, dtype=jnp.int32)
    later = jnp.logical_and(ids[None, :] > ids[:, None], has_rows[None, :])
    next_e = jnp.min(jnp.where(later, ids[None, :], N_EXPERTS), axis=1)
    prev_e = jnp.concatenate([jnp.full((1,), -1, jnp.int32), e_s[:-1]])
    stage = jnp.logical_and(jnp.logical_and(e_s != prev_e, valid == 1), next_e[e_s] < N_EXPERTS)
    as_i32 = lambda v: v.astype(jnp.int32)
    return (as_i32(tile_s), as_i32(e_s), as_i32(starts[e_s]), as_i32(ends[e_s]), first, valid,
            as_i32(order[e_s] % 2), as_i32(stage), as_i32(jnp.minimum(next_e[e_s], N_EXPERTS - 1)))


def _moe(x, mod, nw, fw, router_w, router_b, w1, w3, w2, tm_route, tt, tm_exp, final):
    bsz, seq, d = x.shape
    n_tok = bsz * seq
    h, meta, rt, cnt = _router(x, mod, nw, router_w, router_b, tm_route)
    counts = cnt[0, :N_EXPERTS].astype(jnp.int32)
    starts = jnp.cumsum(counts) - counts
    experts = rt[META_E:META_E + 2].astype(jnp.int32)
    pos = rt[META_R:META_R + 2].astype(jnp.int32)
    for e in range(N_EXPERTS):
        pos = pos + jnp.where(experts == e, starts[e], 0)
    xs = _dispatch(h.reshape(n_tok, d), pos, _pick(n_tok, 2 * tt))
    y = _experts(xs, _expert_steps(counts, 2 * n_tok, tm_exp), w1, w3, w2, tm_exp)
    return _combine(y, pos, x, mod, meta, fw, tt, final)


def _pick(n, pref):
    t = min(n, pref)
    assert n % t == 0, (n, pref)
    return t


def kernel(x, c, ada_w, ada_b, norm_mix_w, norm_ffn_w, w_in, conv_w, hgrn_lb_logits, hgrn_norm_w,
           sgu_norm_w, sgu_w, sgu_b, w_br_a, w_br_b, w_br_c, w_o, ffn_w1, ffn_w3, ffn_w2,
           moe_router_w, moe_router_b, moe_w1, moe_w3, moe_w2, final_norm_w):
    depth = ada_w.shape[0]
    bsz, seq, d = x.shape
    assert w_in.shape[-1] == OFF_GATE + 3 * d and seq % CHUNK == 0
    assert ffn_w1.shape[-1] % MXU_N == 0 and moe_w1.shape[-1] % MXU_N == 0

    tm_in = _pick(seq, 512)
    ts = _pick(seq, 512)
    tm_ffn = _pick(seq, 512)
    tm_route = _pick(seq, 512)
    tt = _pick(seq, 512)
    tm_exp = _pick(2 * bsz * seq, 512)

    mod_all = _adaln(c, ada_w, ada_b).reshape(depth, bsz, 6, d)
    tables = _level_tables()
    bf = lambda w: w.astype(BF16)

    for l in range(depth):
        mod = mod_all[l]
        ya, hq, gc, gates = _inproj(l, x, mod, norm_mix_w[l], bf(w_in[l]), conv_w[l], hgrn_lb_logits,
                                    sgu_norm_w[l], tm_in)
        x = _mixer(ya, hq, gc, gates, x, mod, hgrn_norm_w[l], sgu_w[l], sgu_b[l], tables,
                   bf(w_br_a[l]), bf(w_br_b[l]), bf(w_br_c[l]), bf(w_o[l]), ts)
        final = l == depth - 1
        j = l // 2
        if l % 2 == 0:
            x = _ffn(x, mod, norm_ffn_w[l], final_norm_w, bf(ffn_w1[j]), bf(ffn_w3[j]), bf(ffn_w2[j]),
                     tm_ffn, final)
        else:
            x = _moe(x, mod, norm_ffn_w[l], final_norm_w, moe_router_w[j], moe_router_b[j],
                     moe_w1[j], moe_w3[j], moe_w2[j], tm_route, tt, tm_exp, final)
    return x
```

```python
import functools

import numpy as np
import jax
import jax.numpy as jnp
from jax import lax
from jax.experimental import pallas as pl
from jax.experimental.pallas import tpu as pltpu

F32 = jnp.float32
BF16 = jnp.bfloat16

EPS = 1e-6
LB_FLOOR = 1e-30

D_A = 512
CONV_WIDTH = 3
H_B = 4
DK_B = 128
DV_B = 128
D_BK = H_B * DK_B
D_BV = H_B * DV_B
G_C = 4
GC_CH = 128
D_C = G_C * GC_CH
SGU_CHUNK = 128
N_EXPERTS = 8

OFF_BQ = 3 * D_A
OFF_BG = OFF_BQ + 2 * D_BK + D_BV
OFF_GATE = OFF_BG + D_BV + 2 * D_C

CHUNK = 128
N_LEVELS = 7
assert 1 << N_LEVELS == CHUNK and CHUNK == SGU_CHUNK

LANES = 128
MXU_N = 256
VMEM_LIMIT = 60 * 1024 * 1024
NEG_BIG = -1e30


def _sigmoid(v):
    return 0.5 + 0.5 * jnp.tanh(0.5 * v)


def _silu(v):
    return v * _sigmoid(v)


def _dot(a, b):
    return jnp.dot(a, b, preferred_element_type=F32)


def _dot_nt(a, b):
    return lax.dot_general(a, b, (((1,), (1,)), ((), ())), preferred_element_type=F32)


def _dot_tn(a, b):
    return lax.dot_general(a, b, (((0,), (0,)), ((), ())), preferred_element_type=F32)


def _norm_mod(x, nw, shift, scale):
    ms = jnp.mean(x * x, axis=-1, keepdims=True)
    return (x * lax.rsqrt(ms + EPS) * nw) * (1.0 + scale) + shift


def _params(*sem):
    return pltpu.CompilerParams(dimension_semantics=sem, vmem_limit_bytes=VMEM_LIMIT)


def _adaln_kernel(c_ref, w_ref, b_ref, o_ref):
    ca = _silu(c_ref[...]).astype(BF16)
    o_ref[0] = _dot(ca, w_ref[0].astype(BF16)) + b_ref[0]


def _adaln(c, ada_w, ada_b):
    depth, d, n = ada_w.shape
    bsz = c.shape[0]
    tn = n // 4
    return pl.pallas_call(
        _adaln_kernel,
        grid=(depth, n // tn),
        in_specs=[
            pl.BlockSpec((bsz, d), lambda l, j: (0, 0)),
            pl.BlockSpec((1, d, tn), lambda l, j: (l, 0, j)),
            pl.BlockSpec((1, 1, tn), lambda l, j: (l, 0, j)),
        ],
        out_specs=pl.BlockSpec((1, bsz, tn), lambda l, j: (l, 0, j)),
        out_shape=jax.ShapeDtypeStruct((depth, bsz, n), F32),
        compiler_params=_params("arbitrary", "arbitrary"),
        name="adaln",
    )(c, ada_w, ada_b.reshape(depth, 1, n))


HQ_Q, HQ_F, HQ_K, HQ_V = 0, D_BK, 2 * D_BK, 3 * D_BK
GC_G, GC_U, GC_V = 0, D_BV, D_BV + D_C
assert D_BK == D_BV == D_C == D_A


def _inproj_kernel(layer, x_ref, mod_ref, nw_ref, w_ref, convw_ref, lbl_ref, snw_ref,
                   ya_ref, hq_ref, gc_ref, gates_ref, zbuf):
    tm = x_ref.shape[1]
    d = x_ref.shape[2]

    @pl.when(pl.program_id(1) == 0)
    def _():
        zbuf[0:8, :] = jnp.zeros((8, D_A), F32)

    h = _norm_mod(x_ref[0], nw_ref[...], mod_ref[0, 0:1, :], mod_ref[0, 1:2, :]).astype(BF16)

    def group(off):
        return _dot(h, w_ref[:, off:off + D_A])

    z = group(D_A) * group(2 * D_A)
    zbuf[8:8 + tm, :] = z
    y = (convw_ref[0:1, :] * z + convw_ref[1:2, :] * zbuf[7:7 + tm, :]
         + convw_ref[2:3, :] * zbuf[6:6 + tm, :])
    ya_ref[0] = (group(0) * y).astype(BF16)
    zbuf[0:8, :] = zbuf[tm:tm + 8, :]

    lg = lbl_ref[...]
    pe = jnp.exp(lg - jnp.max(lg, axis=0, keepdims=True))
    p = pe / jnp.sum(pe, axis=0, keepdims=True)
    lb = jnp.clip(jnp.sum(p[0:layer + 1], axis=0, keepdims=True) - p[0:1], 0.0, 1.0)
    hq_ref[0, :, HQ_Q:HQ_Q + D_BK] = _silu(group(OFF_BQ)).astype(BF16)
    hq_ref[0, :, HQ_V:HQ_V + D_BV] = group(OFF_BQ + 2 * D_BK).astype(BF16)
    sig = _sigmoid(group(OFF_BQ + D_BK))
    hq_ref[0, :, HQ_F:HQ_F + D_BK] = jnp.log(jnp.maximum(lb, LB_FLOOR) + (1.0 - lb) * sig).astype(BF16)
    hq_ref[0, :, HQ_K:HQ_K + D_BK] = ((1.0 - lb) * (1.0 - sig)).astype(BF16)

    gc_ref[0, :, GC_G:GC_G + D_BV] = _silu(group(OFF_BG)).astype(BF16)
    gc_ref[0, :, GC_U:GC_U + D_C] = group(OFF_BG + D_BV).astype(BF16)
    cv = group(OFF_BG + D_BV + D_C)
    vn = cv * lax.rsqrt(jnp.mean(cv * cv, axis=-1, keepdims=True) + EPS) * snw_ref[...]
    gc_ref[0, :, GC_V:GC_V + D_C] = vn.astype(BF16)

    for g in range(3 * d // D_A):
        gates_ref[0, :, g * D_A:(g + 1) * D_A] = _sigmoid(group(OFF_GATE + g * D_A)).astype(BF16)


def _inproj(layer, x, mod, nw, w_bf16, conv_w, lb_logits, sgu_nw, tm):
    bsz, seq, d = x.shape
    n = w_bf16.shape[2]
    assert n == OFF_GATE + 3 * d and (3 * d) % D_A == 0
    row = lambda b, i: (b, i, 0)
    const2 = lambda b, i: (0, 0)
    out = lambda width: jax.ShapeDtypeStruct((bsz, seq, width), BF16)
    return pl.pallas_call(
        functools.partial(_inproj_kernel, layer),
        grid=(bsz, seq // tm),
        in_specs=[
            pl.BlockSpec((1, tm, d), row),
            pl.BlockSpec((1, 6, d), lambda b, i: (b, 0, 0)),
            pl.BlockSpec((1, d), const2),
            pl.BlockSpec((None, d, n), lambda b, i: (layer, 0, 0), pipeline_mode=pl.Buffered(1)),
            pl.BlockSpec(conv_w.shape, const2),
            pl.BlockSpec(lb_logits.shape, const2),
            pl.BlockSpec((1, D_C), const2),
        ],
        out_specs=[
            pl.BlockSpec((1, tm, D_A), row),
            pl.BlockSpec((1, tm, 4 * D_BK), row),
            pl.BlockSpec((1, tm, 3 * D_C), row),
            pl.BlockSpec((1, tm, 3 * d), row),
        ],
        out_shape=[out(D_A), out(4 * D_BK), out(3 * D_C), out(3 * d)],
        scratch_shapes=[pltpu.VMEM((tm + 8, D_A), F32)],
        compiler_params=_params("arbitrary", "arbitrary"),
        name="inproj",
    )(x, mod, nw.reshape(1, d), w_bf16, conv_w, lb_logits, sgu_nw.reshape(1, D_C))


def _level_tables():
    t = np.arange(CHUNK)
    sum_mat = np.zeros((N_LEVELS + 1, CHUNK, CHUNK), np.float32)
    sum_mat[0] = (t[None, :] <= t[:, None])
    up = np.zeros((N_LEVELS + 1, CHUNK, LANES), np.float32)
    low = np.zeros((N_LEVELS + 1, CHUNK, LANES), np.float32)
    pair = np.zeros((N_LEVELS + 1, CHUNK, CHUNK), np.float32)
    pair[0] = np.eye(CHUNK)
    for j in range(1, N_LEVELS + 1):
        blk, half = 1 << j, 1 << (j - 1)
        base = (t // blk) * blk
        m = base + half - 1
        is_up = (t - base) >= half
        u = t[None, :]
        upper_rows = (u > m[:, None]) & (u <= t[:, None])
        lower_rows = (u > t[:, None]) & (u <= m[:, None])
        sum_mat[j] = np.where(is_up[:, None], upper_rows, lower_rows)
        up[j] = is_up[:, None]
        low[j] = ~is_up[:, None]
        pair[j] = (t[:, None] // blk) == (t[None, :] // blk)
    pairc = np.zeros((N_LEVELS + 1, CHUNK // 2, CHUNK), np.float32)
    for j in range(1, N_LEVELS + 1):
        pairc[j] = pair[j][np.nonzero(((t >> (j - 1)) & 1) == 1)[0]]
    return (jnp.asarray(sum_mat.reshape((N_LEVELS + 1) * CHUNK, CHUNK), BF16),
            jnp.asarray(up), jnp.asarray(low), jnp.asarray(pair), jnp.asarray(pairc))


HEAD_GROUPS = ((0, 1), (2, 3))
SUBLANES = 8
MERGE_ROWS = 128


def _row_blocks(m):
    return [m[SUBLANES * r:SUBLANES * (r + 1), :] for r in range(m.shape[0] // SUBLANES)]


def _mixer_kernel(ts,
                  ya_ref, hq_ref, gc_ref, gates_ref, x_ref, mod_ref, hnw_ref, sguw_ref, sgub_ref,
                  summat_ref, up_ref, low_ref, pair_ref, pairc_ref, wa_ref, wb_ref, wc_ref, wo_ref,
                  o_ref,
                  st_ref, yb_ref, yc_ref, wsm_ref, br_ref, mg_ref, kt_ref):
    n_chunks = ts // CHUNK
    d = x_ref.shape[-1]

    @pl.when(pl.program_id(1) == 0)
    def _():
        st_ref[...] = jnp.zeros_like(st_ref)

    tril = (lax.broadcasted_iota(jnp.int32, (SGU_CHUNK, SGU_CHUNK), 0)
            >= lax.broadcasted_iota(jnp.int32, (SGU_CHUNK, SGU_CHUNK), 1))
    for g in range(G_C):
        wsm_ref[g] = jnp.where(tril, sguw_ref[g], 0.0).astype(BF16)

    def chunk_body(c, carry):
        r0 = pl.multiple_of(c * CHUNK, CHUNK)
        rows = pl.ds(r0, CHUNK)

        for g in range(G_C):
            cs = slice(g * GC_CH, (g + 1) * GC_CH)
            vn = gc_ref[0, rows, GC_V + g * GC_CH:GC_V + (g + 1) * GC_CH]
            u = gc_ref[0, rows, GC_U + g * GC_CH:GC_U + (g + 1) * GC_CH].astype(F32)
            yc_ref[rows, cs] = (u * (_dot(wsm_ref[g], vn) + sgub_ref[g])).astype(BF16)

        lv = _dot(summat_ref[...], hq_ref[0, rows, HQ_F:HQ_F + D_BK])

        for heads in HEAD_GROUPS:
            cs = {h: slice(h * DK_B, (h + 1) * DK_B) for h in heads}
            q_bf = {h: hq_ref[0, rows, HQ_Q + h * DK_B:HQ_Q + (h + 1) * DK_B] for h in heads}
            k_bf = {h: hq_ref[0, rows, HQ_K + h * DK_B:HQ_K + (h + 1) * DK_B] for h in heads}
            v_h = {h: hq_ref[0, rows, HQ_V + h * DV_B:HQ_V + (h + 1) * DV_B] for h in heads}
            q_h = {h: q_bf[h].astype(F32) for h in heads}
            k_h = {h: k_bf[h].astype(F32) for h in heads}
            scores = {}
            for h in heads:
                slot = h * (N_LEVELS + 1)
                kt_ref[slot] = k_bf[h].T
                scores[h] = _row_blocks(pair_ref[0] * _dot(q_bf[h], kt_ref[slot]))
            for j in range(1, N_LEVELS + 1):
                half = 1 << (j - 1)
                for h in heads:
                    e = jnp.exp(lv[j * CHUNK:(j + 1) * CHUNK, cs[h]])
                    slot = h * (N_LEVELS + 1) + j
                    if half < SUBLANES:
                        q_j = (q_h[h] * e * up_ref[j]).astype(BF16)
                        k_j = (k_h[h] * e * low_ref[j]).astype(BF16)
                        kt_ref[slot] = k_j.T
                        s_j = _row_blocks(pair_ref[j] * _dot(q_j, kt_ref[slot]))
                        scores[h] = [a + b for a, b in zip(scores[h], s_j)]
                    else:
                        zero = jnp.zeros((half, DK_B), F32)
                        k_parts, q_parts, up_blocks = [], [], []
                        for a in range(0, CHUNK, 2 * half):
                            k_parts += [k_h[h][a:a + half] * e[a:a + half], zero]
                            q_parts.append(q_h[h][a + half:a + 2 * half] * e[a + half:a + 2 * half])
                            up_blocks += list(range((a + half) // SUBLANES, (a + 2 * half) // SUBLANES))
                        kt_ref[slot] = jnp.concatenate(k_parts, axis=0).astype(BF16).T
                        q_c = jnp.concatenate(q_parts, axis=0).astype(BF16)
                        s_c = _row_blocks(pairc_ref[j] * _dot(q_c, kt_ref[slot]))
                        for r, blk in zip(up_blocks, s_c):
                            scores[h][r] = scores[h][r] + blk
            for h in heads:
                b_h = lv[0:CHUNK, cs[h]]
                st = st_ref[h]
                o = _dot_nt((q_h[h] * jnp.exp(b_h)).astype(BF16), st.astype(BF16))
                o = o + _dot(jnp.concatenate(scores[h], axis=0).astype(BF16), v_h[h])
                b_last = b_h[CHUNK - 1:CHUNK, :]
                k_dec = (k_h[h] * jnp.exp(b_last - b_h)).astype(BF16)
                st_ref[h] = st * jnp.exp(b_last) + _dot_tn(v_h[h], k_dec)
                on = o * lax.rsqrt(jnp.mean(o * o, axis=-1, keepdims=True) + EPS) * hnw_ref[:, cs[h]]
                sg = gc_ref[0, rows, GC_G + h * DV_B:GC_G + (h + 1) * DV_B].astype(F32)
                yb_ref[rows, cs[h]] = (on * sg).astype(BF16)
        return carry

    lax.fori_loop(0, n_chunks, chunk_body, 0)

    br_ref[0] = _dot(ya_ref[0], wa_ref[...])
    br_ref[1] = _dot(yb_ref[...], wb_ref[...])
    br_ref[2] = _dot(yc_ref[...], wc_ref[...])

    def merge_body(r, carry):
        rows = pl.ds(pl.multiple_of(r * MERGE_ROWS, MERGE_ROWS), MERGE_ROWS)
        merged = gates_ref[0, rows, 0:d].astype(F32) * br_ref[0, rows, :]
        merged = merged + gates_ref[0, rows, d:2 * d].astype(F32) * br_ref[1, rows, :]
        merged = merged + gates_ref[0, rows, 2 * d:3 * d].astype(F32) * br_ref[2, rows, :]
        mg_ref[rows, :] = merged.astype(BF16)
        return carry

    lax.fori_loop(0, ts // MERGE_ROWS, merge_body, 0)
    o_ref[0] = x_ref[0] + mod_ref[0, 2:3, :] * _dot(mg_ref[...], wo_ref[...])


def _mixer(layer, ya, hq, gc, gates, x, mod, hgrn_nw, sgu_w, sgu_b, tables, wa, wb, wc, wo, ts):
    bsz, seq, d = x.shape
    slab = lambda w: pl.BlockSpec((None,) + w.shape[1:], lambda b, i: (layer, 0, 0))
    sum_mat, up, low, pair, pairc = tables
    sgub = jnp.broadcast_to(sgu_b[:, :, None], (G_C, SGU_CHUNK, GC_CH))
    row = lambda b, i: (b, i, 0)
    const2 = lambda b, i: (0, 0)
    const3 = lambda b, i: (0, 0, 0)
    return pl.pallas_call(
        functools.partial(_mixer_kernel, ts),
        grid=(bsz, seq // ts),
        in_specs=[
            pl.BlockSpec((1, ts, ya.shape[-1]), row),
            pl.BlockSpec((1, ts, hq.shape[-1]), row),
            pl.BlockSpec((1, ts, gc.shape[-1]), row),
            pl.BlockSpec((1, ts, gates.shape[-1]), row),
            pl.BlockSpec((1, ts, d), row),
            pl.BlockSpec((1, 6, d), lambda b, i: (b, 0, 0)),
            pl.BlockSpec((1, D_BV), const2),
            pl.BlockSpec(sgu_w.shape, const3),
            pl.BlockSpec(sgub.shape, const3),
            pl.BlockSpec(sum_mat.shape, const2),
            pl.BlockSpec(up.shape, const3),
            pl.BlockSpec(low.shape, const3),
            pl.BlockSpec(pair.shape, const3),
            pl.BlockSpec(pairc.shape, const3),
            slab(wa),
            slab(wb),
            slab(wc),
            slab(wo),
        ],
        out_specs=pl.BlockSpec((1, ts, d), row),
        out_shape=jax.ShapeDtypeStruct((bsz, seq, d), F32),
        scratch_shapes=[
            pltpu.VMEM((H_B, DV_B, DK_B), F32),
            pltpu.VMEM((ts, D_BV), BF16),
            pltpu.VMEM((ts, D_C), BF16),
            pltpu.VMEM((G_C, SGU_CHUNK, SGU_CHUNK), BF16),
            pltpu.VMEM((3, ts, d), F32),
            pltpu.VMEM((ts, d), BF16),
            pltpu.VMEM((H_B * (N_LEVELS + 1), DK_B, CHUNK), BF16),
        ],
        compiler_params=_params("arbitrary", "arbitrary"),
        name="mixer",
    )(ya, hq, gc, gates, x, mod, hgrn_nw.reshape(1, D_BV), sgu_w, sgub,
      sum_mat, up, low, pair, pairc, wa, wb, wc, wo)


def _swiglu_tile(h, w1_at, w3_at, w2, g_ref):
    d_ff = g_ref.shape[1]
    for c0 in range(0, d_ff, MXU_N):
        a = _dot(h, w1_at(c0))
        g_ref[:, c0:c0 + MXU_N] = (_silu(a) * _dot(h, w3_at(c0))).astype(BF16)
    return _dot(g_ref[...], w2)


def _final_norm(v, fw):
    return v * lax.rsqrt(jnp.mean(v * v, axis=-1, keepdims=True) + EPS) * fw


def _ffn_kernel(final, x_ref, mod_ref, nw_ref, fw_ref, w1_ref, w3_ref, w2_ref, o_ref, g_ref):
    x = x_ref[0]
    h = _norm_mod(x, nw_ref[...], mod_ref[0, 3:4, :], mod_ref[0, 4:5, :]).astype(BF16)
    y = _swiglu_tile(h, lambda c0: w1_ref[:, c0:c0 + MXU_N], lambda c0: w3_ref[:, c0:c0 + MXU_N],
                     w2_ref[...], g_ref)
    out = x + mod_ref[0, 5:6, :] * y
    o_ref[0] = _final_norm(out, fw_ref[...]) if final else out


def _ffn(x, mod, nw, fw, w1, w3, w2, tm, final):
    bsz, seq, d = x.shape
    d_ff = w1.shape[1]
    const2 = lambda b, i: (0, 0)
    resident = dict(pipeline_mode=pl.Buffered(1))
    return pl.pallas_call(
        functools.partial(_ffn_kernel, final),
        grid=(bsz, seq // tm),
        in_specs=[
            pl.BlockSpec((1, tm, d), lambda b, i: (b, i, 0)),
            pl.BlockSpec((1, 6, d), lambda b, i: (b, 0, 0)),
            pl.BlockSpec((1, d), const2),
            pl.BlockSpec((1, d), const2),
            pl.BlockSpec((d, d_ff), const2, **resident),
            pl.BlockSpec((d, d_ff), const2, **resident),
            pl.BlockSpec((d_ff, d), const2, **resident),
        ],
        out_specs=pl.BlockSpec((1, tm, d), lambda b, i: (b, i, 0)),
        out_shape=jax.ShapeDtypeStruct((bsz, seq, d), F32),
        scratch_shapes=[pltpu.VMEM((tm, d_ff), BF16)],
        compiler_params=_params("arbitrary", "arbitrary"),
        name="ffn",
    )(x, mod, nw.reshape(1, d), fw.reshape(1, d), w1, w3, w2)


META_E, META_P, META_R = 0, 2, 4


def _router_kernel(x_ref, mod_ref, nw_ref, rw_ref, rb_ref, tri_ref, h_ref, meta_ref, rt_ref, cnt_ref, carry):
    first = jnp.logical_and(pl.program_id(0) == 0, pl.program_id(1) == 0)

    @pl.when(first)
    def _():
        carry[...] = jnp.zeros_like(carry)

    h = _norm_mod(x_ref[0], nw_ref[...], mod_ref[0, 3:4, :], mod_ref[0, 4:5, :])
    h_ref[0] = h
    h_hi = h.astype(BF16)
    h_lo = (h - h_hi.astype(F32)).astype(BF16)
    part = _dot(h_hi, rw_ref[...])
    logits = (part[:, 0:LANES] + part[:, LANES:2 * LANES]) + _dot(h_lo, rw_ref[:, 0:LANES]) + rb_ref[...]
    lane = lax.broadcasted_iota(jnp.int32, logits.shape, 1)
    m1 = jnp.max(logits, axis=-1, keepdims=True)
    i1 = jnp.min(jnp.where(logits == m1, lane, LANES), axis=-1, keepdims=True)
    rest = jnp.where(lane == i1, -jnp.inf, logits)
    m2 = jnp.max(rest, axis=-1, keepdims=True)
    i2 = jnp.min(jnp.where(rest == m2, lane, LANES), axis=-1, keepdims=True)
    e2 = jnp.exp(m2 - m1)
    p1 = 1.0 / (1.0 + e2)
    p2 = e2 / (1.0 + e2)
    sel1 = lane == i1
    sel2 = lane == i2
    onehot = jnp.where(jnp.logical_or(sel1, sel2), 1.0, 0.0)
    before = _dot(tri_ref[...], onehot.astype(BF16)) + carry[...]
    r1 = jnp.sum(jnp.where(sel1, before, 0.0), axis=-1, keepdims=True)
    r2 = jnp.sum(jnp.where(sel2, before, 0.0), axis=-1, keepdims=True)
    carry[...] = carry[...] + jnp.sum(onehot, axis=0, keepdims=True)
    cnt_ref[...] = carry[...]
    rec = jnp.where(lane == META_E, i1.astype(F32), 0.0)
    rec = jnp.where(lane == META_E + 1, i2.astype(F32), rec)
    rec = jnp.where(lane == META_P, p1, rec)
    rec = jnp.where(lane == META_P + 1, p2, rec)
    rec = jnp.where(lane == META_R, r1, rec)
    rec = jnp.where(lane == META_R + 1, r2, rec)
    meta_ref[0] = rec
    rt_ref[...] = rec.T[0:SUBLANES, :]


def _router(x, mod, nw, router_w, router_b, tm):
    bsz, seq, d = x.shape
    ne = router_w.shape[1]
    rw = jnp.pad(router_w.astype(F32), ((0, 0), (0, LANES - ne)))
    rw_hi = rw.astype(BF16)
    rw = jnp.concatenate([rw_hi, (rw - rw_hi.astype(F32)).astype(BF16)], axis=1)
    rb = jnp.pad(router_b.astype(F32).reshape(1, ne), ((0, 0), (0, LANES - ne)), constant_values=NEG_BIG)
    tri = jnp.asarray(np.tril(np.ones((tm, tm), np.float32), -1), BF16)
    const2 = lambda b, i: (0, 0)
    return pl.pallas_call(
        _router_kernel,
        grid=(bsz, seq // tm),
        in_specs=[
            pl.BlockSpec((1, tm, d), lambda b, i: (b, i, 0)),
            pl.BlockSpec((1, 6, d), lambda b, i: (b, 0, 0)),
            pl.BlockSpec((1, d), const2),
            pl.BlockSpec((d, 2 * LANES), const2),
            pl.BlockSpec((1, LANES), const2),
            pl.BlockSpec((tm, tm), const2),
        ],
        out_specs=[
            pl.BlockSpec((1, tm, d), lambda b, i: (b, i, 0)),
            pl.BlockSpec((1, tm, LANES), lambda b, i: (b, i, 0)),
            pl.BlockSpec((SUBLANES, tm), lambda b, i: (0, b * (seq // tm) + i)),
            pl.BlockSpec((1, LANES), const2),
        ],
        out_shape=[
            jax.ShapeDtypeStruct((bsz, seq, d), F32),
            jax.ShapeDtypeStruct((bsz, seq, LANES), F32),
            jax.ShapeDtypeStruct((SUBLANES, bsz * seq), F32),
            jax.ShapeDtypeStruct((1, LANES), F32),
        ],
        scratch_shapes=[pltpu.VMEM((1, LANES), F32)],
        compiler_params=_params("arbitrary", "arbitrary"),
        name="router",
    )(x, mod, nw.reshape(1, d), rw, rb, tri)


def _row_copy(src, src_row, dst, dst_row, sem):
    return pltpu.make_async_copy(src.at[pl.ds(src_row, 1), :], dst.at[pl.ds(dst_row, 1), :], sem)


def _dispatch_kernel(tt, p0_ref, p1_ref, h_ref, xs_hbm, sem):
    def issue(t, carry):
        _row_copy(h_ref, t, xs_hbm, p0_ref[0, 0, t], sem).start(priority=0)
        _row_copy(h_ref, t, xs_hbm, p1_ref[0, 0, t], sem).start(priority=1)
        return carry

    lax.fori_loop(0, tt, issue, 0, unroll=8)
    for _ in range(2):
        pltpu.make_async_copy(h_ref, xs_hbm.at[pl.ds(0, tt), :], sem).wait()


def _dispatch(h2d, pos, tt):
    n_tok, d = h2d.shape
    idx = pl.BlockSpec((1, 1, tt), lambda i: (i, 0, 0), memory_space=pltpu.SMEM)
    return pl.pallas_call(
        functools.partial(_dispatch_kernel, tt),
        grid=(n_tok // tt,),
        in_specs=[
            idx,
            idx,
            pl.BlockSpec((tt, d), lambda i: (i, 0)),
        ],
        out_specs=pl.BlockSpec(memory_space=pl.ANY),
        out_shape=jax.ShapeDtypeStruct((2 * n_tok, d), F32),
        scratch_shapes=[pltpu.SemaphoreType.DMA(())],
        compiler_params=_params("arbitrary"),
        name="dispatch",
    )(pos[0].reshape(n_tok // tt, 1, tt), pos[1].reshape(n_tok // tt, 1, tt), h2d)


def _expert_kernel(tm, tile_ref, exp_ref, lo_ref, hi_ref, first_ref, valid_ref, slot_ref, stage_ref, next_ref,
                   xs_ref, w1_hbm, w3_hbm, w2_hbm, y_ref, g_ref, wb1, wb3, wb2, stg1, stg3, stg2, sem):
    s = pl.program_id(0)
    d_ff = g_ref.shape[1]
    n_chunks = d_ff // MXU_N

    def chunk_copies(e, c, k):
        cols = pl.ds(c * MXU_N, MXU_N)
        return (pltpu.make_async_copy(w1_hbm.at[e, :, cols], stg1.at[k], sem.at[0, k]),
                pltpu.make_async_copy(w3_hbm.at[e, :, cols], stg3.at[k], sem.at[1, k]),
                pltpu.make_async_copy(w2_hbm.at[e, cols, :], stg2.at[k], sem.at[2, k]))

    def start_chunk(e, c):
        for cp in chunk_copies(e, c, c % 2):
            cp.start()

    def land_chunk(e, c, dst):
        for cp in chunk_copies(e, c, c % 2):
            cp.wait()
        cols = slice(c * MXU_N, (c + 1) * MXU_N)
        wb1[dst, :, cols] = stg1[c % 2].astype(BF16)
        wb3[dst, :, cols] = stg3[c % 2].astype(BF16)
        wb2[dst, cols, :] = stg2[c % 2].astype(BF16)

    @pl.when(s == 0)
    def _():
        e0 = exp_ref[0]
        start_chunk(e0, 0)
        for c in range(n_chunks):
            if c + 1 < n_chunks:
                start_chunk(e0, c + 1)
            land_chunk(e0, c, slot_ref[0])

    def run(stage):
        cur = slot_ref[s]
        nxt = next_ref[s]
        h = xs_ref[...].astype(BF16)
        if stage:
            start_chunk(nxt, 0)
        for c in range(n_chunks):
            cols = slice(c * MXU_N, (c + 1) * MXU_N)
            a = _dot(h, wb1[cur, :, cols])
            g_ref[:, cols] = (_silu(a) * _dot(h, wb3[cur, :, cols])).astype(BF16)
            if stage:
                if c + 1 < n_chunks:
                    start_chunk(nxt, c + 1)
                land_chunk(nxt, c, 1 - cur)
        res = _dot(g_ref[...], wb2[cur])
        row = tile_ref[s] * tm + lax.broadcasted_iota(jnp.int32, (tm, 1), 0)
        mine = jnp.logical_and(row >= lo_ref[s], row < hi_ref[s])

        @pl.when(first_ref[s] == 1)
        def _():
            y_ref[...] = jnp.where(mine, res, 0.0)

        @pl.when(first_ref[s] == 0)
        def _():
            y_ref[...] = jnp.where(mine, res, y_ref[...])

    @pl.when(jnp.logical_and(valid_ref[s] == 1, stage_ref[s] == 1))
    def _():
        run(True)

    @pl.when(jnp.logical_and(valid_ref[s] == 1, stage_ref[s] == 0))
    def _():
        run(False)


def _experts(xs, steps, w1, w3, w2, tm):
    n_rows, d = xs.shape
    d_ff = w1.shape[2]
    n_steps = steps[0].shape[0]
    tile_map = lambda s, tile, *_: (tile[s], 0)
    grid_spec = pltpu.PrefetchScalarGridSpec(
        num_scalar_prefetch=len(steps),
        grid=(n_steps,),
        in_specs=[
            pl.BlockSpec((tm, d), tile_map),
            pl.BlockSpec(memory_space=pl.ANY),
            pl.BlockSpec(memory_space=pl.ANY),
            pl.BlockSpec(memory_space=pl.ANY),
        ],
        out_specs=pl.BlockSpec((tm, d), tile_map),
        scratch_shapes=[
            pltpu.VMEM((tm, d_ff), BF16),
            pltpu.VMEM((2, d, d_ff), BF16),
            pltpu.VMEM((2, d, d_ff), BF16),
            pltpu.VMEM((2, d_ff, d), BF16),
            pltpu.VMEM((2, d, MXU_N), F32),
            pltpu.VMEM((2, d, MXU_N), F32),
            pltpu.VMEM((2, MXU_N, d), F32),
            pltpu.SemaphoreType.DMA((3, 2)),
        ],
    )
    return pl.pallas_call(
        functools.partial(_expert_kernel, tm),
        grid_spec=grid_spec,
        out_shape=jax.ShapeDtypeStruct((n_rows, d), F32),
        compiler_params=_params("arbitrary"),
        name="experts",
    )(*steps, xs, w1, w3, w2)


def _combine_kernel(tt, final, p0_ref, p1_ref, n0_ref, n1_ref, y_hbm, x_ref, mod_ref, meta_ref, fw_ref,
                    o_ref, buf, sem):
    step = pl.program_id(0) * pl.num_programs(1) + pl.program_id(1)
    n_steps = pl.num_programs(0) * pl.num_programs(1)
    slot = step % 2

    def gather(refs, t, s):
        _row_copy(y_hbm, refs[0][0, 0, t], buf.at[s, 0], t, sem.at[s]).start(priority=0)
        _row_copy(y_hbm, refs[1][0, 0, t], buf.at[s, 1], t, sem.at[s]).start(priority=1)

    def drain(s):
        for k in range(2):
            pltpu.make_async_copy(y_hbm.at[pl.ds(0, tt), :], buf.at[s, k], sem.at[s]).wait()

    @pl.when(step == 0)
    def _():
        def body(t, carry):
            gather((p0_ref, p1_ref), t, 0)
            return carry

        lax.fori_loop(0, tt, body, 0, unroll=8)

    drain(slot)
    for t in range(tt):
        gather((n0_ref, n1_ref), t, 1 - slot)
    p1 = meta_ref[0, :, META_P:META_P + 1]
    p2 = meta_ref[0, :, META_P + 1:META_P + 2]
    out = x_ref[0] + mod_ref[0, 5:6, :] * (p1 * buf[slot, 0] + p2 * buf[slot, 1])
    o_ref[0] = _final_norm(out, fw_ref[...]) if final else out

    @pl.when(step == n_steps - 1)
    def _():
        drain(1 - slot)


def _combine(y, pos, x, mod, meta, fw, tt, final):
    bsz, seq, d = x.shape
    per_b = seq // tt
    last = bsz * per_b - 1
    cur = pl.BlockSpec((1, 1, tt), lambda b, i: (b * per_b + i, 0, 0), memory_space=pltpu.SMEM)
    nxt = pl.BlockSpec((1, 1, tt), lambda b, i: (jnp.minimum(b * per_b + i + 1, last), 0, 0),
                       memory_space=pltpu.SMEM)
    p0 = pos[0].reshape(bsz * per_b, 1, tt)
    p1 = pos[1].reshape(bsz * per_b, 1, tt)
    return pl.pallas_call(
        functools.partial(_combine_kernel, tt, final),
        grid=(bsz, per_b),
        in_specs=[
            cur,
            cur,
            nxt,
            nxt,
            pl.BlockSpec(memory_space=pl.ANY),
            pl.BlockSpec((1, tt, d), lambda b, i: (b, i, 0)),
            pl.BlockSpec((1, 6, d), lambda b, i: (b, 0, 0)),
            pl.BlockSpec((1, tt, LANES), lambda b, i: (b, i, 0)),
            pl.BlockSpec((1, d), lambda b, i: (0, 0)),
        ],
        out_specs=pl.BlockSpec((1, tt, d), lambda b, i: (b, i, 0)),
        out_shape=jax.ShapeDtypeStruct((bsz, seq, d), F32),
        scratch_shapes=[pltpu.VMEM((2, 2, tt, d), F32), pltpu.SemaphoreType.DMA((2,))],
        compiler_params=_params("arbitrary", "arbitrary"),
        name="combine",
    )(p0, p1, p0, p1, y, x, mod, meta, fw.reshape(1, d))


def _expert_steps(counts, n_rows, tm):
    n_tiles = n_rows // tm
    n_steps = n_tiles + N_EXPERTS - 1
    ends = jnp.cumsum(counts)
    starts = ends - counts
    first_tile = starts // tm
    last_tile = jnp.maximum(ends - 1, 0) // tm
    tiles_e = jnp.where(counts > 0, last_tile - first_tile + 1, 0)
    step_end = jnp.cumsum(tiles_e)
    step_start = step_end - tiles_e
    total = step_end[-1]
    s = jnp.arange(n_steps, dtype=jnp.int32)
    sc = jnp.minimum(s, jnp.maximum(total - 1, 0))
    e_s = jnp.minimum(jnp.sum((sc[:, None] >= step_end[None, :]).astype(jnp.int32), axis=1), N_EXPERTS - 1)
    tile_s = jnp.clip(first_tile[e_s] + (sc - step_start[e_s]), 0, n_tiles - 1)
    valid = (s < total).astype(jnp.int32)
    prev_tile = jnp.concatenate([jnp.full((1,), -1, jnp.int32), tile_s[:-1]])
    first = jnp.logical_and(tile_s != prev_tile, valid == 1).astype(jnp.int32)
    has_rows = counts > 0
    order = jnp.cumsum(has_rows.astype(jnp.int32)) - 1
    ids = jnp.arange(N_EXPERTS, dtype=jnp.int32)
    later = jnp.logical_and(ids[None, :] > ids[:, None], has_rows[None, :])
    next_e = jnp.min(jnp.where(later, ids[None, :], N_EXPERTS), axis=1)
    prev_e = jnp.concatenate([jnp.full((1,), -1, jnp.int32), e_s[:-1]])
    stage = jnp.logical_and(jnp.logical_and(e_s != prev_e, valid == 1), next_e[e_s] < N_EXPERTS)
    as_i32 = lambda v: v.astype(jnp.int32)
    return (as_i32(tile_s), as_i32(e_s), as_i32(starts[e_s]), as_i32(ends[e_s]), first, valid,
            as_i32(order[e_s] % 2), as_i32(stage), as_i32(jnp.minimum(next_e[e_s], N_EXPERTS - 1)))


def _moe(x, mod, nw, fw, router_w, router_b, w1, w3, w2, tm_route, tt, tm_exp, final):
    bsz, seq, d = x.shape
    n_tok = bsz * seq
    h, meta, rt, cnt = _router(x, mod, nw, router_w, router_b, tm_route)
    counts = cnt[0, :N_EXPERTS].astype(jnp.int32)
    starts = jnp.cumsum(counts) - counts
    experts = rt[META_E:META_E + 2].astype(jnp.int32)
    pos = rt[META_R:META_R + 2].astype(jnp.int32)
    for e in range(N_EXPERTS):
        pos = pos + jnp.where(experts == e, starts[e], 0)
    xs = _dispatch(h.reshape(n_tok, d), pos, _pick(n_tok, 2 * tt))
    y = _experts(xs, _expert_steps(counts, 2 * n_tok, tm_exp), w1, w3, w2, tm_exp)
    return _combine(y, pos, x, mod, meta, fw, tt, final)


def _pick(n, pref):
    t = min(n, pref)
    assert n % t == 0, (n, pref)
    return t


def kernel(x, c, ada_w, ada_b, norm_mix_w, norm_ffn_w, w_in, conv_w, hgrn_lb_logits, hgrn_norm_w,
           sgu_norm_w, sgu_w, sgu_b, w_br_a, w_br_b, w_br_c, w_o, ffn_w1, ffn_w3, ffn_w2,
           moe_router_w, moe_router_b, moe_w1, moe_w3, moe_w2, final_norm_w):
    depth = ada_w.shape[0]
    bsz, seq, d = x.shape
    assert w_in.shape[-1] == OFF_GATE + 3 * d and seq % CHUNK == 0
    assert ffn_w1.shape[-1] % MXU_N == 0 and moe_w1.shape[-1] % MXU_N == 0

    tm_in = _pick(seq, 512)
    ts = _pick(seq, 512)
    tm_ffn = _pick(seq, 512)
    tm_route = _pick(seq, 512)
    tt = _pick(seq, 512)
    tm_exp = _pick(2 * bsz * seq, 512)

    mod_all = _adaln(c, ada_w, ada_b).reshape(depth, bsz, 6, d)
    tables = _level_tables()
    bf = lambda w: w.astype(BF16)
    w_in_bf, wa_bf, wb_bf, wc_bf, wo_bf = bf(w_in), bf(w_br_a), bf(w_br_b), bf(w_br_c), bf(w_o)

    for l in range(depth):
        mod = mod_all[l]
        ya, hq, gc, gates = _inproj(l, x, mod, norm_mix_w[l], w_in_bf, conv_w[l], hgrn_lb_logits,
                                    sgu_norm_w[l], tm_in)
        x = _mixer(l, ya, hq, gc, gates, x, mod, hgrn_norm_w[l], sgu_w[l], sgu_b[l], tables,
                   wa_bf, wb_bf, wc_bf, wo_bf, ts)
        final = l == depth - 1
        j = l // 2
        if l % 2 == 0:
            x = _ffn(x, mod, norm_ffn_w[l], final_norm_w, bf(ffn_w1[j]), bf(ffn_w3[j]), bf(ffn_w2[j]),
                     tm_ffn, final)
        else:
            x = _moe(x, mod, norm_ffn_w[l], final_norm_w, moe_router_w[j], moe_router_b[j],
                     moe_w1[j], moe_w3[j], moe_w2[j], tm_route, tt, tm_exp, final)
    return x
```

```python
import functools

import numpy as np
import jax
import jax.numpy as jnp
from jax import lax
from jax.experimental import pallas as pl
from jax.experimental.pallas import tpu as pltpu

F32 = jnp.float32
BF16 = jnp.bfloat16

EPS = 1e-6
LB_FLOOR = 1e-30

D_A = 512
CONV_WIDTH = 3
H_B = 4
DK_B = 128
DV_B = 128
D_BK = H_B * DK_B
D_BV = H_B * DV_B
G_C = 4
GC_CH = 128
D_C = G_C * GC_CH
SGU_CHUNK = 128
N_EXPERTS = 8

OFF_BQ = 3 * D_A
OFF_BG = OFF_BQ + 2 * D_BK + D_BV
OFF_GATE = OFF_BG + D_BV + 2 * D_C

CHUNK = 128
N_LEVELS = 7
assert 1 << N_LEVELS == CHUNK and CHUNK == SGU_CHUNK

LANES = 128
MXU_N = 256
VMEM_LIMIT = 60 * 1024 * 1024
NEG_BIG = -1e30


def _sigmoid(v):
    return 0.5 + 0.5 * jnp.tanh(0.5 * v)


def _silu(v):
    return v * _sigmoid(v)


def _dot(a, b):
    return jnp.dot(a, b, preferred_element_type=F32)


def _dot_nt(a, b):
    return lax.dot_general(a, b, (((1,), (1,)), ((), ())), preferred_element_type=F32)


def _dot_tn(a, b):
    return lax.dot_general(a, b, (((0,), (0,)), ((), ())), preferred_element_type=F32)


def _norm_mod(x, nw, shift, scale):
    ms = jnp.mean(x * x, axis=-1, keepdims=True)
    return (x * lax.rsqrt(ms + EPS) * nw) * (1.0 + scale) + shift


def _params(*sem):
    return pltpu.CompilerParams(dimension_semantics=sem, vmem_limit_bytes=VMEM_LIMIT)


def _adaln_kernel(c_ref, w_ref, b_ref, o_ref):
    ca = _silu(c_ref[...]).astype(BF16)
    o_ref[0] = _dot(ca, w_ref[0].astype(BF16)) + b_ref[0]


def _adaln(c, ada_w, ada_b):
    depth, d, n = ada_w.shape
    bsz = c.shape[0]
    tn = n // 4
    return pl.pallas_call(
        _adaln_kernel,
        grid=(depth, n // tn),
        in_specs=[
            pl.BlockSpec((bsz, d), lambda l, j: (0, 0)),
            pl.BlockSpec((1, d, tn), lambda l, j: (l, 0, j)),
            pl.BlockSpec((1, 1, tn), lambda l, j: (l, 0, j)),
        ],
        out_specs=pl.BlockSpec((1, bsz, tn), lambda l, j: (l, 0, j)),
        out_shape=jax.ShapeDtypeStruct((depth, bsz, n), F32),
        compiler_params=_params("arbitrary", "arbitrary"),
        name="adaln",
    )(c, ada_w, ada_b.reshape(depth, 1, n))


HQ_Q, HQ_F, HQ_K, HQ_V = 0, D_BK, 2 * D_BK, 3 * D_BK
GC_G, GC_U, GC_V = 0, D_BV, D_BV + D_C
assert D_BK == D_BV == D_C == D_A


def _inproj_kernel(layer, x_ref, mod_ref, nw_ref, w_ref, convw_ref, lbl_ref, snw_ref,
                   ya_ref, hq_ref, gc_ref, gates_ref, zbuf):
    tm = x_ref.shape[1]
    d = x_ref.shape[2]

    @pl.when(pl.program_id(1) == 0)
    def _():
        zbuf[0:8, :] = jnp.zeros((8, D_A), F32)

    h = _norm_mod(x_ref[0], nw_ref[...], mod_ref[0, 0:1, :], mod_ref[0, 1:2, :]).astype(BF16)

    def group(off):
        return _dot(h, w_ref[:, off:off + D_A])

    z = group(D_A) * group(2 * D_A)
    zbuf[8:8 + tm, :] = z
    y = (convw_ref[0:1, :] * z + convw_ref[1:2, :] * zbuf[7:7 + tm, :]
         + convw_ref[2:3, :] * zbuf[6:6 + tm, :])
    ya_ref[0] = (group(0) * y).astype(BF16)
    zbuf[0:8, :] = zbuf[tm:tm + 8, :]

    lg = lbl_ref[...]
    pe = jnp.exp(lg - jnp.max(lg, axis=0, keepdims=True))
    p = pe / jnp.sum(pe, axis=0, keepdims=True)
    lb = jnp.clip(jnp.sum(p[0:layer + 1], axis=0, keepdims=True) - p[0:1], 0.0, 1.0)
    hq_ref[0, :, HQ_Q:HQ_Q + D_BK] = _silu(group(OFF_BQ)).astype(BF16)
    hq_ref[0, :, HQ_V:HQ_V + D_BV] = group(OFF_BQ + 2 * D_BK).astype(BF16)
    sig = _sigmoid(group(OFF_BQ + D_BK))
    hq_ref[0, :, HQ_F:HQ_F + D_BK] = jnp.log(jnp.maximum(lb, LB_FLOOR) + (1.0 - lb) * sig).astype(BF16)
    hq_ref[0, :, HQ_K:HQ_K + D_BK] = ((1.0 - lb) * (1.0 - sig)).astype(BF16)

    gc_ref[0, :, GC_G:GC_G + D_BV] = _silu(group(OFF_BG)).astype(BF16)
    gc_ref[0, :, GC_U:GC_U + D_C] = group(OFF_BG + D_BV).astype(BF16)
    cv = group(OFF_BG + D_BV + D_C)
    vn = cv * lax.rsqrt(jnp.mean(cv * cv, axis=-1, keepdims=True) + EPS) * snw_ref[...]
    gc_ref[0, :, GC_V:GC_V + D_C] = vn.astype(BF16)

    for g in range(3 * d // D_A):
        gates_ref[0, :, g * D_A:(g + 1) * D_A] = _sigmoid(group(OFF_GATE + g * D_A)).astype(BF16)


def _inproj(layer, x, mod, nw, w_bf16, conv_w, lb_logits, sgu_nw, tm):
    bsz, seq, d = x.shape
    n = w_bf16.shape[2]
    assert n == OFF_GATE + 3 * d and (3 * d) % D_A == 0
    row = lambda b, i: (b, i, 0)
    const2 = lambda b, i: (0, 0)
    out = lambda width: jax.ShapeDtypeStruct((bsz, seq, width), BF16)
    return pl.pallas_call(
        functools.partial(_inproj_kernel, layer),
        grid=(bsz, seq // tm),
        in_specs=[
            pl.BlockSpec((1, tm, d), row),
            pl.BlockSpec((1, 6, d), lambda b, i: (b, 0, 0)),
            pl.BlockSpec((1, d), const2),
            pl.BlockSpec((None, d, n), lambda b, i: (layer, 0, 0), pipeline_mode=pl.Buffered(1)),
            pl.BlockSpec(conv_w.shape, const2),
            pl.BlockSpec(lb_logits.shape, const2),
            pl.BlockSpec((1, D_C), const2),
        ],
        out_specs=[
            pl.BlockSpec((1, tm, D_A), row),
            pl.BlockSpec((1, tm, 4 * D_BK), row),
            pl.BlockSpec((1, tm, 3 * D_C), row),
            pl.BlockSpec((1, tm, 3 * d), row),
        ],
        out_shape=[out(D_A), out(4 * D_BK), out(3 * D_C), out(3 * d)],
        scratch_shapes=[pltpu.VMEM((tm + 8, D_A), F32)],
        compiler_params=_params("arbitrary", "arbitrary"),
        name="inproj",
    )(x, mod, nw.reshape(1, d), w_bf16, conv_w, lb_logits, sgu_nw.reshape(1, D_C))


def _level_tables():
    t = np.arange(CHUNK)
    sum_mat = np.zeros((N_LEVELS + 1, CHUNK, CHUNK), np.float32)
    sum_mat[0] = (t[None, :] <= t[:, None])
    up = np.zeros((N_LEVELS + 1, CHUNK, LANES), np.float32)
    low = np.zeros((N_LEVELS + 1, CHUNK, LANES), np.float32)
    pair = np.zeros((N_LEVELS + 1, CHUNK, CHUNK), np.float32)
    pair[0] = np.eye(CHUNK)
    for j in range(1, N_LEVELS + 1):
        blk, half = 1 << j, 1 << (j - 1)
        base = (t // blk) * blk
        m = base + half - 1
        is_up = (t - base) >= half
        u = t[None, :]
        upper_rows = (u > m[:, None]) & (u <= t[:, None])
        lower_rows = (u > t[:, None]) & (u <= m[:, None])
        sum_mat[j] = np.where(is_up[:, None], upper_rows, lower_rows)
        up[j] = is_up[:, None]
        low[j] = ~is_up[:, None]
        pair[j] = (t[:, None] // blk) == (t[None, :] // blk)
    pairc = np.zeros((N_LEVELS + 1, CHUNK // 2, CHUNK), np.float32)
    for j in range(1, N_LEVELS + 1):
        pairc[j] = pair[j][np.nonzero(((t >> (j - 1)) & 1) == 1)[0]]
    return (jnp.asarray(sum_mat.reshape((N_LEVELS + 1) * CHUNK, CHUNK), BF16),
            jnp.asarray(up), jnp.asarray(low), jnp.asarray(pair), jnp.asarray(pairc))


HEAD_GROUPS = ((0, 1), (2, 3))
SUBLANES = 8
MERGE_ROWS = 128


def _row_blocks(m):
    return [m[SUBLANES * r:SUBLANES * (r + 1), :] for r in range(m.shape[0] // SUBLANES)]


def _mixer_kernel(ts,
                  ya_ref, hq_ref, gc_ref, gates_ref, x_ref, mod_ref, hnw_ref, sguw_ref, sgub_ref,
                  summat_ref, up_ref, low_ref, pair_ref, pairc_ref, wa_ref, wb_ref, wc_ref, wo_ref,
                  o_ref,
                  st_ref, yb_ref, yc_ref, wsm_ref, br_ref, mg_ref, kt_ref):
    n_chunks = ts // CHUNK
    d = x_ref.shape[-1]

    @pl.when(pl.program_id(1) == 0)
    def _():
        st_ref[...] = jnp.zeros_like(st_ref)

    tril = (lax.broadcasted_iota(jnp.int32, (SGU_CHUNK, SGU_CHUNK), 0)
            >= lax.broadcasted_iota(jnp.int32, (SGU_CHUNK, SGU_CHUNK), 1))
    for g in range(G_C):
        wsm_ref[g] = jnp.where(tril, sguw_ref[g], 0.0).astype(BF16)

    def chunk_body(c, carry):
        r0 = pl.multiple_of(c * CHUNK, CHUNK)
        rows = pl.ds(r0, CHUNK)

        for g in range(G_C):
            cs = slice(g * GC_CH, (g + 1) * GC_CH)
            vn = gc_ref[0, rows, GC_V + g * GC_CH:GC_V + (g + 1) * GC_CH]
            u = gc_ref[0, rows, GC_U + g * GC_CH:GC_U + (g + 1) * GC_CH].astype(F32)
            yc_ref[rows, cs] = (u * (_dot(wsm_ref[g], vn) + sgub_ref[g])).astype(BF16)

        lv = _dot(summat_ref[...], hq_ref[0, rows, HQ_F:HQ_F + D_BK])

        for heads in HEAD_GROUPS:
            cs = {h: slice(h * DK_B, (h + 1) * DK_B) for h in heads}
            q_bf = {h: hq_ref[0, rows, HQ_Q + h * DK_B:HQ_Q + (h + 1) * DK_B] for h in heads}
            k_bf = {h: hq_ref[0, rows, HQ_K + h * DK_B:HQ_K + (h + 1) * DK_B] for h in heads}
            v_h = {h: hq_ref[0, rows, HQ_V + h * DV_B:HQ_V + (h + 1) * DV_B] for h in heads}
            q_h = {h: q_bf[h].astype(F32) for h in heads}
            k_h = {h: k_bf[h].astype(F32) for h in heads}
            scores = {}
            for h in heads:
                slot = h * (N_LEVELS + 1)
                kt_ref[slot] = k_bf[h].T
                scores[h] = _row_blocks(pair_ref[0] * _dot(q_bf[h], kt_ref[slot]))
            for j in range(1, N_LEVELS + 1):
                half = 1 << (j - 1)
                for h in heads:
                    e = jnp.exp(lv[j * CHUNK:(j + 1) * CHUNK, cs[h]])
                    slot = h * (N_LEVELS + 1) + j
                    if half < SUBLANES:
                        q_j = (q_h[h] * e * up_ref[j]).astype(BF16)
                        k_j = (k_h[h] * e * low_ref[j]).astype(BF16)
                        kt_ref[slot] = k_j.T
                        s_j = _row_blocks(pair_ref[j] * _dot(q_j, kt_ref[slot]))
                        scores[h] = [a + b for a, b in zip(scores[h], s_j)]
                    else:
                        zero = jnp.zeros((half, DK_B), F32)
                        k_parts, q_parts, up_blocks = [], [], []
                        for a in range(0, CHUNK, 2 * half):
                            k_parts += [k_h[h][a:a + half] * e[a:a + half], zero]
                            q_parts.append(q_h[h][a + half:a + 2 * half] * e[a + half:a + 2 * half])
                            up_blocks += list(range((a + half) // SUBLANES, (a + 2 * half) // SUBLANES))
                        kt_ref[slot] = jnp.concatenate(k_parts, axis=0).astype(BF16).T
                        q_c = jnp.concatenate(q_parts, axis=0).astype(BF16)
                        s_c = _row_blocks(pairc_ref[j] * _dot(q_c, kt_ref[slot]))
                        for r, blk in zip(up_blocks, s_c):
                            scores[h][r] = scores[h][r] + blk
            for h in heads:
                b_h = lv[0:CHUNK, cs[h]]
                st = st_ref[h]
                o = _dot_nt((q_h[h] * jnp.exp(b_h)).astype(BF16), st.astype(BF16))
                o = o + _dot(jnp.concatenate(scores[h], axis=0).astype(BF16), v_h[h])
                b_last = b_h[CHUNK - 1:CHUNK, :]
                k_dec = (k_h[h] * jnp.exp(b_last - b_h)).astype(BF16)
                st_ref[h] = st * jnp.exp(b_last) + _dot_tn(v_h[h], k_dec)
                on = o * lax.rsqrt(jnp.mean(o * o, axis=-1, keepdims=True) + EPS) * hnw_ref[:, cs[h]]
                sg = gc_ref[0, rows, GC_G + h * DV_B:GC_G + (h + 1) * DV_B].astype(F32)
                yb_ref[rows, cs[h]] = (on * sg).astype(BF16)
        return carry

    lax.fori_loop(0, n_chunks, chunk_body, 0)

    br_ref[0] = _dot(ya_ref[0], wa_ref[...])
    br_ref[1] = _dot(yb_ref[...], wb_ref[...])
    br_ref[2] = _dot(yc_ref[...], wc_ref[...])

    def merge_body(r, carry):
        rows = pl.ds(pl.multiple_of(r * MERGE_ROWS, MERGE_ROWS), MERGE_ROWS)
        merged = gates_ref[0, rows, 0:d].astype(F32) * br_ref[0, rows, :]
        merged = merged + gates_ref[0, rows, d:2 * d].astype(F32) * br_ref[1, rows, :]
        merged = merged + gates_ref[0, rows, 2 * d:3 * d].astype(F32) * br_ref[2, rows, :]
        mg_ref[rows, :] = merged.astype(BF16)
        return carry

    lax.fori_loop(0, ts // MERGE_ROWS, merge_body, 0)
    o_ref[0] = x_ref[0] + mod_ref[0, 2:3, :] * _dot(mg_ref[...], wo_ref[...])


def _mixer(layer, ya, hq, gc, gates, x, mod, hgrn_nw, sgu_w, sgu_b, tables, wa, wb, wc, wo, ts):
    bsz, seq, d = x.shape
    slab = lambda w: pl.BlockSpec((None,) + w.shape[1:], lambda b, i: (layer, 0, 0))
    sum_mat, up, low, pair, pairc = tables
    sgub = jnp.broadcast_to(sgu_b[:, :, None], (G_C, SGU_CHUNK, GC_CH))
    row = lambda b, i: (b, i, 0)
    const2 = lambda b, i: (0, 0)
    const3 = lambda b, i: (0, 0, 0)
    return pl.pallas_call(
        functools.partial(_mixer_kernel, ts),
        grid=(bsz, seq // ts),
        in_specs=[
            pl.BlockSpec((1, ts, ya.shape[-1]), row),
            pl.BlockSpec((1, ts, hq.shape[-1]), row),
            pl.BlockSpec((1, ts, gc.shape[-1]), row),
            pl.BlockSpec((1, ts, gates.shape[-1]), row),
            pl.BlockSpec((1, ts, d), row),
            pl.BlockSpec((1, 6, d), lambda b, i: (b, 0, 0)),
            pl.BlockSpec((1, D_BV), const2),
            pl.BlockSpec(sgu_w.shape, const3),
            pl.BlockSpec(sgub.shape, const3),
            pl.BlockSpec(sum_mat.shape, const2),
            pl.BlockSpec(up.shape, const3),
            pl.BlockSpec(low.shape, const3),
            pl.BlockSpec(pair.shape, const3),
            pl.BlockSpec(pairc.shape, const3),
            slab(wa),
            slab(wb),
            slab(wc),
            slab(wo),
        ],
        out_specs=pl.BlockSpec((1, ts, d), row),
        out_shape=jax.ShapeDtypeStruct((bsz, seq, d), F32),
        scratch_shapes=[
            pltpu.VMEM((H_B, DV_B, DK_B), F32),
            pltpu.VMEM((ts, D_BV), BF16),
            pltpu.VMEM((ts, D_C), BF16),
            pltpu.VMEM((G_C, SGU_CHUNK, SGU_CHUNK), BF16),
            pltpu.VMEM((3, ts, d), F32),
            pltpu.VMEM((ts, d), BF16),
            pltpu.VMEM((H_B * (N_LEVELS + 1), DK_B, CHUNK), BF16),
        ],
        compiler_params=_params("arbitrary", "arbitrary"),
        name="mixer",
    )(ya, hq, gc, gates, x, mod, hgrn_nw.reshape(1, D_BV), sgu_w, sgub,
      sum_mat, up, low, pair, pairc, wa, wb, wc, wo)


def _swiglu_tile(h, w1_at, w3_at, w2, g_ref):
    d_ff = g_ref.shape[1]
    for c0 in range(0, d_ff, MXU_N):
        a = _dot(h, w1_at(c0))
        g_ref[:, c0:c0 + MXU_N] = (_silu(a) * _dot(h, w3_at(c0))).astype(BF16)
    return _dot(g_ref[...], w2)


def _final_norm(v, fw):
    return v * lax.rsqrt(jnp.mean(v * v, axis=-1, keepdims=True) + EPS) * fw


def _ffn_kernel(final, x_ref, mod_ref, nw_ref, fw_ref, w1_ref, w3_ref, w2_ref, o_ref, g_ref):
    x = x_ref[0]
    h = _norm_mod(x, nw_ref[...], mod_ref[0, 3:4, :], mod_ref[0, 4:5, :]).astype(BF16)
    y = _swiglu_tile(h, lambda c0: w1_ref[:, c0:c0 + MXU_N], lambda c0: w3_ref[:, c0:c0 + MXU_N],
                     w2_ref[...], g_ref)
    out = x + mod_ref[0, 5:6, :] * y
    o_ref[0] = _final_norm(out, fw_ref[...]) if final else out


def _ffn(x, mod, nw, fw, w1, w3, w2, tm, final):
    bsz, seq, d = x.shape
    d_ff = w1.shape[1]
    const2 = lambda b, i: (0, 0)
    resident = dict(pipeline_mode=pl.Buffered(1))
    return pl.pallas_call(
        functools.partial(_ffn_kernel, final),
        grid=(bsz, seq // tm),
        in_specs=[
            pl.BlockSpec((1, tm, d), lambda b, i: (b, i, 0)),
            pl.BlockSpec((1, 6, d), lambda b, i: (b, 0, 0)),
            pl.BlockSpec((1, d), const2),
            pl.BlockSpec((1, d), const2),
            pl.BlockSpec((d, d_ff), const2, **resident),
            pl.BlockSpec((d, d_ff), const2, **resident),
            pl.BlockSpec((d_ff, d), const2, **resident),
        ],
        out_specs=pl.BlockSpec((1, tm, d), lambda b, i: (b, i, 0)),
        out_shape=jax.ShapeDtypeStruct((bsz, seq, d), F32),
        scratch_shapes=[pltpu.VMEM((tm, d_ff), BF16)],
        compiler_params=_params("arbitrary", "arbitrary"),
        name="ffn",
    )(x, mod, nw.reshape(1, d), fw.reshape(1, d), w1, w3, w2)


META_E, META_P, META_R = 0, 2, 4


def _router_kernel(x_ref, mod_ref, nw_ref, rw_ref, rb_ref, tri_ref, h_ref, meta_ref, rt_ref, cnt_ref, carry):
    first = jnp.logical_and(pl.program_id(0) == 0, pl.program_id(1) == 0)

    @pl.when(first)
    def _():
        carry[...] = jnp.zeros_like(carry)

    h = _norm_mod(x_ref[0], nw_ref[...], mod_ref[0, 3:4, :], mod_ref[0, 4:5, :])
    h_ref[0] = h
    h_hi = h.astype(BF16)
    h_lo = (h - h_hi.astype(F32)).astype(BF16)
    part = _dot(h_hi, rw_ref[...])
    logits = (part[:, 0:LANES] + part[:, LANES:2 * LANES]) + _dot(h_lo, rw_ref[:, 0:LANES]) + rb_ref[...]
    lane = lax.broadcasted_iota(jnp.int32, logits.shape, 1)
    m1 = jnp.max(logits, axis=-1, keepdims=True)
    i1 = jnp.min(jnp.where(logits == m1, lane, LANES), axis=-1, keepdims=True)
    rest = jnp.where(lane == i1, -jnp.inf, logits)
    m2 = jnp.max(rest, axis=-1, keepdims=True)
    i2 = jnp.min(jnp.where(rest == m2, lane, LANES), axis=-1, keepdims=True)
    e2 = jnp.exp(m2 - m1)
    p1 = 1.0 / (1.0 + e2)
    p2 = e2 / (1.0 + e2)
    sel1 = lane == i1
    sel2 = lane == i2
    onehot = jnp.where(jnp.logical_or(sel1, sel2), 1.0, 0.0)
    before = _dot(tri_ref[...], onehot.astype(BF16)) + carry[...]
    r1 = jnp.sum(jnp.where(sel1, before, 0.0), axis=-1, keepdims=True)
    r2 = jnp.sum(jnp.where(sel2, before, 0.0), axis=-1, keepdims=True)
    carry[...] = carry[...] + jnp.sum(onehot, axis=0, keepdims=True)
    cnt_ref[...] = carry[...]
    rec = jnp.where(lane == META_E, i1.astype(F32), 0.0)
    rec = jnp.where(lane == META_E + 1, i2.astype(F32), rec)
    rec = jnp.where(lane == META_P, p1, rec)
    rec = jnp.where(lane == META_P + 1, p2, rec)
    rec = jnp.where(lane == META_R, r1, rec)
    rec = jnp.where(lane == META_R + 1, r2, rec)
    meta_ref[0] = rec
    rt_ref[...] = rec.T[0:SUBLANES, :]


def _router(x, mod, nw, router_w, router_b, tm):
    bsz, seq, d = x.shape
    ne = router_w.shape[1]
    rw = jnp.pad(router_w.astype(F32), ((0, 0), (0, LANES - ne)))
    rw_hi = rw.astype(BF16)
    rw = jnp.concatenate([rw_hi, (rw - rw_hi.astype(F32)).astype(BF16)], axis=1)
    rb = jnp.pad(router_b.astype(F32).reshape(1, ne), ((0, 0), (0, LANES - ne)), constant_values=NEG_BIG)
    tri = jnp.asarray(np.tril(np.ones((tm, tm), np.float32), -1), BF16)
    const2 = lambda b, i: (0, 0)
    return pl.pallas_call(
        _router_kernel,
        grid=(bsz, seq // tm),
        in_specs=[
            pl.BlockSpec((1, tm, d), lambda b, i: (b, i, 0)),
            pl.BlockSpec((1, 6, d), lambda b, i: (b, 0, 0)),
            pl.BlockSpec((1, d), const2),
            pl.BlockSpec((d, 2 * LANES), const2),
            pl.BlockSpec((1, LANES), const2),
            pl.BlockSpec((tm, tm), const2),
        ],
        out_specs=[
            pl.BlockSpec((1, tm, d), lambda b, i: (b, i, 0)),
            pl.BlockSpec((1, tm, LANES), lambda b, i: (b, i, 0)),
            pl.BlockSpec((SUBLANES, tm), lambda b, i: (0, b * (seq // tm) + i)),
            pl.BlockSpec((1, LANES), const2),
        ],
        out_shape=[
            jax.ShapeDtypeStruct((bsz, seq, d), F32),
            jax.ShapeDtypeStruct((bsz, seq, LANES), F32),
            jax.ShapeDtypeStruct((SUBLANES, bsz * seq), F32),
            jax.ShapeDtypeStruct((1, LANES), F32),
        ],
        scratch_shapes=[pltpu.VMEM((1, LANES), F32)],
        compiler_params=_params("arbitrary", "arbitrary"),
        name="router",
    )(x, mod, nw.reshape(1, d), rw, rb, tri)


def _row_copy(src, src_row, dst, dst_row, sem):
    return pltpu.make_async_copy(src.at[pl.ds(src_row, 1), :], dst.at[pl.ds(dst_row, 1), :], sem)


def _dispatch_kernel(tt, p0_ref, p1_ref, h_ref, xs_hbm, sem):
    def issue(t, carry):
        _row_copy(h_ref, t, xs_hbm, p0_ref[0, 0, t], sem).start(priority=0)
        _row_copy(h_ref, t, xs_hbm, p1_ref[0, 0, t], sem).start(priority=1)
        return carry

    lax.fori_loop(0, tt, issue, 0, unroll=8)
    for _ in range(2):
        pltpu.make_async_copy(h_ref, xs_hbm.at[pl.ds(0, tt), :], sem).wait()


def _dispatch(h2d, pos, tt):
    n_tok, d = h2d.shape
    idx = pl.BlockSpec((1, 1, tt), lambda i: (i, 0, 0), memory_space=pltpu.SMEM)
    return pl.pallas_call(
        functools.partial(_dispatch_kernel, tt),
        grid=(n_tok // tt,),
        in_specs=[
            idx,
            idx,
            pl.BlockSpec((tt, d), lambda i: (i, 0)),
        ],
        out_specs=pl.BlockSpec(memory_space=pl.ANY),
        out_shape=jax.ShapeDtypeStruct((2 * n_tok, d), F32),
        scratch_shapes=[pltpu.SemaphoreType.DMA(())],
        compiler_params=_params("arbitrary"),
        name="dispatch",
    )(pos[0].reshape(n_tok // tt, 1, tt), pos[1].reshape(n_tok // tt, 1, tt), h2d)


def _expert_kernel(tm, tile_ref, exp_ref, lo_ref, hi_ref, first_ref, valid_ref, slot_ref, stage_ref, next_ref,
                   xs_ref, w1_hbm, w3_hbm, w2_hbm, y_ref, g_ref, wb1, wb3, wb2, stg1, stg3, stg2, sem):
    s = pl.program_id(0)
    d_ff = g_ref.shape[1]
    n_chunks = d_ff // MXU_N

    def chunk_copies(e, c, k):
        cols = pl.ds(c * MXU_N, MXU_N)
        return (pltpu.make_async_copy(w1_hbm.at[e, :, cols], stg1.at[k], sem.at[0, k]),
                pltpu.make_async_copy(w3_hbm.at[e, :, cols], stg3.at[k], sem.at[1, k]),
                pltpu.make_async_copy(w2_hbm.at[e, cols, :], stg2.at[k], sem.at[2, k]))

    def start_chunk(e, c):
        for cp in chunk_copies(e, c, c % 2):
            cp.start()

    def land_chunk(e, c, dst):
        for cp in chunk_copies(e, c, c % 2):
            cp.wait()
        cols = slice(c * MXU_N, (c + 1) * MXU_N)
        wb1[dst, :, cols] = stg1[c % 2].astype(BF16)
        wb3[dst, :, cols] = stg3[c % 2].astype(BF16)
        wb2[dst, cols, :] = stg2[c % 2].astype(BF16)

    @pl.when(s == 0)
    def _():
        e0 = exp_ref[0]
        start_chunk(e0, 0)
        for c in range(n_chunks):
            if c + 1 < n_chunks:
                start_chunk(e0, c + 1)
            land_chunk(e0, c, slot_ref[0])

    def run(stage):
        cur = slot_ref[s]
        nxt = next_ref[s]
        h = xs_ref[...].astype(BF16)
        if stage:
            start_chunk(nxt, 0)
        for c in range(n_chunks):
            cols = slice(c * MXU_N, (c + 1) * MXU_N)
            a = _dot(h, wb1[cur, :, cols])
            g_ref[:, cols] = (_silu(a) * _dot(h, wb3[cur, :, cols])).astype(BF16)
            if stage:
                if c + 1 < n_chunks:
                    start_chunk(nxt, c + 1)
                land_chunk(nxt, c, 1 - cur)
        res = _dot(g_ref[...], wb2[cur])
        row = tile_ref[s] * tm + lax.broadcasted_iota(jnp.int32, (tm, 1), 0)
        mine = jnp.logical_and(row >= lo_ref[s], row < hi_ref[s])

        @pl.when(first_ref[s] == 1)
        def _():
            y_ref[...] = jnp.where(mine, res, 0.0)

        @pl.when(first_ref[s] == 0)
        def _():
            y_ref[...] = jnp.where(mine, res, y_ref[...])

    @pl.when(jnp.logical_and(valid_ref[s] == 1, stage_ref[s] == 1))
    def _():
        run(True)

    @pl.when(jnp.logical_and(valid_ref[s] == 1, stage_ref[s] == 0))
    def _():
        run(False)


def _experts(xs, steps, w1, w3, w2, tm):
    n_rows, d = xs.shape
    d_ff = w1.shape[2]
    n_steps = steps[0].shape[0]
    tile_map = lambda s, tile, *_: (tile[s], 0)
    grid_spec = pltpu.PrefetchScalarGridSpec(
        num_scalar_prefetch=len(steps),
        grid=(n_steps,),
        in_specs=[
            pl.BlockSpec((tm, d), tile_map),
            pl.BlockSpec(memory_space=pl.ANY),
            pl.BlockSpec(memory_space=pl.ANY),
            pl.BlockSpec(memory_space=pl.ANY),
        ],
        out_specs=pl.BlockSpec((tm, d), tile_map),
        scratch_shapes=[
            pltpu.VMEM((tm, d_ff), BF16),
            pltpu.VMEM((2, d, d_ff), BF16),
            pltpu.VMEM((2, d, d_ff), BF16),
            pltpu.VMEM((2, d_ff, d), BF16),
            pltpu.VMEM((2, d, MXU_N), F32),
            pltpu.VMEM((2, d, MXU_N), F32),
            pltpu.VMEM((2, MXU_N, d), F32),
            pltpu.SemaphoreType.DMA((3, 2)),
        ],
    )
    return pl.pallas_call(
        functools.partial(_expert_kernel, tm),
        grid_spec=grid_spec,
        out_shape=jax.ShapeDtypeStruct((n_rows, d), F32),
        compiler_params=_params("arbitrary"),
        name="experts",
    )(*steps, xs, w1, w3, w2)


def _combine_kernel(tt, final, p0_ref, p1_ref, n0_ref, n1_ref, y_hbm, x_ref, mod_ref, meta_ref, fw_ref,
                    o_ref, buf, sem):
    step = pl.program_id(0) * pl.num_programs(1) + pl.program_id(1)
    n_steps = pl.num_programs(0) * pl.num_programs(1)
    slot = step % 2

    def gather(refs, t, s):
        _row_copy(y_hbm, refs[0][0, 0, t], buf.at[s, 0], t, sem.at[s]).start(priority=0)
        _row_copy(y_hbm, refs[1][0, 0, t], buf.at[s, 1], t, sem.at[s]).start(priority=1)

    def drain(s):
        for k in range(2):
            pltpu.make_async_copy(y_hbm.at[pl.ds(0, tt), :], buf.at[s, k], sem.at[s]).wait()

    @pl.when(step == 0)
    def _():
        def body(t, carry):
            gather((p0_ref, p1_ref), t, 0)
            return carry

        lax.fori_loop(0, tt, body, 0, unroll=8)

    drain(slot)
    for t in range(tt):
        gather((n0_ref, n1_ref), t, 1 - slot)
    p1 = meta_ref[0, :, META_P:META_P + 1]
    p2 = meta_ref[0, :, META_P + 1:META_P + 2]
    out = x_ref[0] + mod_ref[0, 5:6, :] * (p1 * buf[slot, 0] + p2 * buf[slot, 1])
    o_ref[0] = _final_norm(out, fw_ref[...]) if final else out

    @pl.when(step == n_steps - 1)
    def _():
        drain(1 - slot)


def _combine(y, pos, x, mod, meta, fw, tt, final):
    bsz, seq, d = x.shape
    per_b = seq // tt
    last = bsz * per_b - 1
    cur = pl.BlockSpec((1, 1, tt), lambda b, i: (b * per_b + i, 0, 0), memory_space=pltpu.SMEM)
    nxt = pl.BlockSpec((1, 1, tt), lambda b, i: (jnp.minimum(b * per_b + i + 1, last), 0, 0),
                       memory_space=pltpu.SMEM)
    p0 = pos[0].reshape(bsz * per_b, 1, tt)
    p1 = pos[1].reshape(bsz * per_b, 1, tt)
    return pl.pallas_call(
        functools.partial(_combine_kernel, tt, final),
        grid=(bsz, per_b),
        in_specs=[
            cur,
            cur,
            nxt,
            nxt,
            pl.BlockSpec(memory_space=pl.ANY),
            pl.BlockSpec((1, tt, d), lambda b, i: (b, i, 0)),
            pl.BlockSpec((1, 6, d), lambda b, i: (b, 0, 0)),
            pl.BlockSpec((1, tt, LANES), lambda b, i: (b, i, 0)),
            pl.BlockSpec((1, d), lambda b, i: (0, 0)),
        ],
        out_specs=pl.BlockSpec((1, tt, d), lambda b, i: (b, i, 0)),
        out_shape=jax.ShapeDtypeStruct((bsz, seq, d), F32),
        scratch_shapes=[pltpu.VMEM((2, 2, tt, d), F32), pltpu.SemaphoreType.DMA((2,))],
        compiler_params=_params("arbitrary", "arbitrary"),
        name="combine",
    )(p0, p1, p0, p1, y, x, mod, meta, fw.reshape(1, d))


def _plan_kernel(tm, cnt_ref, rt_ref,
                 tile_o, exp_o, lo_o, hi_o, first_o, valid_o, slot_o, stage_o, next_o, pos_o):
    n_steps = tile_o.shape[0]
    shift = tm.bit_length() - 1
    i32 = jnp.int32
    counts = [cnt_ref[0, e] for e in range(N_EXPERTS)]
    starts = []
    acc = i32(0)
    for e in range(N_EXPERTS):
        starts.append(acc)
        acc = acc + counts[e]
    nxt = [None] * N_EXPERTS
    cur = i32(N_EXPERTS)
    for e in reversed(range(N_EXPERTS)):
        nxt[e] = cur
        cur = jnp.where(counts[e] > 0, i32(e), cur)

    for k in range(2):
        ex = rt_ref[META_E + k:META_E + k + 1, :]
        p = rt_ref[META_R + k:META_R + k + 1, :]
        for e in range(N_EXPERTS):
            p = p + jnp.where(ex == float(e), starts[e].astype(F32), 0.0)
        pos_o[k:k + 1, :] = p.astype(i32)

    for ref in (tile_o, exp_o, lo_o, hi_o, slot_o, next_o):
        ref[0] = i32(0)
    step = i32(0)
    order = i32(0)
    prev_tile = i32(-1)
    for e in range(N_EXPERTS):
        lo = starts[e]
        hi = lo + counts[e]
        t0 = lax.shift_right_logical(lo, shift)
        t1 = lax.shift_right_logical(jnp.maximum(hi - 1, 0), shift)
        n_e = jnp.where(counts[e] > 0, t1 - t0 + 1, 0)
        slot = order % 2
        has_next = (nxt[e] < N_EXPERTS).astype(i32)
        nxt_e = jnp.minimum(nxt[e], N_EXPERTS - 1)

        def visit(i, carry, e=e, lo=lo, hi=hi, t0=t0, slot=slot, has_next=has_next, nxt_e=nxt_e):
            step, prev_tile = carry
            tile = t0 + i
            tile_o[step] = tile
            exp_o[step] = i32(e)
            lo_o[step] = lo
            hi_o[step] = hi
            first_o[step] = (tile != prev_tile).astype(i32)
            valid_o[step] = i32(1)
            slot_o[step] = slot
            stage_o[step] = jnp.where(i == 0, has_next, 0)
            next_o[step] = nxt_e
            return step + 1, tile

        step, prev_tile = lax.fori_loop(0, n_e, visit, (step, prev_tile))
        order = order + (counts[e] > 0).astype(i32)

    last = jnp.maximum(step - 1, 0)

    def pad(i, carry):
        tile_o[i] = tile_o[last]
        exp_o[i] = exp_o[last]
        lo_o[i] = lo_o[last]
        hi_o[i] = hi_o[last]
        slot_o[i] = slot_o[last]
        next_o[i] = next_o[last]
        first_o[i] = i32(0)
        valid_o[i] = i32(0)
        stage_o[i] = i32(0)
        return carry

    lax.fori_loop(jnp.maximum(step, 1), n_steps, pad, 0)

    @pl.when(step == 0)
    def _():
        first_o[0] = i32(0)
        valid_o[0] = i32(0)
        stage_o[0] = i32(0)


def _plan(cnt, rt, n_rows, tm):
    assert tm & (tm - 1) == 0
    n_tok = rt.shape[1]
    n_steps = n_rows // tm + N_EXPERTS - 1
    smem = pl.BlockSpec(memory_space=pltpu.SMEM)
    vmem = pl.BlockSpec(memory_space=pltpu.VMEM)
    step_arr = jax.ShapeDtypeStruct((n_steps,), jnp.int32)
    out = pl.pallas_call(
        functools.partial(_plan_kernel, tm),
        in_specs=[smem, vmem],
        out_specs=[smem] * 9 + [vmem],
        out_shape=[step_arr] * 9 + [jax.ShapeDtypeStruct((2, n_tok), jnp.int32)],
        name="plan",
    )(cnt.astype(jnp.int32), rt)
    return tuple(out[:9]), out[9]


def _moe(x, mod, nw, fw, router_w, router_b, w1, w3, w2, tm_route, tt, tm_exp, final):
    bsz, seq, d = x.shape
    n_tok = bsz * seq
    h, meta, rt, cnt = _router(x, mod, nw, router_w, router_b, tm_route)
    steps, pos = _plan(cnt, rt, 2 * n_tok, tm_exp)
    xs = _dispatch(h.reshape(n_tok, d), pos, _pick(n_tok, 2 * tt))
    y = _experts(xs, steps, w1, w3, w2, tm_exp)
    return _combine(y, pos, x, mod, meta, fw, tt, final)


def _pick(n, pref):
    t = min(n, pref)
    assert n % t == 0, (n, pref)
    return t


def kernel(x, c, ada_w, ada_b, norm_mix_w, norm_ffn_w, w_in, conv_w, hgrn_lb_logits, hgrn_norm_w,
           sgu_norm_w, sgu_w, sgu_b, w_br_a, w_br_b, w_br_c, w_o, ffn_w1, ffn_w3, ffn_w2,
           moe_router_w, moe_router_b, moe_w1, moe_w3, moe_w2, final_norm_w):
    depth = ada_w.shape[0]
    bsz, seq, d = x.shape
    assert w_in.shape[-1] == OFF_GATE + 3 * d and seq % CHUNK == 0
    assert ffn_w1.shape[-1] % MXU_N == 0 and moe_w1.shape[-1] % MXU_N == 0

    tm_in = _pick(seq, 512)
    ts = _pick(seq, 512)
    tm_ffn = _pick(seq, 1024)
    tm_route = _pick(seq, 512)
    tt = _pick(seq, 512)
    tm_exp = _pick(2 * bsz * seq, 512)

    mod_all = _adaln(c, ada_w, ada_b).reshape(depth, bsz, 6, d)
    tables = _level_tables()
    bf = lambda w: w.astype(BF16)
    w_in_bf, wa_bf, wb_bf, wc_bf, wo_bf = bf(w_in), bf(w_br_a), bf(w_br_b), bf(w_br_c), bf(w_o)

    for l in range(depth):
        mod = mod_all[l]
        ya, hq, gc, gates = _inproj(l, x, mod, norm_mix_w[l], w_in_bf, conv_w[l], hgrn_lb_logits,
                                    sgu_norm_w[l], tm_in)
        x = _mixer(l, ya, hq, gc, gates, x, mod, hgrn_norm_w[l], sgu_w[l], sgu_b[l], tables,
                   wa_bf, wb_bf, wc_bf, wo_bf, ts)
        final = l == depth - 1
        j = l // 2
        if l % 2 == 0:
            x = _ffn(x, mod, norm_ffn_w[l], final_norm_w, bf(ffn_w1[j]), bf(ffn_w3[j]), bf(ffn_w2[j]),
                     tm_ffn, final)
        else:
            x = _moe(x, mod, norm_ffn_w[l], final_norm_w, moe_router_w[j], moe_router_b[j],
                     moe_w1[j], moe_w3[j], moe_w2[j], tm_route, tt, tm_exp, final)
    return x
```

```python
import functools

import numpy as np
import jax
import jax.numpy as jnp
from jax import lax
from jax.experimental import pallas as pl
from jax.experimental.pallas import tpu as pltpu

F32 = jnp.float32
BF16 = jnp.bfloat16

EPS = 1e-6
LB_FLOOR = 1e-30

D_A = 512
CONV_WIDTH = 3
H_B = 4
DK_B = 128
DV_B = 128
D_BK = H_B * DK_B
D_BV = H_B * DV_B
G_C = 4
GC_CH = 128
D_C = G_C * GC_CH
SGU_CHUNK = 128
N_EXPERTS = 8

OFF_BQ = 3 * D_A
OFF_BG = OFF_BQ + 2 * D_BK + D_BV
OFF_GATE = OFF_BG + D_BV + 2 * D_C

CHUNK = 128
N_LEVELS = 7
assert 1 << N_LEVELS == CHUNK and CHUNK == SGU_CHUNK

LANES = 128
MXU_N = 256
VMEM_LIMIT = 60 * 1024 * 1024
NEG_BIG = -1e30


def _sigmoid(v):
    return 0.5 + 0.5 * jnp.tanh(0.5 * v)


def _silu(v):
    return v * _sigmoid(v)


def _dot(a, b):
    return jnp.dot(a, b, preferred_element_type=F32)


def _dot_nt(a, b):
    return lax.dot_general(a, b, (((1,), (1,)), ((), ())), preferred_element_type=F32)


def _dot_tn(a, b):
    return lax.dot_general(a, b, (((0,), (0,)), ((), ())), preferred_element_type=F32)


def _norm_mod(x, nw, shift, scale):
    ms = jnp.mean(x * x, axis=-1, keepdims=True)
    return (x * lax.rsqrt(ms + EPS) * nw) * (1.0 + scale) + shift


def _params(*sem):
    return pltpu.CompilerParams(dimension_semantics=sem, vmem_limit_bytes=VMEM_LIMIT)


def _adaln_kernel(c_ref, w_ref, b_ref, o_ref):
    ca = _silu(c_ref[...]).astype(BF16)
    o_ref[0] = _dot(ca, w_ref[0].astype(BF16)) + b_ref[0]


def _adaln(c, ada_w, ada_b):
    depth, d, n = ada_w.shape
    bsz = c.shape[0]
    tn = n // 4
    return pl.pallas_call(
        _adaln_kernel,
        grid=(depth, n // tn),
        in_specs=[
            pl.BlockSpec((bsz, d), lambda l, j: (0, 0)),
            pl.BlockSpec((1, d, tn), lambda l, j: (l, 0, j)),
            pl.BlockSpec((1, 1, tn), lambda l, j: (l, 0, j)),
        ],
        out_specs=pl.BlockSpec((1, bsz, tn), lambda l, j: (l, 0, j)),
        out_shape=jax.ShapeDtypeStruct((depth, bsz, n), F32),
        compiler_params=_params("arbitrary", "arbitrary"),
        name="adaln",
    )(c, ada_w, ada_b.reshape(depth, 1, n))


HQ_Q, HQ_F, HQ_K, HQ_V = 0, D_BK, 2 * D_BK, 3 * D_BK
GC_G, GC_U, GC_V = 0, D_BV, D_BV + D_C
assert D_BK == D_BV == D_C == D_A


def _inproj_kernel(layer, x_ref, mod_ref, nw_ref, w_ref, convw_ref, lbl_ref, snw_ref,
                   ya_ref, hq_ref, gc_ref, gates_ref, zbuf):
    tm = x_ref.shape[1]
    d = x_ref.shape[2]

    @pl.when(pl.program_id(1) == 0)
    def _():
        zbuf[0:8, :] = jnp.zeros((8, D_A), F32)

    h = _norm_mod(x_ref[0], nw_ref[...], mod_ref[0, 0:1, :], mod_ref[0, 1:2, :]).astype(BF16)

    def group(off):
        return _dot(h, w_ref[:, off:off + D_A])

    z = group(D_A) * group(2 * D_A)
    zbuf[8:8 + tm, :] = z
    y = (convw_ref[0:1, :] * z + convw_ref[1:2, :] * zbuf[7:7 + tm, :]
         + convw_ref[2:3, :] * zbuf[6:6 + tm, :])
    ya_ref[0] = (group(0) * y).astype(BF16)
    zbuf[0:8, :] = zbuf[tm:tm + 8, :]

    lg = lbl_ref[...]
    pe = jnp.exp(lg - jnp.max(lg, axis=0, keepdims=True))
    p = pe / jnp.sum(pe, axis=0, keepdims=True)
    lb = jnp.clip(jnp.sum(p[0:layer + 1], axis=0, keepdims=True) - p[0:1], 0.0, 1.0)
    hq_ref[0, :, HQ_Q:HQ_Q + D_BK] = _silu(group(OFF_BQ)).astype(BF16)
    hq_ref[0, :, HQ_V:HQ_V + D_BV] = group(OFF_BQ + 2 * D_BK).astype(BF16)
    sig = _sigmoid(group(OFF_BQ + D_BK))
    hq_ref[0, :, HQ_F:HQ_F + D_BK] = jnp.log(jnp.maximum(lb, LB_FLOOR) + (1.0 - lb) * sig).astype(BF16)
    hq_ref[0, :, HQ_K:HQ_K + D_BK] = ((1.0 - lb) * (1.0 - sig)).astype(BF16)

    gc_ref[0, :, GC_G:GC_G + D_BV] = _silu(group(OFF_BG)).astype(BF16)
    gc_ref[0, :, GC_U:GC_U + D_C] = group(OFF_BG + D_BV).astype(BF16)
    cv = group(OFF_BG + D_BV + D_C)
    vn = cv * lax.rsqrt(jnp.mean(cv * cv, axis=-1, keepdims=True) + EPS) * snw_ref[...]
    gc_ref[0, :, GC_V:GC_V + D_C] = vn.astype(BF16)

    for g in range(3 * d // D_A):
        gates_ref[0, :, g * D_A:(g + 1) * D_A] = _sigmoid(group(OFF_GATE + g * D_A)).astype(BF16)


def _inproj(layer, x, mod, nw, w_bf16, conv_w, lb_logits, sgu_nw, tm):
    bsz, seq, d = x.shape
    n = w_bf16.shape[2]
    assert n == OFF_GATE + 3 * d and (3 * d) % D_A == 0
    row = lambda b, i: (b, i, 0)
    const2 = lambda b, i: (0, 0)
    out = lambda width: jax.ShapeDtypeStruct((bsz, seq, width), BF16)
    return pl.pallas_call(
        functools.partial(_inproj_kernel, layer),
        grid=(bsz, seq // tm),
        in_specs=[
            pl.BlockSpec((1, tm, d), row),
            pl.BlockSpec((1, 6, d), lambda b, i: (b, 0, 0)),
            pl.BlockSpec((1, d), const2),
            pl.BlockSpec((None, d, n), lambda b, i: (layer, 0, 0), pipeline_mode=pl.Buffered(1)),
            pl.BlockSpec(conv_w.shape, const2),
            pl.BlockSpec(lb_logits.shape, const2),
            pl.BlockSpec((1, D_C), const2),
        ],
        out_specs=[
            pl.BlockSpec((1, tm, D_A), row),
            pl.BlockSpec((1, tm, 4 * D_BK), row),
            pl.BlockSpec((1, tm, 3 * D_C), row),
            pl.BlockSpec((1, tm, 3 * d), row),
        ],
        out_shape=[out(D_A), out(4 * D_BK), out(3 * D_C), out(3 * d)],
        scratch_shapes=[pltpu.VMEM((tm + 8, D_A), F32)],
        compiler_params=_params("arbitrary", "arbitrary"),
        name="inproj",
    )(x, mod, nw.reshape(1, d), w_bf16, conv_w, lb_logits, sgu_nw.reshape(1, D_C))


def _level_tables():
    t = np.arange(CHUNK)
    sum_mat = np.zeros((N_LEVELS + 1, CHUNK, CHUNK), np.float32)
    sum_mat[0] = (t[None, :] <= t[:, None])
    up = np.zeros((N_LEVELS + 1, CHUNK, LANES), np.float32)
    low = np.zeros((N_LEVELS + 1, CHUNK, LANES), np.float32)
    pair = np.zeros((N_LEVELS + 1, CHUNK, CHUNK), np.float32)
    pair[0] = np.eye(CHUNK)
    for j in range(1, N_LEVELS + 1):
        blk, half = 1 << j, 1 << (j - 1)
        base = (t // blk) * blk
        m = base + half - 1
        is_up = (t - base) >= half
        u = t[None, :]
        upper_rows = (u > m[:, None]) & (u <= t[:, None])
        lower_rows = (u > t[:, None]) & (u <= m[:, None])
        sum_mat[j] = np.where(is_up[:, None], upper_rows, lower_rows)
        up[j] = is_up[:, None]
        low[j] = ~is_up[:, None]
        pair[j] = (t[:, None] // blk) == (t[None, :] // blk)
    pairc = np.zeros((N_LEVELS + 1, CHUNK // 2, CHUNK), np.float32)
    for j in range(1, N_LEVELS + 1):
        pairc[j] = pair[j][np.nonzero(((t >> (j - 1)) & 1) == 1)[0]]
    return (jnp.asarray(sum_mat.reshape((N_LEVELS + 1) * CHUNK, CHUNK), BF16),
            jnp.asarray(up), jnp.asarray(low), jnp.asarray(pair), jnp.asarray(pairc))


HEAD_GROUPS = ((0, 1), (2, 3))
SUBLANES = 8
MERGE_ROWS = 128


def _row_blocks(m):
    return [m[SUBLANES * r:SUBLANES * (r + 1), :] for r in range(m.shape[0] // SUBLANES)]


def _mixer_kernel(ts,
                  ya_ref, hq_ref, gc_ref, gates_ref, x_ref, mod_ref, hnw_ref, sguw_ref, sgub_ref,
                  summat_ref, up_ref, low_ref, pair_ref, pairc_ref, wa_ref, wb_ref, wc_ref, wo_ref,
                  o_ref,
                  st_ref, yb_ref, yc_ref, wsm_ref, br_ref, mg_ref, kt_ref):
    n_chunks = ts // CHUNK
    d = x_ref.shape[-1]

    @pl.when(pl.program_id(1) == 0)
    def _():
        st_ref[...] = jnp.zeros_like(st_ref)

    tril = (lax.broadcasted_iota(jnp.int32, (SGU_CHUNK, SGU_CHUNK), 0)
            >= lax.broadcasted_iota(jnp.int32, (SGU_CHUNK, SGU_CHUNK), 1))
    for g in range(G_C):
        wsm_ref[g] = jnp.where(tril, sguw_ref[g], 0.0).astype(BF16)

    def chunk_body(c, carry):
        r0 = pl.multiple_of(c * CHUNK, CHUNK)
        rows = pl.ds(r0, CHUNK)

        for g in range(G_C):
            cs = slice(g * GC_CH, (g + 1) * GC_CH)
            vn = gc_ref[0, rows, GC_V + g * GC_CH:GC_V + (g + 1) * GC_CH]
            u = gc_ref[0, rows, GC_U + g * GC_CH:GC_U + (g + 1) * GC_CH].astype(F32)
            yc_ref[rows, cs] = (u * (_dot(wsm_ref[g], vn) + sgub_ref[g])).astype(BF16)

        lv = _dot(summat_ref[...], hq_ref[0, rows, HQ_F:HQ_F + D_BK])

        for heads in HEAD_GROUPS:
            cs = {h: slice(h * DK_B, (h + 1) * DK_B) for h in heads}
            q_bf = {h: hq_ref[0, rows, HQ_Q + h * DK_B:HQ_Q + (h + 1) * DK_B] for h in heads}
            k_bf = {h: hq_ref[0, rows, HQ_K + h * DK_B:HQ_K + (h + 1) * DK_B] for h in heads}
            v_h = {h: hq_ref[0, rows, HQ_V + h * DV_B:HQ_V + (h + 1) * DV_B] for h in heads}
            q_h = {h: q_bf[h].astype(F32) for h in heads}
            k_h = {h: k_bf[h].astype(F32) for h in heads}
            scores = {}
            for h in heads:
                slot = h * (N_LEVELS + 1)
                kt_ref[slot] = k_bf[h].T
                scores[h] = _row_blocks(pair_ref[0] * _dot(q_bf[h], kt_ref[slot]))
            for j in range(1, N_LEVELS + 1):
                half = 1 << (j - 1)
                for h in heads:
                    e = jnp.exp(lv[j * CHUNK:(j + 1) * CHUNK, cs[h]])
                    slot = h * (N_LEVELS + 1) + j
                    if half < SUBLANES:
                        q_j = (q_h[h] * e * up_ref[j]).astype(BF16)
                        k_j = (k_h[h] * e * low_ref[j]).astype(BF16)
                        kt_ref[slot] = k_j.T
                        s_j = _row_blocks(pair_ref[j] * _dot(q_j, kt_ref[slot]))
                        scores[h] = [a + b for a, b in zip(scores[h], s_j)]
                    else:
                        zero = jnp.zeros((half, DK_B), F32)
                        k_parts, q_parts, up_blocks = [], [], []
                        for a in range(0, CHUNK, 2 * half):
                            k_parts += [k_h[h][a:a + half] * e[a:a + half], zero]
                            q_parts.append(q_h[h][a + half:a + 2 * half] * e[a + half:a + 2 * half])
                            up_blocks += list(range((a + half) // SUBLANES, (a + 2 * half) // SUBLANES))
                        kt_ref[slot] = jnp.concatenate(k_parts, axis=0).astype(BF16).T
                        q_c = jnp.concatenate(q_parts, axis=0).astype(BF16)
                        s_c = _row_blocks(pairc_ref[j] * _dot(q_c, kt_ref[slot]))
                        for r, blk in zip(up_blocks, s_c):
                            scores[h][r] = scores[h][r] + blk
            for h in heads:
                b_h = lv[0:CHUNK, cs[h]]
                st = st_ref[h]
                o = _dot_nt((q_h[h] * jnp.exp(b_h)).astype(BF16), st.astype(BF16))
                o = o + _dot(jnp.concatenate(scores[h], axis=0).astype(BF16), v_h[h])
                b_last = b_h[CHUNK - 1:CHUNK, :]
                k_dec = (k_h[h] * jnp.exp(b_last - b_h)).astype(BF16)
                st_ref[h] = st * jnp.exp(b_last) + _dot_tn(v_h[h], k_dec)
                on = o * lax.rsqrt(jnp.mean(o * o, axis=-1, keepdims=True) + EPS) * hnw_ref[:, cs[h]]
                sg = gc_ref[0, rows, GC_G + h * DV_B:GC_G + (h + 1) * DV_B].astype(F32)
                yb_ref[rows, cs[h]] = (on * sg).astype(BF16)
        return carry

    lax.fori_loop(0, n_chunks, chunk_body, 0)

    br_ref[0] = _dot(ya_ref[0], wa_ref[...])
    br_ref[1] = _dot(yb_ref[...], wb_ref[...])
    br_ref[2] = _dot(yc_ref[...], wc_ref[...])

    def merge_body(r, carry):
        rows = pl.ds(pl.multiple_of(r * MERGE_ROWS, MERGE_ROWS), MERGE_ROWS)
        merged = gates_ref[0, rows, 0:d].astype(F32) * br_ref[0, rows, :]
        merged = merged + gates_ref[0, rows, d:2 * d].astype(F32) * br_ref[1, rows, :]
        merged = merged + gates_ref[0, rows, 2 * d:3 * d].astype(F32) * br_ref[2, rows, :]
        mg_ref[rows, :] = merged.astype(BF16)
        return carry

    lax.fori_loop(0, ts // MERGE_ROWS, merge_body, 0)
    o_ref[0] = x_ref[0] + mod_ref[0, 2:3, :] * _dot(mg_ref[...], wo_ref[...])


def _mixer(layer, ya, hq, gc, gates, x, mod, hgrn_nw, sgu_w, sgu_b, tables, wa, wb, wc, wo, ts):
    bsz, seq, d = x.shape
    slab = lambda w: pl.BlockSpec((None,) + w.shape[1:], lambda b, i: (layer, 0, 0))
    sum_mat, up, low, pair, pairc = tables
    sgub = jnp.broadcast_to(sgu_b[:, :, None], (G_C, SGU_CHUNK, GC_CH))
    row = lambda b, i: (b, i, 0)
    const2 = lambda b, i: (0, 0)
    const3 = lambda b, i: (0, 0, 0)
    return pl.pallas_call(
        functools.partial(_mixer_kernel, ts),
        grid=(bsz, seq // ts),
        in_specs=[
            pl.BlockSpec((1, ts, ya.shape[-1]), row),
            pl.BlockSpec((1, ts, hq.shape[-1]), row),
            pl.BlockSpec((1, ts, gc.shape[-1]), row),
            pl.BlockSpec((1, ts, gates.shape[-1]), row),
            pl.BlockSpec((1, ts, d), row),
            pl.BlockSpec((1, 6, d), lambda b, i: (b, 0, 0)),
            pl.BlockSpec((1, D_BV), const2),
            pl.BlockSpec(sgu_w.shape, const3),
            pl.BlockSpec(sgub.shape, const3),
            pl.BlockSpec(sum_mat.shape, const2),
            pl.BlockSpec(up.shape, const3),
            pl.BlockSpec(low.shape, const3),
            pl.BlockSpec(pair.shape, const3),
            pl.BlockSpec(pairc.shape, const3),
            slab(wa),
            slab(wb),
            slab(wc),
            slab(wo),
        ],
        out_specs=pl.BlockSpec((1, ts, d), row),
        out_shape=jax.ShapeDtypeStruct((bsz, seq, d), F32),
        scratch_shapes=[
            pltpu.VMEM((H_B, DV_B, DK_B), F32),
            pltpu.VMEM((ts, D_BV), BF16),
            pltpu.VMEM((ts, D_C), BF16),
            pltpu.VMEM((G_C, SGU_CHUNK, SGU_CHUNK), BF16),
            pltpu.VMEM((3, ts, d), F32),
            pltpu.VMEM((ts, d), BF16),
            pltpu.VMEM((H_B * (N_LEVELS + 1), DK_B, CHUNK), BF16),
        ],
        compiler_params=_params("arbitrary", "arbitrary"),
        name="mixer",
    )(ya, hq, gc, gates, x, mod, hgrn_nw.reshape(1, D_BV), sgu_w, sgub,
      sum_mat, up, low, pair, pairc, wa, wb, wc, wo)


def _swiglu_tile(h, w1_at, w3_at, w2, g_ref):
    d_ff = g_ref.shape[1]
    for c0 in range(0, d_ff, MXU_N):
        a = _dot(h, w1_at(c0))
        g_ref[:, c0:c0 + MXU_N] = (_silu(a) * _dot(h, w3_at(c0))).astype(BF16)
    return _dot(g_ref[...], w2)


def _final_norm(v, fw):
    return v * lax.rsqrt(jnp.mean(v * v, axis=-1, keepdims=True) + EPS) * fw


def _ffn_kernel(final, x_ref, mod_ref, nw_ref, fw_ref, w1_ref, w3_ref, w2_ref, o_ref, g_ref):
    x = x_ref[0]
    h = _norm_mod(x, nw_ref[...], mod_ref[0, 3:4, :], mod_ref[0, 4:5, :]).astype(BF16)
    y = _swiglu_tile(h, lambda c0: w1_ref[:, c0:c0 + MXU_N], lambda c0: w3_ref[:, c0:c0 + MXU_N],
                     w2_ref[...], g_ref)
    out = x + mod_ref[0, 5:6, :] * y
    o_ref[0] = _final_norm(out, fw_ref[...]) if final else out


def _ffn(x, mod, nw, fw, w1, w3, w2, tm, final):
    bsz, seq, d = x.shape
    d_ff = w1.shape[1]
    const2 = lambda b, i: (0, 0)
    resident = dict(pipeline_mode=pl.Buffered(1))
    return pl.pallas_call(
        functools.partial(_ffn_kernel, final),
        grid=(bsz, seq // tm),
        in_specs=[
            pl.BlockSpec((1, tm, d), lambda b, i: (b, i, 0)),
            pl.BlockSpec((1, 6, d), lambda b, i: (b, 0, 0)),
            pl.BlockSpec((1, d), const2),
            pl.BlockSpec((1, d), const2),
            pl.BlockSpec((d, d_ff), const2, **resident),
            pl.BlockSpec((d, d_ff), const2, **resident),
            pl.BlockSpec((d_ff, d), const2, **resident),
        ],
        out_specs=pl.BlockSpec((1, tm, d), lambda b, i: (b, i, 0)),
        out_shape=jax.ShapeDtypeStruct((bsz, seq, d), F32),
        scratch_shapes=[pltpu.VMEM((tm, d_ff), BF16)],
        compiler_params=_params("arbitrary", "arbitrary"),
        name="ffn",
    )(x, mod, nw.reshape(1, d), fw.reshape(1, d), w1, w3, w2)


META_E, META_P, META_R = 0, 2, 4


def _router_kernel(x_ref, mod_ref, nw_ref, rw_ref, rb_ref, tri_ref, h_ref, meta_ref, rt_ref, cnt_ref, carry):
    first = jnp.logical_and(pl.program_id(0) == 0, pl.program_id(1) == 0)

    @pl.when(first)
    def _():
        carry[...] = jnp.zeros_like(carry)

    h = _norm_mod(x_ref[0], nw_ref[...], mod_ref[0, 3:4, :], mod_ref[0, 4:5, :])
    h_ref[...] = h.reshape(h_ref.shape)
    h_hi = h.astype(BF16)
    h_lo = (h - h_hi.astype(F32)).astype(BF16)
    part = _dot(h_hi, rw_ref[...])
    logits = (part[:, 0:LANES] + part[:, LANES:2 * LANES]) + _dot(h_lo, rw_ref[:, 0:LANES]) + rb_ref[...]
    lane = lax.broadcasted_iota(jnp.int32, logits.shape, 1)
    m1 = jnp.max(logits, axis=-1, keepdims=True)
    i1 = jnp.min(jnp.where(logits == m1, lane, LANES), axis=-1, keepdims=True)
    rest = jnp.where(lane == i1, -jnp.inf, logits)
    m2 = jnp.max(rest, axis=-1, keepdims=True)
    i2 = jnp.min(jnp.where(rest == m2, lane, LANES), axis=-1, keepdims=True)
    e2 = jnp.exp(m2 - m1)
    p1 = 1.0 / (1.0 + e2)
    p2 = e2 / (1.0 + e2)
    sel1 = lane == i1
    sel2 = lane == i2
    onehot = jnp.where(jnp.logical_or(sel1, sel2), 1.0, 0.0)
    before = _dot(tri_ref[...], onehot.astype(BF16)) + carry[...]
    r1 = jnp.sum(jnp.where(sel1, before, 0.0), axis=-1, keepdims=True)
    r2 = jnp.sum(jnp.where(sel2, before, 0.0), axis=-1, keepdims=True)
    carry[...] = carry[...] + jnp.sum(onehot, axis=0, keepdims=True)
    cnt_ref[...] = carry[...]
    rec = jnp.where(lane == META_E, i1.astype(F32), 0.0)
    rec = jnp.where(lane == META_E + 1, i2.astype(F32), rec)
    rec = jnp.where(lane == META_P, p1, rec)
    rec = jnp.where(lane == META_P + 1, p2, rec)
    rec = jnp.where(lane == META_R, r1, rec)
    rec = jnp.where(lane == META_R + 1, r2, rec)
    meta_ref[0] = rec
    rt_ref[...] = rec.T[0:SUBLANES, :]


def _router(x, mod, nw, router_w, router_b, tm):
    bsz, seq, d = x.shape
    ne = router_w.shape[1]
    rw = jnp.pad(router_w.astype(F32), ((0, 0), (0, LANES - ne)))
    rw_hi = rw.astype(BF16)
    rw = jnp.concatenate([rw_hi, (rw - rw_hi.astype(F32)).astype(BF16)], axis=1)
    rb = jnp.pad(router_b.astype(F32).reshape(1, ne), ((0, 0), (0, LANES - ne)), constant_values=NEG_BIG)
    tri = jnp.asarray(np.tril(np.ones((tm, tm), np.float32), -1), BF16)
    const2 = lambda b, i: (0, 0)
    return pl.pallas_call(
        _router_kernel,
        grid=(bsz, seq // tm),
        in_specs=[
            pl.BlockSpec((1, tm, d), lambda b, i: (b, i, 0)),
            pl.BlockSpec((1, 6, d), lambda b, i: (b, 0, 0)),
            pl.BlockSpec((1, d), const2),
            pl.BlockSpec((d, 2 * LANES), const2),
            pl.BlockSpec((1, LANES), const2),
            pl.BlockSpec((tm, tm), const2),
        ],
        out_specs=[
            pl.BlockSpec((tm, d // LANES, LANES), lambda b, i: (b * (seq // tm) + i, 0, 0)),
            pl.BlockSpec((1, tm, LANES), lambda b, i: (b, i, 0)),
            pl.BlockSpec((SUBLANES, tm), lambda b, i: (0, b * (seq // tm) + i)),
            pl.BlockSpec((1, LANES), const2),
        ],
        out_shape=[
            jax.ShapeDtypeStruct((bsz * seq, d // LANES, LANES), F32),
            jax.ShapeDtypeStruct((bsz, seq, LANES), F32),
            jax.ShapeDtypeStruct((SUBLANES, bsz * seq), F32),
            jax.ShapeDtypeStruct((1, LANES), F32),
        ],
        scratch_shapes=[pltpu.VMEM((1, LANES), F32)],
        compiler_params=_params("arbitrary", "arbitrary"),
        name="router",
    )(x, mod, nw.reshape(1, d), rw, rb, tri)


def _row_copy(src, src_row, dst, dst_row, sem):
    return pltpu.make_async_copy(src.at[src_row], dst.at[dst_row], sem)


def _dispatch_kernel(tt, p0_ref, p1_ref, h_ref, xs_hbm, sem):
    def issue(t, carry):
        _row_copy(h_ref, t, xs_hbm, p0_ref[0, 0, t], sem).start(priority=0)
        _row_copy(h_ref, t, xs_hbm, p1_ref[0, 0, t], sem).start(priority=1)
        return carry

    lax.fori_loop(0, tt, issue, 0, unroll=8)
    for _ in range(2):
        pltpu.make_async_copy(h_ref, xs_hbm.at[pl.ds(0, tt)], sem).wait()


def _dispatch(h2d, pos, tt):
    n_tok, nsub, lanes = h2d.shape
    idx = pl.BlockSpec((1, 1, tt), lambda i: (i, 0, 0), memory_space=pltpu.SMEM)
    return pl.pallas_call(
        functools.partial(_dispatch_kernel, tt),
        grid=(n_tok // tt,),
        in_specs=[
            idx,
            idx,
            pl.BlockSpec((tt, nsub, lanes), lambda i: (i, 0, 0)),
        ],
        out_specs=pl.BlockSpec(memory_space=pl.ANY),
        out_shape=jax.ShapeDtypeStruct((2 * n_tok, nsub, lanes), F32),
        scratch_shapes=[pltpu.SemaphoreType.DMA(())],
        compiler_params=_params("arbitrary"),
        name="dispatch",
    )(pos[0].reshape(n_tok // tt, 1, tt), pos[1].reshape(n_tok // tt, 1, tt), h2d)


def _expert_kernel(tm, tile_ref, exp_ref, lo_ref, hi_ref, first_ref, valid_ref, slot_ref, stage_ref, next_ref,
                   xs_ref, w1_hbm, w3_hbm, w2_hbm, y_ref, g_ref, wb1, wb3, wb2, stg1, stg3, stg2, sem):
    s = pl.program_id(0)
    d_ff = g_ref.shape[1]
    n_chunks = d_ff // MXU_N

    def chunk_copies(e, c, k):
        cols = pl.ds(c * MXU_N, MXU_N)
        return (pltpu.make_async_copy(w1_hbm.at[e, :, cols], stg1.at[k], sem.at[0, k]),
                pltpu.make_async_copy(w3_hbm.at[e, :, cols], stg3.at[k], sem.at[1, k]),
                pltpu.make_async_copy(w2_hbm.at[e, cols, :], stg2.at[k], sem.at[2, k]))

    def start_chunk(e, c):
        for cp in chunk_copies(e, c, c % 2):
            cp.start()

    def land_chunk(e, c, dst):
        for cp in chunk_copies(e, c, c % 2):
            cp.wait()
        cols = slice(c * MXU_N, (c + 1) * MXU_N)
        wb1[dst, :, cols] = stg1[c % 2].astype(BF16)
        wb3[dst, :, cols] = stg3[c % 2].astype(BF16)
        wb2[dst, cols, :] = stg2[c % 2].astype(BF16)

    @pl.when(s == 0)
    def _():
        e0 = exp_ref[0]
        start_chunk(e0, 0)
        for c in range(n_chunks):
            if c + 1 < n_chunks:
                start_chunk(e0, c + 1)
            land_chunk(e0, c, slot_ref[0])

    def run(stage):
        cur = slot_ref[s]
        nxt = next_ref[s]
        h = xs_ref[...].reshape(tm, -1).astype(BF16)
        if stage:
            start_chunk(nxt, 0)
        for c in range(n_chunks):
            cols = slice(c * MXU_N, (c + 1) * MXU_N)
            a = _dot(h, wb1[cur, :, cols])
            g_ref[:, cols] = (_silu(a) * _dot(h, wb3[cur, :, cols])).astype(BF16)
            if stage:
                if c + 1 < n_chunks:
                    start_chunk(nxt, c + 1)
                land_chunk(nxt, c, 1 - cur)
        res = _dot(g_ref[...], wb2[cur]).reshape(y_ref.shape)
        row = tile_ref[s] * tm + lax.broadcasted_iota(jnp.int32, (tm, 1, 1), 0)
        mine = jnp.logical_and(row >= lo_ref[s], row < hi_ref[s])

        @pl.when(first_ref[s] == 1)
        def _():
            y_ref[...] = jnp.where(mine, res, 0.0)

        @pl.when(first_ref[s] == 0)
        def _():
            y_ref[...] = jnp.where(mine, res, y_ref[...])

    @pl.when(jnp.logical_and(valid_ref[s] == 1, stage_ref[s] == 1))
    def _():
        run(True)

    @pl.when(jnp.logical_and(valid_ref[s] == 1, stage_ref[s] == 0))
    def _():
        run(False)


def _experts(xs, steps, w1, w3, w2, tm):
    n_rows, nsub, lanes = xs.shape
    d = nsub * lanes
    d_ff = w1.shape[2]
    n_steps = steps[0].shape[0]
    tile_map = lambda s, tile, *_: (tile[s], 0, 0)
    grid_spec = pltpu.PrefetchScalarGridSpec(
        num_scalar_prefetch=len(steps),
        grid=(n_steps,),
        in_specs=[
            pl.BlockSpec((tm, nsub, lanes), tile_map),
            pl.BlockSpec(memory_space=pl.ANY),
            pl.BlockSpec(memory_space=pl.ANY),
            pl.BlockSpec(memory_space=pl.ANY),
        ],
        out_specs=pl.BlockSpec((tm, nsub, lanes), tile_map),
        scratch_shapes=[
            pltpu.VMEM((tm, d_ff), BF16),
            pltpu.VMEM((2, d, d_ff), BF16),
            pltpu.VMEM((2, d, d_ff), BF16),
            pltpu.VMEM((2, d_ff, d), BF16),
            pltpu.VMEM((2, d, MXU_N), F32),
            pltpu.VMEM((2, d, MXU_N), F32),
            pltpu.VMEM((2, MXU_N, d), F32),
            pltpu.SemaphoreType.DMA((3, 2)),
        ],
    )
    return pl.pallas_call(
        functools.partial(_expert_kernel, tm),
        grid_spec=grid_spec,
        out_shape=jax.ShapeDtypeStruct((n_rows, nsub, lanes), F32),
        compiler_params=_params("arbitrary"),
        name="experts",
    )(*steps, xs, w1, w3, w2)


def _combine_kernel(tt, final, p0_ref, p1_ref, n0_ref, n1_ref, y_hbm, x_ref, mod_ref, meta_ref, fw_ref,
                    o_ref, buf, sem):
    step = pl.program_id(0) * pl.num_programs(1) + pl.program_id(1)
    n_steps = pl.num_programs(0) * pl.num_programs(1)
    slot = step % 2

    def gather(refs, t, s):
        _row_copy(y_hbm, refs[0][0, 0, t], buf.at[s, 0], t, sem.at[s]).start(priority=0)
        _row_copy(y_hbm, refs[1][0, 0, t], buf.at[s, 1], t, sem.at[s]).start(priority=1)

    def drain(s):
        for k in range(2):
            pltpu.make_async_copy(y_hbm.at[pl.ds(0, tt)], buf.at[s, k], sem.at[s]).wait()

    @pl.when(step == 0)
    def _():
        def body(t, carry):
            gather((p0_ref, p1_ref), t, 0)
            return carry

        lax.fori_loop(0, tt, body, 0, unroll=8)

    drain(slot)
    for t in range(tt):
        gather((n0_ref, n1_ref), t, 1 - slot)
    p1 = meta_ref[0, :, META_P:META_P + 1]
    p2 = meta_ref[0, :, META_P + 1:META_P + 2]
    d = x_ref.shape[-1]
    out = x_ref[0] + mod_ref[0, 5:6, :] * (p1 * buf[slot, 0].reshape(tt, d) + p2 * buf[slot, 1].reshape(tt, d))
    o_ref[0] = _final_norm(out, fw_ref[...]) if final else out

    @pl.when(step == n_steps - 1)
    def _():
        drain(1 - slot)


def _combine(y, pos, x, mod, meta, fw, tt, final):
    bsz, seq, d = x.shape
    per_b = seq // tt
    last = bsz * per_b - 1
    cur = pl.BlockSpec((1, 1, tt), lambda b, i: (b * per_b + i, 0, 0), memory_space=pltpu.SMEM)
    nxt = pl.BlockSpec((1, 1, tt), lambda b, i: (jnp.minimum(b * per_b + i + 1, last), 0, 0),
                       memory_space=pltpu.SMEM)
    p0 = pos[0].reshape(bsz * per_b, 1, tt)
    p1 = pos[1].reshape(bsz * per_b, 1, tt)
    return pl.pallas_call(
        functools.partial(_combine_kernel, tt, final),
        grid=(bsz, per_b),
        in_specs=[
            cur,
            cur,
            nxt,
            nxt,
            pl.BlockSpec(memory_space=pl.ANY),
            pl.BlockSpec((1, tt, d), lambda b, i: (b, i, 0)),
            pl.BlockSpec((1, 6, d), lambda b, i: (b, 0, 0)),
            pl.BlockSpec((1, tt, LANES), lambda b, i: (b, i, 0)),
            pl.BlockSpec((1, d), lambda b, i: (0, 0)),
        ],
        out_specs=pl.BlockSpec((1, tt, d), lambda b, i: (b, i, 0)),
        out_shape=jax.ShapeDtypeStruct((bsz, seq, d), F32),
        scratch_shapes=[pltpu.VMEM((2, 2, tt, d // LANES, LANES), F32), pltpu.SemaphoreType.DMA((2,))],
        compiler_params=_params("arbitrary", "arbitrary"),
        name="combine",
    )(p0, p1, p0, p1, y, x, mod, meta, fw.reshape(1, d))


def _plan_kernel(tm, cnt_ref, rt_ref,
                 tile_o, exp_o, lo_o, hi_o, first_o, valid_o, slot_o, stage_o, next_o, pos_o):
    n_steps = tile_o.shape[0]
    shift = tm.bit_length() - 1
    i32 = jnp.int32
    counts = [cnt_ref[0, e] for e in range(N_EXPERTS)]
    starts = []
    acc = i32(0)
    for e in range(N_EXPERTS):
        starts.append(acc)
        acc = acc + counts[e]
    nxt = [None] * N_EXPERTS
    cur = i32(N_EXPERTS)
    for e in reversed(range(N_EXPERTS)):
        nxt[e] = cur
        cur = jnp.where(counts[e] > 0, i32(e), cur)

    for k in range(2):
        ex = rt_ref[META_E + k:META_E + k + 1, :]
        p = rt_ref[META_R + k:META_R + k + 1, :]
        for e in range(N_EXPERTS):
            p = p + jnp.where(ex == float(e), starts[e].astype(F32), 0.0)
        pos_o[k:k + 1, :] = p.astype(i32)

    for ref in (tile_o, exp_o, lo_o, hi_o, slot_o, next_o):
        ref[0] = i32(0)
    step = i32(0)
    order = i32(0)
    prev_tile = i32(-1)
    for e in range(N_EXPERTS):
        lo = starts[e]
        hi = lo + counts[e]
        t0 = lax.shift_right_logical(lo, shift)
        t1 = lax.shift_right_logical(jnp.maximum(hi - 1, 0), shift)
        n_e = jnp.where(counts[e] > 0, t1 - t0 + 1, 0)
        slot = order % 2
        has_next = (nxt[e] < N_EXPERTS).astype(i32)
        nxt_e = jnp.minimum(nxt[e], N_EXPERTS - 1)

        def visit(i, carry, e=e, lo=lo, hi=hi, t0=t0, slot=slot, has_next=has_next, nxt_e=nxt_e):
            step, prev_tile = carry
            tile = t0 + i
            tile_o[step] = tile
            exp_o[step] = i32(e)
            lo_o[step] = lo
            hi_o[step] = hi
            first_o[step] = (tile != prev_tile).astype(i32)
            valid_o[step] = i32(1)
            slot_o[step] = slot
            stage_o[step] = jnp.where(i == 0, has_next, 0)
            next_o[step] = nxt_e
            return step + 1, tile

        step, prev_tile = lax.fori_loop(0, n_e, visit, (step, prev_tile))
        order = order + (counts[e] > 0).astype(i32)

    last = jnp.maximum(step - 1, 0)

    def pad(i, carry):
        tile_o[i] = tile_o[last]
        exp_o[i] = exp_o[last]
        lo_o[i] = lo_o[last]
        hi_o[i] = hi_o[last]
        slot_o[i] = slot_o[last]
        next_o[i] = next_o[last]
        first_o[i] = i32(0)
        valid_o[i] = i32(0)
        stage_o[i] = i32(0)
        return carry

    lax.fori_loop(jnp.maximum(step, 1), n_steps, pad, 0)

    @pl.when(step == 0)
    def _():
        first_o[0] = i32(0)
        valid_o[0] = i32(0)
        stage_o[0] = i32(0)


def _plan(cnt, rt, n_rows, tm):
    assert tm & (tm - 1) == 0
    n_tok = rt.shape[1]
    n_steps = n_rows // tm + N_EXPERTS - 1
    smem = pl.BlockSpec(memory_space=pltpu.SMEM)
    vmem = pl.BlockSpec(memory_space=pltpu.VMEM)
    step_arr = jax.ShapeDtypeStruct((n_steps,), jnp.int32)
    out = pl.pallas_call(
        functools.partial(_plan_kernel, tm),
        in_specs=[smem, vmem],
        out_specs=[smem] * 9 + [vmem],
        out_shape=[step_arr] * 9 + [jax.ShapeDtypeStruct((2, n_tok), jnp.int32)],
        name="plan",
    )(cnt.astype(jnp.int32), rt)
    return tuple(out[:9]), out[9]


def _moe(x, mod, nw, fw, router_w, router_b, w1, w3, w2, tm_route, tt, tm_exp, final):
    bsz, seq, d = x.shape
    n_tok = bsz * seq
    h, meta, rt, cnt = _router(x, mod, nw, router_w, router_b, tm_route)
    steps, pos = _plan(cnt, rt, 2 * n_tok, tm_exp)
    xs = _dispatch(h, pos, _pick(n_tok, 2 * tt))
    y = _experts(xs, steps, w1, w3, w2, tm_exp)
    return _combine(y, pos, x, mod, meta, fw, tt, final)


def _pick(n, pref):
    t = min(n, pref)
    assert n % t == 0, (n, pref)
    return t


def kernel(x, c, ada_w, ada_b, norm_mix_w, norm_ffn_w, w_in, conv_w, hgrn_lb_logits, hgrn_norm_w,
           sgu_norm_w, sgu_w, sgu_b, w_br_a, w_br_b, w_br_c, w_o, ffn_w1, ffn_w3, ffn_w2,
           moe_router_w, moe_router_b, moe_w1, moe_w3, moe_w2, final_norm_w):
    depth = ada_w.shape[0]
    bsz, seq, d = x.shape
    assert w_in.shape[-1] == OFF_GATE + 3 * d and seq % CHUNK == 0
    assert ffn_w1.shape[-1] % MXU_N == 0 and moe_w1.shape[-1] % MXU_N == 0

    tm_in = _pick(seq, 512)
    ts = _pick(seq, 512)
    tm_ffn = _pick(seq, 1024)
    tm_route = _pick(seq, 512)
    tt = _pick(seq, 512)
    tm_exp = _pick(2 * bsz * seq, 512)

    mod_all = _adaln(c, ada_w, ada_b).reshape(depth, bsz, 6, d)
    tables = _level_tables()
    bf = lambda w: w.astype(BF16)
    w_in_bf, wa_bf, wb_bf, wc_bf, wo_bf = bf(w_in), bf(w_br_a), bf(w_br_b), bf(w_br_c), bf(w_o)

    for l in range(depth):
        mod = mod_all[l]
        ya, hq, gc, gates = _inproj(l, x, mod, norm_mix_w[l], w_in_bf, conv_w[l], hgrn_lb_logits,
                                    sgu_norm_w[l], tm_in)
        x = _mixer(l, ya, hq, gc, gates, x, mod, hgrn_norm_w[l], sgu_w[l], sgu_b[l], tables,
                   wa_bf, wb_bf, wc_bf, wo_bf, ts)
        final = l == depth - 1
        j = l // 2
        if l % 2 == 0:
            x = _ffn(x, mod, norm_ffn_w[l], final_norm_w, bf(ffn_w1[j]), bf(ffn_w3[j]), bf(ffn_w2[j]),
                     tm_ffn, final)
        else:
            x = _moe(x, mod, norm_ffn_w[l], final_norm_w, moe_router_w[j], moe_router_b[j],
                     moe_w1[j], moe_w3[j], moe_w2[j], tm_route, tt, tm_exp, final)
    return x
```

```python
import functools

import numpy as np
import jax
import jax.numpy as jnp
from jax import lax
from jax.experimental import pallas as pl
from jax.experimental.pallas import tpu as pltpu

F32 = jnp.float32
BF16 = jnp.bfloat16

EPS = 1e-6
LB_FLOOR = 1e-30

D_A = 512
CONV_WIDTH = 3
H_B = 4
DK_B = 128
DV_B = 128
D_BK = H_B * DK_B
D_BV = H_B * DV_B
G_C = 4
GC_CH = 128
D_C = G_C * GC_CH
SGU_CHUNK = 128
N_EXPERTS = 8

OFF_BQ = 3 * D_A
OFF_BG = OFF_BQ + 2 * D_BK + D_BV
OFF_GATE = OFF_BG + D_BV + 2 * D_C

CHUNK = 128
N_LEVELS = 7
assert 1 << N_LEVELS == CHUNK and CHUNK == SGU_CHUNK

LANES = 128
MXU_N = 256
VMEM_LIMIT = 60 * 1024 * 1024
NEG_BIG = -1e30


def _sigmoid(v):
    return 0.5 + 0.5 * jnp.tanh(0.5 * v)


def _silu(v):
    return v * _sigmoid(v)


def _dot(a, b):
    return jnp.dot(a, b, preferred_element_type=F32)


def _dot_nt(a, b):
    return lax.dot_general(a, b, (((1,), (1,)), ((), ())), preferred_element_type=F32)


def _dot_tn(a, b):
    return lax.dot_general(a, b, (((0,), (0,)), ((), ())), preferred_element_type=F32)


def _norm_mod(x, nw, shift, scale):
    ms = jnp.mean(x * x, axis=-1, keepdims=True)
    return (x * lax.rsqrt(ms + EPS) * nw) * (1.0 + scale) + shift


def _params(*sem):
    return pltpu.CompilerParams(dimension_semantics=sem, vmem_limit_bytes=VMEM_LIMIT)


def _adaln_kernel(c_ref, w_ref, b_ref, o_ref):
    ca = _silu(c_ref[...]).astype(BF16)
    o_ref[0] = _dot(ca, w_ref[0].astype(BF16)) + b_ref[0]


def _adaln(c, ada_w, ada_b):
    depth, d, n = ada_w.shape
    bsz = c.shape[0]
    tn = n // 4
    return pl.pallas_call(
        _adaln_kernel,
        grid=(depth, n // tn),
        in_specs=[
            pl.BlockSpec((bsz, d), lambda l, j: (0, 0)),
            pl.BlockSpec((1, d, tn), lambda l, j: (l, 0, j)),
            pl.BlockSpec((1, 1, tn), lambda l, j: (l, 0, j)),
        ],
        out_specs=pl.BlockSpec((1, bsz, tn), lambda l, j: (l, 0, j)),
        out_shape=jax.ShapeDtypeStruct((depth, bsz, n), F32),
        compiler_params=_params("arbitrary", "arbitrary"),
        name="adaln",
    )(c, ada_w, ada_b.reshape(depth, 1, n))


HQ_Q, HQ_F, HQ_K, HQ_V = 0, D_BK, 2 * D_BK, 3 * D_BK
GC_G, GC_U, GC_V = 0, D_BV, D_BV + D_C
assert D_BK == D_BV == D_C == D_A


def _inproj_kernel(layer, x_ref, mod_ref, nw_ref, w_ref, convw_ref, lbl_ref, snw_ref,
                   ya_ref, hq_ref, gc_ref, gates_ref, zbuf):
    tm = x_ref.shape[1]
    d = x_ref.shape[2]

    @pl.when(pl.program_id(1) == 0)
    def _():
        zbuf[0:8, :] = jnp.zeros((8, D_A), F32)

    h = _norm_mod(x_ref[0], nw_ref[...], mod_ref[0, 0:1, :], mod_ref[0, 1:2, :]).astype(BF16)

    def group(off):
        return _dot(h, w_ref[:, off:off + D_A])

    z = group(D_A) * group(2 * D_A)
    zbuf[8:8 + tm, :] = z
    y = (convw_ref[0:1, :] * z + convw_ref[1:2, :] * zbuf[7:7 + tm, :]
         + convw_ref[2:3, :] * zbuf[6:6 + tm, :])
    ya_ref[0] = (group(0) * y).astype(BF16)
    zbuf[0:8, :] = zbuf[tm:tm + 8, :]

    lg = lbl_ref[...]
    pe = jnp.exp(lg - jnp.max(lg, axis=0, keepdims=True))
    p = pe / jnp.sum(pe, axis=0, keepdims=True)
    lb = jnp.clip(jnp.sum(p[0:layer + 1], axis=0, keepdims=True) - p[0:1], 0.0, 1.0)
    hq_ref[0, :, HQ_Q:HQ_Q + D_BK] = _silu(group(OFF_BQ)).astype(BF16)
    hq_ref[0, :, HQ_V:HQ_V + D_BV] = group(OFF_BQ + 2 * D_BK).astype(BF16)
    sig = _sigmoid(group(OFF_BQ + D_BK))
    hq_ref[0, :, HQ_F:HQ_F + D_BK] = jnp.log(jnp.maximum(lb, LB_FLOOR) + (1.0 - lb) * sig).astype(BF16)
    hq_ref[0, :, HQ_K:HQ_K + D_BK] = ((1.0 - lb) * (1.0 - sig)).astype(BF16)

    gc_ref[0, :, GC_G:GC_G + D_BV] = _silu(group(OFF_BG)).astype(BF16)
    gc_ref[0, :, GC_U:GC_U + D_C] = group(OFF_BG + D_BV).astype(BF16)
    cv = group(OFF_BG + D_BV + D_C)
    vn = cv * lax.rsqrt(jnp.mean(cv * cv, axis=-1, keepdims=True) + EPS) * snw_ref[...]
    gc_ref[0, :, GC_V:GC_V + D_C] = vn.astype(BF16)

    for g in range(3 * d // D_A):
        gates_ref[0, :, g * D_A:(g + 1) * D_A] = _sigmoid(group(OFF_GATE + g * D_A)).astype(BF16)


def _inproj(layer, x, mod, nw, w_bf16, conv_w, lb_logits, sgu_nw, tm):
    bsz, seq, d = x.shape
    n = w_bf16.shape[2]
    assert n == OFF_GATE + 3 * d and (3 * d) % D_A == 0
    row = lambda b, i: (b, i, 0)
    const2 = lambda b, i: (0, 0)
    out = lambda width: jax.ShapeDtypeStruct((bsz, seq, width), BF16)
    return pl.pallas_call(
        functools.partial(_inproj_kernel, layer),
        grid=(bsz, seq // tm),
        in_specs=[
            pl.BlockSpec((1, tm, d), row),
            pl.BlockSpec((1, 6, d), lambda b, i: (b, 0, 0)),
            pl.BlockSpec((1, d), const2),
            pl.BlockSpec((None, d, n), lambda b, i: (layer, 0, 0), pipeline_mode=pl.Buffered(1)),
            pl.BlockSpec(conv_w.shape, const2),
            pl.BlockSpec(lb_logits.shape, const2),
            pl.BlockSpec((1, D_C), const2),
        ],
        out_specs=[
            pl.BlockSpec((1, tm, D_A), row),
            pl.BlockSpec((1, tm, 4 * D_BK), row),
            pl.BlockSpec((1, tm, 3 * D_C), row),
            pl.BlockSpec((1, tm, 3 * d), row),
        ],
        out_shape=[out(D_A), out(4 * D_BK), out(3 * D_C), out(3 * d)],
        scratch_shapes=[pltpu.VMEM((tm + 8, D_A), F32)],
        compiler_params=_params("arbitrary", "arbitrary"),
        name="inproj",
    )(x, mod, nw.reshape(1, d), w_bf16, conv_w, lb_logits, sgu_nw.reshape(1, D_C))


def _level_tables():
    t = np.arange(CHUNK)
    sum_mat = np.zeros((N_LEVELS + 1, CHUNK, CHUNK), np.float32)
    sum_mat[0] = (t[None, :] <= t[:, None])
    up = np.zeros((N_LEVELS + 1, CHUNK, LANES), np.float32)
    low = np.zeros((N_LEVELS + 1, CHUNK, LANES), np.float32)
    pair = np.zeros((N_LEVELS + 1, CHUNK, CHUNK), np.float32)
    pair[0] = np.eye(CHUNK)
    for j in range(1, N_LEVELS + 1):
        blk, half = 1 << j, 1 << (j - 1)
        base = (t // blk) * blk
        m = base + half - 1
        is_up = (t - base) >= half
        u = t[None, :]
        upper_rows = (u > m[:, None]) & (u <= t[:, None])
        lower_rows = (u > t[:, None]) & (u <= m[:, None])
        sum_mat[j] = np.where(is_up[:, None], upper_rows, lower_rows)
        up[j] = is_up[:, None]
        low[j] = ~is_up[:, None]
        pair[j] = (t[:, None] // blk) == (t[None, :] // blk)
    pairc = np.zeros((N_LEVELS + 1, CHUNK // 2, CHUNK), np.float32)
    for j in range(1, N_LEVELS + 1):
        pairc[j] = pair[j][np.nonzero(((t >> (j - 1)) & 1) == 1)[0]]
    return (jnp.asarray(sum_mat.reshape((N_LEVELS + 1) * CHUNK, CHUNK), BF16),
            jnp.asarray(up), jnp.asarray(low), jnp.asarray(pair), jnp.asarray(pairc))


HEAD_GROUPS = ((0, 1), (2, 3))
SUBLANES = 8
MERGE_ROWS = 128


def _row_blocks(m):
    return [m[SUBLANES * r:SUBLANES * (r + 1), :] for r in range(m.shape[0] // SUBLANES)]


def _mixer_kernel(ts,
                  ya_ref, hq_ref, gc_ref, gates_ref, x_ref, mod_ref, hnw_ref, sguw_ref, sgub_ref,
                  summat_ref, up_ref, low_ref, pair_ref, pairc_ref, wa_ref, wb_ref, wc_ref, wo_ref,
                  o_ref,
                  st_ref, yb_ref, yc_ref, wsm_ref, br_ref, mg_ref, kt_ref):
    n_chunks = ts // CHUNK
    d = x_ref.shape[-1]

    @pl.when(pl.program_id(1) == 0)
    def _():
        st_ref[...] = jnp.zeros_like(st_ref)

    tril = (lax.broadcasted_iota(jnp.int32, (SGU_CHUNK, SGU_CHUNK), 0)
            >= lax.broadcasted_iota(jnp.int32, (SGU_CHUNK, SGU_CHUNK), 1))
    for g in range(G_C):
        wsm_ref[g] = jnp.where(tril, sguw_ref[g], 0.0).astype(BF16)

    def chunk_body(c, carry):
        r0 = pl.multiple_of(c * CHUNK, CHUNK)
        rows = pl.ds(r0, CHUNK)

        for g in range(G_C):
            cs = slice(g * GC_CH, (g + 1) * GC_CH)
            vn = gc_ref[0, rows, GC_V + g * GC_CH:GC_V + (g + 1) * GC_CH]
            u = gc_ref[0, rows, GC_U + g * GC_CH:GC_U + (g + 1) * GC_CH].astype(F32)
            yc_ref[rows, cs] = (u * (_dot(wsm_ref[g], vn) + sgub_ref[g])).astype(BF16)

        lv = _dot(summat_ref[...], hq_ref[0, rows, HQ_F:HQ_F + D_BK])

        for heads in HEAD_GROUPS:
            cs = {h: slice(h * DK_B, (h + 1) * DK_B) for h in heads}
            q_bf = {h: hq_ref[0, rows, HQ_Q + h * DK_B:HQ_Q + (h + 1) * DK_B] for h in heads}
            k_bf = {h: hq_ref[0, rows, HQ_K + h * DK_B:HQ_K + (h + 1) * DK_B] for h in heads}
            v_h = {h: hq_ref[0, rows, HQ_V + h * DV_B:HQ_V + (h + 1) * DV_B] for h in heads}
            q_h = {h: q_bf[h].astype(F32) for h in heads}
            k_h = {h: k_bf[h].astype(F32) for h in heads}
            scores = {}
            for h in heads:
                slot = h * (N_LEVELS + 1)
                kt_ref[slot] = k_bf[h].T
                scores[h] = _row_blocks(pair_ref[0] * _dot(q_bf[h], kt_ref[slot]))
            for j in range(1, N_LEVELS + 1):
                half = 1 << (j - 1)
                for h in heads:
                    e = jnp.exp(lv[j * CHUNK:(j + 1) * CHUNK, cs[h]])
                    slot = h * (N_LEVELS + 1) + j
                    if half < SUBLANES:
                        q_j = (q_h[h] * e * up_ref[j]).astype(BF16)
                        k_j = (k_h[h] * e * low_ref[j]).astype(BF16)
                        kt_ref[slot] = k_j.T
                        s_j = _row_blocks(pair_ref[j] * _dot(q_j, kt_ref[slot]))
                        scores[h] = [a + b for a, b in zip(scores[h], s_j)]
                    else:
                        zero = jnp.zeros((half, DK_B), F32)
                        k_parts, q_parts, up_blocks = [], [], []
                        for a in range(0, CHUNK, 2 * half):
                            k_parts += [k_h[h][a:a + half] * e[a:a + half], zero]
                            q_parts.append(q_h[h][a + half:a + 2 * half] * e[a + half:a + 2 * half])
                            up_blocks += list(range((a + half) // SUBLANES, (a + 2 * half) // SUBLANES))
                        kt_ref[slot] = jnp.concatenate(k_parts, axis=0).astype(BF16).T
                        q_c = jnp.concatenate(q_parts, axis=0).astype(BF16)
                        s_c = _row_blocks(pairc_ref[j] * _dot(q_c, kt_ref[slot]))
                        for r, blk in zip(up_blocks, s_c):
                            scores[h][r] = scores[h][r] + blk
            for h in heads:
                b_h = lv[0:CHUNK, cs[h]]
                st = st_ref[h]
                o = _dot_nt((q_h[h] * jnp.exp(b_h)).astype(BF16), st.astype(BF16))
                o = o + _dot(jnp.concatenate(scores[h], axis=0).astype(BF16), v_h[h])
                b_last = b_h[CHUNK - 1:CHUNK, :]
                k_dec = (k_h[h] * jnp.exp(b_last - b_h)).astype(BF16)
                st_ref[h] = st * jnp.exp(b_last) + _dot_tn(v_h[h], k_dec)
                on = o * lax.rsqrt(jnp.mean(o * o, axis=-1, keepdims=True) + EPS) * hnw_ref[:, cs[h]]
                sg = gc_ref[0, rows, GC_G + h * DV_B:GC_G + (h + 1) * DV_B].astype(F32)
                yb_ref[rows, cs[h]] = (on * sg).astype(BF16)
        return carry

    lax.fori_loop(0, n_chunks, chunk_body, 0)

    br_ref[0] = _dot(ya_ref[0], wa_ref[...])
    br_ref[1] = _dot(yb_ref[...], wb_ref[...])
    br_ref[2] = _dot(yc_ref[...], wc_ref[...])

    def merge_body(r, carry):
        rows = pl.ds(pl.multiple_of(r * MERGE_ROWS, MERGE_ROWS), MERGE_ROWS)
        merged = gates_ref[0, rows, 0:d].astype(F32) * br_ref[0, rows, :]
        merged = merged + gates_ref[0, rows, d:2 * d].astype(F32) * br_ref[1, rows, :]
        merged = merged + gates_ref[0, rows, 2 * d:3 * d].astype(F32) * br_ref[2, rows, :]
        mg_ref[rows, :] = merged.astype(BF16)
        return carry

    lax.fori_loop(0, ts // MERGE_ROWS, merge_body, 0)
    o_ref[0] = x_ref[0] + mod_ref[0, 2:3, :] * _dot(mg_ref[...], wo_ref[...])


def _mixer(layer, ya, hq, gc, gates, x, mod, hgrn_nw, sgu_w, sgu_b, tables, wa, wb, wc, wo, ts):
    bsz, seq, d = x.shape
    slab = lambda w: pl.BlockSpec((None,) + w.shape[1:], lambda b, i: (layer, 0, 0))
    sum_mat, up, low, pair, pairc = tables
    sgub = jnp.broadcast_to(sgu_b[:, :, None], (G_C, SGU_CHUNK, GC_CH))
    row = lambda b, i: (b, i, 0)
    const2 = lambda b, i: (0, 0)
    const3 = lambda b, i: (0, 0, 0)
    return pl.pallas_call(
        functools.partial(_mixer_kernel, ts),
        grid=(bsz, seq // ts),
        in_specs=[
            pl.BlockSpec((1, ts, ya.shape[-1]), row),
            pl.BlockSpec((1, ts, hq.shape[-1]), row),
            pl.BlockSpec((1, ts, gc.shape[-1]), row),
            pl.BlockSpec((1, ts, gates.shape[-1]), row),
            pl.BlockSpec((1, ts, d), row),
            pl.BlockSpec((1, 6, d), lambda b, i: (b, 0, 0)),
            pl.BlockSpec((1, D_BV), const2),
            pl.BlockSpec(sgu_w.shape, const3),
            pl.BlockSpec(sgub.shape, const3),
            pl.BlockSpec(sum_mat.shape, const2),
            pl.BlockSpec(up.shape, const3),
            pl.BlockSpec(low.shape, const3),
            pl.BlockSpec(pair.shape, const3),
            pl.BlockSpec(pairc.shape, const3),
            slab(wa),
            slab(wb),
            slab(wc),
            slab(wo),
        ],
        out_specs=pl.BlockSpec((1, ts, d), row),
        out_shape=jax.ShapeDtypeStruct((bsz, seq, d), F32),
        scratch_shapes=[
            pltpu.VMEM((H_B, DV_B, DK_B), F32),
            pltpu.VMEM((ts, D_BV), BF16),
            pltpu.VMEM((ts, D_C), BF16),
            pltpu.VMEM((G_C, SGU_CHUNK, SGU_CHUNK), BF16),
            pltpu.VMEM((3, ts, d), F32),
            pltpu.VMEM((ts, d), BF16),
            pltpu.VMEM((H_B * (N_LEVELS + 1), DK_B, CHUNK), BF16),
        ],
        compiler_params=_params("arbitrary", "arbitrary"),
        name="mixer",
    )(ya, hq, gc, gates, x, mod, hgrn_nw.reshape(1, D_BV), sgu_w, sgub,
      sum_mat, up, low, pair, pairc, wa, wb, wc, wo)


def _swiglu_tile(h, w1_at, w3_at, w2, g_ref):
    d_ff = g_ref.shape[1]
    for c0 in range(0, d_ff, MXU_N):
        a = _dot(h, w1_at(c0))
        g_ref[:, c0:c0 + MXU_N] = (_silu(a) * _dot(h, w3_at(c0))).astype(BF16)
    return _dot(g_ref[...], w2)


def _final_norm(v, fw):
    return v * lax.rsqrt(jnp.mean(v * v, axis=-1, keepdims=True) + EPS) * fw


def _ffn_kernel(final, x_ref, mod_ref, nw_ref, fw_ref, w1_ref, w3_ref, w2_ref, o_ref, g_ref):
    x = x_ref[0]
    h = _norm_mod(x, nw_ref[...], mod_ref[0, 3:4, :], mod_ref[0, 4:5, :]).astype(BF16)
    y = _swiglu_tile(h, lambda c0: w1_ref[:, c0:c0 + MXU_N], lambda c0: w3_ref[:, c0:c0 + MXU_N],
                     w2_ref[...], g_ref)
    out = x + mod_ref[0, 5:6, :] * y
    o_ref[0] = _final_norm(out, fw_ref[...]) if final else out


def _ffn(x, mod, nw, fw, w1, w3, w2, tm, final):
    bsz, seq, d = x.shape
    d_ff = w1.shape[1]
    const2 = lambda b, i: (0, 0)
    resident = dict(pipeline_mode=pl.Buffered(1))
    return pl.pallas_call(
        functools.partial(_ffn_kernel, final),
        grid=(bsz, seq // tm),
        in_specs=[
            pl.BlockSpec((1, tm, d), lambda b, i: (b, i, 0)),
            pl.BlockSpec((1, 6, d), lambda b, i: (b, 0, 0)),
            pl.BlockSpec((1, d), const2),
            pl.BlockSpec((1, d), const2),
            pl.BlockSpec((d, d_ff), const2, **resident),
            pl.BlockSpec((d, d_ff), const2, **resident),
            pl.BlockSpec((d_ff, d), const2, **resident),
        ],
        out_specs=pl.BlockSpec((1, tm, d), lambda b, i: (b, i, 0)),
        out_shape=jax.ShapeDtypeStruct((bsz, seq, d), F32),
        scratch_shapes=[pltpu.VMEM((tm, d_ff), BF16)],
        compiler_params=_params("arbitrary", "arbitrary"),
        name="ffn",
    )(x, mod, nw.reshape(1, d), fw.reshape(1, d), w1, w3, w2)


META_E, META_P, META_R = 0, 2, 4


def _router_kernel(x_ref, mod_ref, nw_ref, rw_ref, rb_ref, tri_ref, h_ref, meta_ref, rt_ref, cnt_ref, carry):
    first = jnp.logical_and(pl.program_id(0) == 0, pl.program_id(1) == 0)

    @pl.when(first)
    def _():
        carry[...] = jnp.zeros_like(carry)

    h = _norm_mod(x_ref[0], nw_ref[...], mod_ref[0, 3:4, :], mod_ref[0, 4:5, :])
    h_ref[...] = h.reshape(h_ref.shape)
    h_hi = h.astype(BF16)
    h_lo = (h - h_hi.astype(F32)).astype(BF16)
    part = _dot(h_hi, rw_ref[...])
    logits = (part[:, 0:LANES] + part[:, LANES:2 * LANES]) + _dot(h_lo, rw_ref[:, 0:LANES]) + rb_ref[...]
    lane = lax.broadcasted_iota(jnp.int32, logits.shape, 1)
    m1 = jnp.max(logits, axis=-1, keepdims=True)
    i1 = jnp.min(jnp.where(logits == m1, lane, LANES), axis=-1, keepdims=True)
    rest = jnp.where(lane == i1, -jnp.inf, logits)
    m2 = jnp.max(rest, axis=-1, keepdims=True)
    i2 = jnp.min(jnp.where(rest == m2, lane, LANES), axis=-1, keepdims=True)
    e2 = jnp.exp(m2 - m1)
    p1 = 1.0 / (1.0 + e2)
    p2 = e2 / (1.0 + e2)
    sel1 = lane == i1
    sel2 = lane == i2
    onehot = jnp.where(jnp.logical_or(sel1, sel2), 1.0, 0.0)
    before = _dot(tri_ref[...], onehot.astype(BF16)) + carry[...]
    r1 = jnp.sum(jnp.where(sel1, before, 0.0), axis=-1, keepdims=True)
    r2 = jnp.sum(jnp.where(sel2, before, 0.0), axis=-1, keepdims=True)
    carry[...] = carry[...] + jnp.sum(onehot, axis=0, keepdims=True)
    cnt_ref[...] = carry[...]
    rec = jnp.where(lane == META_E, i1.astype(F32), 0.0)
    rec = jnp.where(lane == META_E + 1, i2.astype(F32), rec)
    rec = jnp.where(lane == META_P, p1, rec)
    rec = jnp.where(lane == META_P + 1, p2, rec)
    rec = jnp.where(lane == META_R, r1, rec)
    rec = jnp.where(lane == META_R + 1, r2, rec)
    meta_ref[0] = rec
    rt_ref[...] = rec.T[0:SUBLANES, :]


def _router(x, mod, nw, router_w, router_b, tm):
    bsz, seq, d = x.shape
    ne = router_w.shape[1]
    rw = jnp.pad(router_w.astype(F32), ((0, 0), (0, LANES - ne)))
    rw_hi = rw.astype(BF16)
    rw = jnp.concatenate([rw_hi, (rw - rw_hi.astype(F32)).astype(BF16)], axis=1)
    rb = jnp.pad(router_b.astype(F32).reshape(1, ne), ((0, 0), (0, LANES - ne)), constant_values=NEG_BIG)
    tri = jnp.asarray(np.tril(np.ones((tm, tm), np.float32), -1), BF16)
    const2 = lambda b, i: (0, 0)
    return pl.pallas_call(
        _router_kernel,
        grid=(bsz, seq // tm),
        in_specs=[
            pl.BlockSpec((1, tm, d), lambda b, i: (b, i, 0)),
            pl.BlockSpec((1, 6, d), lambda b, i: (b, 0, 0)),
            pl.BlockSpec((1, d), const2),
            pl.BlockSpec((d, 2 * LANES), const2),
            pl.BlockSpec((1, LANES), const2),
            pl.BlockSpec((tm, tm), const2),
        ],
        out_specs=[
            pl.BlockSpec((tm, d // LANES, LANES), lambda b, i: (b * (seq // tm) + i, 0, 0)),
            pl.BlockSpec((1, tm, LANES), lambda b, i: (b, i, 0)),
            pl.BlockSpec((SUBLANES, tm), lambda b, i: (0, b * (seq // tm) + i)),
            pl.BlockSpec((1, LANES), const2),
        ],
        out_shape=[
            jax.ShapeDtypeStruct((bsz * seq, d // LANES, LANES), F32),
            jax.ShapeDtypeStruct((bsz, seq, LANES), F32),
            jax.ShapeDtypeStruct((SUBLANES, bsz * seq), F32),
            jax.ShapeDtypeStruct((1, LANES), F32),
        ],
        scratch_shapes=[pltpu.VMEM((1, LANES), F32)],
        compiler_params=_params("arbitrary", "arbitrary"),
        name="router",
    )(x, mod, nw.reshape(1, d), rw, rb, tri)


def _row_copy(src, src_row, dst, dst_row, sem):
    return pltpu.make_async_copy(src.at[src_row], dst.at[dst_row], sem)


def _dispatch_kernel(tt, p0_ref, p1_ref, h_ref, xs_hbm, sem):
    def issue(t, carry):
        _row_copy(h_ref, t, xs_hbm, p0_ref[0, 0, t], sem).start(priority=0)
        _row_copy(h_ref, t, xs_hbm, p1_ref[0, 0, t], sem).start(priority=1)
        return carry

    lax.fori_loop(0, tt, issue, 0, unroll=8)
    for _ in range(2):
        pltpu.make_async_copy(h_ref, xs_hbm.at[pl.ds(0, tt)], sem).wait()


def _dispatch(h2d, pos, tt):
    n_tok, nsub, lanes = h2d.shape
    idx = pl.BlockSpec((1, 1, tt), lambda i: (i, 0, 0), memory_space=pltpu.SMEM)
    return pl.pallas_call(
        functools.partial(_dispatch_kernel, tt),
        grid=(n_tok // tt,),
        in_specs=[
            idx,
            idx,
            pl.BlockSpec((tt, nsub, lanes), lambda i: (i, 0, 0)),
        ],
        out_specs=pl.BlockSpec(memory_space=pl.ANY),
        out_shape=jax.ShapeDtypeStruct((2 * n_tok, nsub, lanes), F32),
        scratch_shapes=[pltpu.SemaphoreType.DMA(())],
        compiler_params=_params("arbitrary"),
        name="dispatch",
    )(pos[0].reshape(n_tok // tt, 1, tt), pos[1].reshape(n_tok // tt, 1, tt), h2d)


def _expert_kernel(tm, tile_ref, exp_ref, lo_ref, hi_ref, first_ref, valid_ref, slot_ref, stage_ref, next_ref,
                   xs_ref, w1_hbm, w3_hbm, w2_hbm, y_ref, g_ref, wb1, wb3, wb2, stg1, stg3, stg2, sem):
    s = pl.program_id(0)
    d_ff = g_ref.shape[1]
    n_chunks = d_ff // MXU_N

    def chunk_copies(e, c, k):
        cols = pl.ds(c * MXU_N, MXU_N)
        return (pltpu.make_async_copy(w1_hbm.at[e, :, cols], stg1.at[k], sem.at[0, k]),
                pltpu.make_async_copy(w3_hbm.at[e, :, cols], stg3.at[k], sem.at[1, k]),
                pltpu.make_async_copy(w2_hbm.at[e, cols, :], stg2.at[k], sem.at[2, k]))

    def start_chunk(e, c):
        for cp in chunk_copies(e, c, c % 2):
            cp.start()

    def land_chunk(e, c, dst):
        for cp in chunk_copies(e, c, c % 2):
            cp.wait()
        cols = slice(c * MXU_N, (c + 1) * MXU_N)
        wb1[dst, :, cols] = stg1[c % 2].astype(BF16)
        wb3[dst, :, cols] = stg3[c % 2].astype(BF16)
        wb2[dst, cols, :] = stg2[c % 2].astype(BF16)

    @pl.when(s == 0)
    def _():
        e0 = exp_ref[0]
        start_chunk(e0, 0)
        for c in range(n_chunks):
            if c + 1 < n_chunks:
                start_chunk(e0, c + 1)
            land_chunk(e0, c, slot_ref[0])

    def run(stage):
        cur = slot_ref[s]
        nxt = next_ref[s]
        h = xs_ref[...].reshape(tm, -1).astype(BF16)
        if stage:
            start_chunk(nxt, 0)
        for c in range(n_chunks):
            cols = slice(c * MXU_N, (c + 1) * MXU_N)
            a = _dot(h, wb1[cur, :, cols])
            g_ref[:, cols] = (_silu(a) * _dot(h, wb3[cur, :, cols])).astype(BF16)
            if stage:
                if c + 1 < n_chunks:
                    start_chunk(nxt, c + 1)
                land_chunk(nxt, c, 1 - cur)
        res = _dot(g_ref[...], wb2[cur])
        row = tile_ref[s] * tm + lax.broadcasted_iota(jnp.int32, (tm, 1), 0)
        mine = jnp.logical_and(row >= lo_ref[s], row < hi_ref[s])

        @pl.when(first_ref[s] == 1)
        def _():
            y_ref[...] = jnp.where(mine, res, 0.0)

        @pl.when(first_ref[s] == 0)
        def _():
            y_ref[...] = jnp.where(mine, res, y_ref[...])

    @pl.when(jnp.logical_and(valid_ref[s] == 1, stage_ref[s] == 1))
    def _():
        run(True)

    @pl.when(jnp.logical_and(valid_ref[s] == 1, stage_ref[s] == 0))
    def _():
        run(False)


def _experts(xs, steps, w1, w3, w2, tm):
    n_rows, nsub, lanes = xs.shape
    d = nsub * lanes
    d_ff = w1.shape[2]
    n_steps = steps[0].shape[0]
    tile_map = lambda s, tile, *_: (tile[s], 0, 0)
    grid_spec = pltpu.PrefetchScalarGridSpec(
        num_scalar_prefetch=len(steps),
        grid=(n_steps,),
        in_specs=[
            pl.BlockSpec((tm, nsub, lanes), tile_map),
            pl.BlockSpec(memory_space=pl.ANY),
            pl.BlockSpec(memory_space=pl.ANY),
            pl.BlockSpec(memory_space=pl.ANY),
        ],
        out_specs=pl.BlockSpec((tm, d), lambda s, tile, *_: (tile[s], 0)),
        scratch_shapes=[
            pltpu.VMEM((tm, d_ff), BF16),
            pltpu.VMEM((2, d, d_ff), BF16),
            pltpu.VMEM((2, d, d_ff), BF16),
            pltpu.VMEM((2, d_ff, d), BF16),
            pltpu.VMEM((2, d, MXU_N), F32),
            pltpu.VMEM((2, d, MXU_N), F32),
            pltpu.VMEM((2, MXU_N, d), F32),
            pltpu.SemaphoreType.DMA((3, 2)),
        ],
    )
    return pl.pallas_call(
        functools.partial(_expert_kernel, tm),
        grid_spec=grid_spec,
        out_shape=jax.ShapeDtypeStruct((n_rows, d), F32),
        compiler_params=_params("arbitrary"),
        name="experts",
    )(*steps, xs, w1, w3, w2)


def _combine_kernel(tt, final, p0_ref, p1_ref, n0_ref, n1_ref, y_hbm, x_ref, mod_ref, meta_ref, fw_ref,
                    o_ref, buf, sem):
    step = pl.program_id(0) * pl.num_programs(1) + pl.program_id(1)
    n_steps = pl.num_programs(0) * pl.num_programs(1)
    slot = step % 2

    def gather(refs, t, s):
        for k in range(2):
            dst = buf.at[s, k]
            pltpu.make_async_copy(y_hbm.at[pl.ds(refs[k][0, 0, t], 1), :], dst.at[pl.ds(t, 1), :],
                                  sem.at[s]).start(priority=k)

    def drain(s):
        for k in range(2):
            pltpu.make_async_copy(y_hbm.at[pl.ds(0, tt), :], buf.at[s, k], sem.at[s]).wait()

    @pl.when(step == 0)
    def _():
        def body(t, carry):
            gather((p0_ref, p1_ref), t, 0)
            return carry

        lax.fori_loop(0, tt, body, 0, unroll=8)

    drain(slot)
    for t in range(tt):
        gather((n0_ref, n1_ref), t, 1 - slot)
    p1 = meta_ref[0, :, META_P:META_P + 1]
    p2 = meta_ref[0, :, META_P + 1:META_P + 2]
    out = x_ref[0] + mod_ref[0, 5:6, :] * (p1 * buf[slot, 0] + p2 * buf[slot, 1])
    o_ref[0] = _final_norm(out, fw_ref[...]) if final else out

    @pl.when(step == n_steps - 1)
    def _():
        drain(1 - slot)


def _combine(y, pos, x, mod, meta, fw, tt, final):
    bsz, seq, d = x.shape
    per_b = seq // tt
    last = bsz * per_b - 1
    cur = pl.BlockSpec((1, 1, tt), lambda b, i: (b * per_b + i, 0, 0), memory_space=pltpu.SMEM)
    nxt = pl.BlockSpec((1, 1, tt), lambda b, i: (jnp.minimum(b * per_b + i + 1, last), 0, 0),
                       memory_space=pltpu.SMEM)
    p0 = pos[0].reshape(bsz * per_b, 1, tt)
    p1 = pos[1].reshape(bsz * per_b, 1, tt)
    return pl.pallas_call(
        functools.partial(_combine_kernel, tt, final),
        grid=(bsz, per_b),
        in_specs=[
            cur,
            cur,
            nxt,
            nxt,
            pl.BlockSpec(memory_space=pl.ANY),
            pl.BlockSpec((1, tt, d), lambda b, i: (b, i, 0)),
            pl.BlockSpec((1, 6, d), lambda b, i: (b, 0, 0)),
            pl.BlockSpec((1, tt, LANES), lambda b, i: (b, i, 0)),
            pl.BlockSpec((1, d), lambda b, i: (0, 0)),
        ],
        out_specs=pl.BlockSpec((1, tt, d), lambda b, i: (b, i, 0)),
        out_shape=jax.ShapeDtypeStruct((bsz, seq, d), F32),
        scratch_shapes=[pltpu.VMEM((2, 2, tt, d), F32), pltpu.SemaphoreType.DMA((2,))],
        compiler_params=_params("arbitrary", "arbitrary"),
        name="combine",
    )(p0, p1, p0, p1, y, x, mod, meta, fw.reshape(1, d))


def _plan_kernel(tm, cnt_ref, rt_ref,
                 tile_o, exp_o, lo_o, hi_o, first_o, valid_o, slot_o, stage_o, next_o, pos_o):
    n_steps = tile_o.shape[0]
    shift = tm.bit_length() - 1
    i32 = jnp.int32
    counts = [cnt_ref[0, e] for e in range(N_EXPERTS)]
    starts = []
    acc = i32(0)
    for e in range(N_EXPERTS):
        starts.append(acc)
        acc = acc + counts[e]
    nxt = [None] * N_EXPERTS
    cur = i32(N_EXPERTS)
    for e in reversed(range(N_EXPERTS)):
        nxt[e] = cur
        cur = jnp.where(counts[e] > 0, i32(e), cur)

    for k in range(2):
        ex = rt_ref[META_E + k:META_E + k + 1, :]
        p = rt_ref[META_R + k:META_R + k + 1, :]
        for e in range(N_EXPERTS):
            p = p + jnp.where(ex == float(e), starts[e].astype(F32), 0.0)
        pos_o[k:k + 1, :] = p.astype(i32)

    for ref in (tile_o, exp_o, lo_o, hi_o, slot_o, next_o):
        ref[0] = i32(0)
    step = i32(0)
    order = i32(0)
    prev_tile = i32(-1)
    for e in range(N_EXPERTS):
        lo = starts[e]
        hi = lo + counts[e]
        t0 = lax.shift_right_logical(lo, shift)
        t1 = lax.shift_right_logical(jnp.maximum(hi - 1, 0), shift)
        n_e = jnp.where(counts[e] > 0, t1 - t0 + 1, 0)
        slot = order % 2
        has_next = (nxt[e] < N_EXPERTS).astype(i32)
        nxt_e = jnp.minimum(nxt[e], N_EXPERTS - 1)

        def visit(i, carry, e=e, lo=lo, hi=hi, t0=t0, slot=slot, has_next=has_next, nxt_e=nxt_e):
            step, prev_tile = carry
            tile = t0 + i
            tile_o[step] = tile
            exp_o[step] = i32(e)
            lo_o[step] = lo
            hi_o[step] = hi
            first_o[step] = (tile != prev_tile).astype(i32)
            valid_o[step] = i32(1)
            slot_o[step] = slot
            stage_o[step] = jnp.where(i == 0, has_next, 0)
            next_o[step] = nxt_e
            return step + 1, tile

        step, prev_tile = lax.fori_loop(0, n_e, visit, (step, prev_tile))
        order = order + (counts[e] > 0).astype(i32)

    last = jnp.maximum(step - 1, 0)

    def pad(i, carry):
        tile_o[i] = tile_o[last]
        exp_o[i] = exp_o[last]
        lo_o[i] = lo_o[last]
        hi_o[i] = hi_o[last]
        slot_o[i] = slot_o[last]
        next_o[i] = next_o[last]
        first_o[i] = i32(0)
        valid_o[i] = i32(0)
        stage_o[i] = i32(0)
        return carry

    lax.fori_loop(jnp.maximum(step, 1), n_steps, pad, 0)

    @pl.when(step == 0)
    def _():
        first_o[0] = i32(0)
        valid_o[0] = i32(0)
        stage_o[0] = i32(0)


def _plan(cnt, rt, n_rows, tm):
    assert tm & (tm - 1) == 0
    n_tok = rt.shape[1]
    n_steps = n_rows // tm + N_EXPERTS - 1
    smem = pl.BlockSpec(memory_space=pltpu.SMEM)
    vmem = pl.BlockSpec(memory_space=pltpu.VMEM)
    step_arr = jax.ShapeDtypeStruct((n_steps,), jnp.int32)
    out = pl.pallas_call(
        functools.partial(_plan_kernel, tm),
        in_specs=[smem, vmem],
        out_specs=[smem] * 9 + [vmem],
        out_shape=[step_arr] * 9 + [jax.ShapeDtypeStruct((2, n_tok), jnp.int32)],
        name="plan",
    )(cnt.astype(jnp.int32), rt)
    return tuple(out[:9]), out[9]


def _moe(x, mod, nw, fw, router_w, router_b, w1, w3, w2, tm_route, tt, tm_exp, final):
    bsz, seq, d = x.shape
    n_tok = bsz * seq
    h, meta, rt, cnt = _router(x, mod, nw, router_w, router_b, tm_route)
    steps, pos = _plan(cnt, rt, 2 * n_tok, tm_exp)
    xs = _dispatch(h, pos, _pick(n_tok, 2 * tt))
    y = _experts(xs, steps, w1, w3, w2, tm_exp)
    return _combine(y, pos, x, mod, meta, fw, tt, final)


def _pick(n, pref):
    t = min(n, pref)
    assert n % t == 0, (n, pref)
    return t


def kernel(x, c, ada_w, ada_b, norm_mix_w, norm_ffn_w, w_in, conv_w, hgrn_lb_logits, hgrn_norm_w,
           sgu_norm_w, sgu_w, sgu_b, w_br_a, w_br_b, w_br_c, w_o, ffn_w1, ffn_w3, ffn_w2,
           moe_router_w, moe_router_b, moe_w1, moe_w3, moe_w2, final_norm_w):
    depth = ada_w.shape[0]
    bsz, seq, d = x.shape
    assert w_in.shape[-1] == OFF_GATE + 3 * d and seq % CHUNK == 0
    assert ffn_w1.shape[-1] % MXU_N == 0 and moe_w1.shape[-1] % MXU_N == 0

    tm_in = _pick(seq, 512)
    ts = _pick(seq, 512)
    tm_ffn = _pick(seq, 1024)
    tm_route = _pick(seq, 512)
    tt = _pick(seq, 512)
    tm_exp = _pick(2 * bsz * seq, 512)

    mod_all = _adaln(c, ada_w, ada_b).reshape(depth, bsz, 6, d)
    tables = _level_tables()
    bf = lambda w: w.astype(BF16)
    w_in_bf, wa_bf, wb_bf, wc_bf, wo_bf = bf(w_in), bf(w_br_a), bf(w_br_b), bf(w_br_c), bf(w_o)

    for l in range(depth):
        mod = mod_all[l]
        ya, hq, gc, gates = _inproj(l, x, mod, norm_mix_w[l], w_in_bf, conv_w[l], hgrn_lb_logits,
                                    sgu_norm_w[l], tm_in)
        x = _mixer(l, ya, hq, gc, gates, x, mod, hgrn_norm_w[l], sgu_w[l], sgu_b[l], tables,
                   wa_bf, wb_bf, wc_bf, wo_bf, ts)
        final = l == depth - 1
        j = l // 2
        if l % 2 == 0:
            x = _ffn(x, mod, norm_ffn_w[l], final_norm_w, bf(ffn_w1[j]), bf(ffn_w3[j]), bf(ffn_w2[j]),
                     tm_ffn, final)
        else:
            x = _moe(x, mod, norm_ffn_w[l], final_norm_w, moe_router_w[j], moe_router_b[j],
                     moe_w1[j], moe_w3[j], moe_w2[j], tm_route, tt, tm_exp, final)
    return x
```
